```python
import math
import jax, jax.numpy as jnp
from jax import lax
import numpy as np

D_MODEL = 1024
BATCH = 8
SEQ = 4096
DEPTH = 1

HEAD_DIM = 64
DSWA_PATTERNS = ((128, 1), (512, 4), (2048, 16))
DSWA_HEADS_PER_GROUP = 4
DSWA_GROUPS = len(DSWA_PATTERNS)
DSWA_HEADS = DSWA_GROUPS * DSWA_HEADS_PER_GROUP
FOX_HEADS = 8
ATTN_BLOCK = 128
NUM_BUCKETS = 32
MAX_DISTANCE = 2048
N_EXPERTS = 32
TOP_K = 4
D_FF_EXPERT = D_MODEL
SWIGLU_LIMIT = 7.0
SWIGLU_ALPHA = 1.702
EXPERT_BLOCK = 256
RMS_EPS = 1e-6
NEG_INF = -1e30

DSWA_WIDTH = DSWA_HEADS * HEAD_DIM
DSWA_OUT_WIDTH = DSWA_HEADS_PER_GROUP * HEAD_DIM
FOX_WIDTH = FOX_HEADS * HEAD_DIM
IN_PROJ_WIDTH = 3 * DSWA_WIDTH + 3 * FOX_WIDTH + FOX_HEADS + 2 * D_MODEL

kernel_name = "hybrid_dilated_fox_moe_layer"


def rmsnorm(x, g):
    xf = x.astype(jnp.float32)
    y = xf * lax.rsqrt(jnp.mean(xf * xf, axis=-1, keepdims=True) + RMS_EPS)
    return (y * g.astype(jnp.float32)).astype(x.dtype)


def t5_bucket(n):
    max_exact = NUM_BUCKETS // 2
    n_safe = np.maximum(n, 1).astype(np.float64)
    large = max_exact + (np.log(n_safe / max_exact) / np.log(MAX_DISTANCE / max_exact)
                         * (NUM_BUCKETS - max_exact)).astype(np.int64)
    large = np.minimum(large, NUM_BUCKETS - 1)
    return np.where(n < max_exact, n, large).astype(np.int32)


def dilated_window_attention(q, k, v, bias_table, window, dilation):
    B, S, H, Dh = q.shape
    steps = window // dilation
    L = S // dilation
    c = math.gcd(L, ATTN_BLOCK)
    nblk = L // c
    span = c + steps

    def by_residue(t):
        return t.reshape(B, L, dilation, H, Dh).transpose(0, 2, 3, 1, 4)

    qs = by_residue(q).reshape(B, dilation, H, nblk, c, Dh)
    pad = ((0, 0), (0, 0), (0, 0), (steps, 0), (0, 0))
    kp = jnp.pad(by_residue(k), pad)
    vp = jnp.pad(by_residue(v), pad)
    key_idx = np.arange(nblk)[:, None] * c + np.arange(span)[None, :]
    kblk = kp[:, :, :, key_idx]
    vblk = vp[:, :, :, key_idx]
    back = np.arange(c)[:, None] + steps - np.arange(span)[None, :]
    valid = ((back >= 0) & (back <= steps))[None] & ((key_idx - steps)[:, None, :] >= 0)
    bucket = t5_bucket(np.clip(back, 0, steps) * dilation)
    bias = jnp.transpose(bias_table[bucket], (2, 0, 1)).astype(jnp.float32)
    s = jnp.einsum('bzhnqd,bzhnkd->bzhnqk', qs, kblk).astype(jnp.float32) * (Dh ** -0.5) + bias[:, None]
    s = jnp.where(valid, s, NEG_INF)
    m = jnp.max(s, axis=-1, keepdims=True)
    p = jnp.exp(s - m)
    den = jnp.sum(p, axis=-1, keepdims=True)
    o = jnp.einsum('bzhnqk,bzhnkd->bzhnqd', p.astype(v.dtype), vblk).astype(jnp.float32) / den
    lse = (m + jnp.log(den))[..., 0]
    o = o.reshape(B, dilation, H, L, Dh).transpose(0, 3, 1, 2, 4).reshape(B, S, H, Dh)
    lse = lse.reshape(B, dilation, H, L).transpose(0, 3, 1, 2).reshape(B, S, H)
    return o, lse


def forgetting_attention(q, k, v, log_f):
    B, S, H, Dh = q.shape
    nq = S // ATTN_BLOCK
    c = jnp.cumsum(log_f, axis=1)
    ck = c.transpose(0, 2, 1)
    qblk = q.reshape(B, nq, ATTN_BLOCK, H, Dh).transpose(1, 0, 2, 3, 4)
    cblk = c.reshape(B, nq, ATTN_BLOCK, H).transpose(1, 0, 3, 2)
    kpos = jnp.arange(S)

    def one_block(args):
        qb, cq, i = args
        s = jnp.einsum('bqhd,bkhd->bhqk', qb, k).astype(jnp.float32) * (Dh ** -0.5)
        s = s + cq[..., None] - ck[:, :, None, :]
        qpos = i * ATTN_BLOCK + jnp.arange(ATTN_BLOCK)
        s = jnp.where(qpos[:, None] >= kpos[None, :], s, NEG_INF)
        p = jax.nn.softmax(s, axis=-1)
        return jnp.einsum('bhqk,bkhd->bqhd', p.astype(v.dtype), v)

    o = lax.map(one_block, (qblk, cblk, jnp.arange(nq)))
    return o.transpose(1, 0, 2, 3, 4).reshape(B, S, H * Dh)


def routed_ffn(h, w_router, b_router, w_gate_up, b_gate_up, w_down, b_down):
    B, S, D = h.shape
    N = B * S
    xt = h.reshape(N, D)
    logits = (xt @ w_router + b_router).astype(jnp.float32)
    top_v, top_i = lax.top_k(logits, TOP_K)
    gate = jax.nn.softmax(top_v, axis=-1)
    A = N * TOP_K
    flat_e = top_i.reshape(-1)
    flat_tok = jnp.arange(A, dtype=jnp.int32) // TOP_K
    order = jnp.argsort(flat_e)
    se = flat_e[order]
    stok = flat_tok[order]
    sgate = gate.reshape(-1)[order]
    counts = jnp.zeros((N_EXPERTS,), jnp.int32).at[flat_e].add(1)
    starts = jnp.cumsum(counts) - counts
    pcounts = ((counts + EXPERT_BLOCK - 1) // EXPERT_BLOCK) * EXPERT_BLOCK
    pends = jnp.cumsum(pcounts)
    pstarts = pends - pcounts
    dest = pstarts[se] + jnp.arange(A, dtype=jnp.int32) - starts[se]
    P = A + N_EXPERTS * EXPERT_BLOCK
    nblk = P // EXPERT_BLOCK
    row_tok = jnp.full((P,), N, jnp.int32).at[dest].set(stok)
    row_gate = jnp.zeros((P,), jnp.float32).at[dest].set(sgate)
    blk_e = jnp.minimum(jnp.searchsorted(pends, jnp.arange(nblk) * EXPERT_BLOCK, side='right'),
                        N_EXPERTS - 1)
    x_pad = jnp.concatenate([xt, jnp.zeros((1, D), xt.dtype)], axis=0)

    def expert_block(args):
        tok, e = args
        xb = x_pad[tok]
        gu = xb @ w_gate_up[e] + b_gate_up[e]
        g, u = gu[:, :D_FF_EXPERT], gu[:, D_FF_EXPERT:]
        g = jnp.minimum(g, SWIGLU_LIMIT)
        u = jnp.clip(u, -SWIGLU_LIMIT, SWIGLU_LIMIT)
        mid = (u + 1) * (g * jax.nn.sigmoid(SWIGLU_ALPHA * g))
        return mid @ w_down[e] + b_down[e]

    yb = lax.map(expert_block, (row_tok.reshape(nblk, EXPERT_BLOCK), blk_e)).reshape(P, D)
    y = jnp.zeros((N + 1, D), h.dtype).at[row_tok].add(yb * row_gate[:, None].astype(yb.dtype))
    return y[:N].reshape(B, S, D)


def hybrid_layer(x, ln1_g, w_in, b_forget, b_gate, q_norm_a, k_norm_a, q_norm_b, k_norm_b, rel_bias,
                 w_branch_a, w_branch_b, w_out, ln2_g, w_router, b_router, w_gate_up, b_gate_up,
                 w_down, b_down):
    B, S, D = x.shape
    h = rmsnorm(x, ln1_g)
    proj = h @ w_in
    o0 = 0
    qa = proj[..., o0:o0 + DSWA_WIDTH]; o0 += DSWA_WIDTH
    ka = proj[..., o0:o0 + DSWA_WIDTH]; o0 += DSWA_WIDTH
    va = proj[..., o0:o0 + DSWA_WIDTH]; o0 += DSWA_WIDTH
    qb = proj[..., o0:o0 + FOX_WIDTH]; o0 += FOX_WIDTH
    kb = proj[..., o0:o0 + FOX_WIDTH]; o0 += FOX_WIDTH
    vb = proj[..., o0:o0 + FOX_WIDTH]; o0 += FOX_WIDTH
    f_logit = proj[..., o0:o0 + FOX_HEADS]; o0 += FOX_HEADS
    g_logit = proj[..., o0:o0 + 2 * D]

    shp_a = (B, S, DSWA_GROUPS, DSWA_HEADS_PER_GROUP, HEAD_DIM)
    qa = rmsnorm(qa.reshape(shp_a), q_norm_a)
    ka = rmsnorm(ka.reshape(shp_a), k_norm_a)
    va = va.reshape(shp_a)
    outs, lses = [], []
    for g, (window, dilation) in enumerate(DSWA_PATTERNS):
        cols = rel_bias[:, g * DSWA_HEADS_PER_GROUP:(g + 1) * DSWA_HEADS_PER_GROUP]
        o, l = dilated_window_attention(qa[:, :, g], ka[:, :, g], va[:, :, g], cols, window, dilation)
        outs.append(o)
        lses.append(l)
    mix = jax.nn.softmax(jnp.stack(lses, axis=0), axis=0)
    o_a = jnp.sum(mix[..., None] * jnp.stack(outs, axis=0), axis=0).astype(x.dtype)
    o_a = o_a.reshape(B, S, DSWA_OUT_WIDTH)

    shp_b = (B, S, FOX_HEADS, HEAD_DIM)
    qb = rmsnorm(qb.reshape(shp_b), q_norm_b)
    kb = rmsnorm(kb.reshape(shp_b), k_norm_b)
    log_f = jax.nn.log_sigmoid((f_logit + b_forget).astype(jnp.float32))
    o_b = forgetting_attention(qb, kb, vb.reshape(shp_b), log_f)

    gate = jax.nn.sigmoid((g_logit + b_gate).astype(jnp.float32)).astype(x.dtype)
    merged = gate[..., :D] * (o_a @ w_branch_a) + gate[..., D:] * (o_b @ w_branch_b)
    x = x + merged @ w_out
    x = x + routed_ffn(rmsnorm(x, ln2_g), w_router, b_router, w_gate_up, b_gate_up, w_down, b_down)
    return x


def setup_inputs(seed: int = 0) -> dict:
    key = jax.random.key(seed)
    ks = jax.random.split(key, 20)
    f32 = jnp.float32
    nrm = lambda k, shape, s: jax.random.normal(k, shape, f32) * s
    return {
        "x": nrm(ks[0], (BATCH, SEQ, D_MODEL), 1.0),
        "ln1_g": 1.0 + nrm(ks[1], (DEPTH, D_MODEL), 0.02),
        "w_in": nrm(ks[2], (DEPTH, D_MODEL, IN_PROJ_WIDTH), D_MODEL ** -0.5),
        "b_forget": jnp.linspace(1.0, 6.0, FOX_HEADS, dtype=f32)[None, :] + nrm(ks[3], (DEPTH, FOX_HEADS), 0.1),
        "b_gate": nrm(ks[4], (DEPTH, 2 * D_MODEL), 0.1),
        "q_norm_a": 1.0 + nrm(ks[5], (DEPTH, HEAD_DIM), 0.02),
        "k_norm_a": 1.0 + nrm(ks[6], (DEPTH, HEAD_DIM), 0.02),
        "q_norm_b": 1.0 + nrm(ks[7], (DEPTH, HEAD_DIM), 0.02),
        "k_norm_b": 1.0 + nrm(ks[8], (DEPTH, HEAD_DIM), 0.02),
        "rel_bias": nrm(ks[9], (NUM_BUCKETS, DSWA_HEADS), 0.5),
        "w_branch_a": nrm(ks[10], (DEPTH, DSWA_OUT_WIDTH, D_MODEL), DSWA_OUT_WIDTH ** -0.5),
        "w_branch_b": nrm(ks[11], (DEPTH, FOX_WIDTH, D_MODEL), FOX_WIDTH ** -0.5),
        "w_out": nrm(ks[12], (DEPTH, D_MODEL, D_MODEL), D_MODEL ** -0.5),
        "ln2_g": 1.0 + nrm(ks[13], (DEPTH, D_MODEL), 0.02),
        "w_router": nrm(ks[14], (DEPTH, D_MODEL, N_EXPERTS), D_MODEL ** -0.5),
        "b_router": nrm(ks[15], (DEPTH, N_EXPERTS), 0.01),
        "w_gate_up": nrm(ks[16], (DEPTH, N_EXPERTS, D_MODEL, 2 * D_FF_EXPERT), D_MODEL ** -0.5),
        "b_gate_up": nrm(ks[17], (DEPTH, N_EXPERTS, 2 * D_FF_EXPERT), 0.01),
        "w_down": nrm(ks[18], (DEPTH, N_EXPERTS, D_FF_EXPERT, D_MODEL), D_FF_EXPERT ** -0.5),
        "b_down": nrm(ks[19], (DEPTH, N_EXPERTS, D_MODEL), 0.01),
    }


def reference(x, ln1_g, w_in, b_forget, b_gate, q_norm_a, k_norm_a, q_norm_b, k_norm_b, rel_bias,
              w_branch_a, w_branch_b, w_out, ln2_g, w_router, b_router, w_gate_up, b_gate_up,
              w_down, b_down):
    for layer in range(DEPTH):
        x = hybrid_layer(x, ln1_g[layer], w_in[layer], b_forget[layer], b_gate[layer],
                         q_norm_a[layer], k_norm_a[layer], q_norm_b[layer], k_norm_b[layer], rel_bias,
                         w_branch_a[layer], w_branch_b[layer], w_out[layer], ln2_g[layer],
                         w_router[layer], b_router[layer], w_gate_up[layer], b_gate_up[layer],
                         w_down[layer], b_down[layer])
    return x
```

```python
import functools
import math

import numpy as np
import jax
import jax.numpy as jnp
from jax import lax
from jax.experimental import pallas as pl
from jax.experimental.pallas import tpu as pltpu

F32 = jnp.float32
BF16 = jnp.bfloat16

D_MODEL = 1024
HEAD_DIM = 64
DSWA_PATTERNS = ((128, 1), (512, 4), (2048, 16))
DSWA_HG = 4
DSWA_GROUPS = len(DSWA_PATTERNS)
DSWA_WIDTH = DSWA_GROUPS * DSWA_HG * HEAD_DIM
DSWA_OUT = DSWA_HG * HEAD_DIM
FOX_HEADS = 8
FOX_WIDTH = FOX_HEADS * HEAD_DIM
ATTN_BLOCK = 128
NUM_BUCKETS = 32
MAX_DISTANCE = 2048
N_EXPERTS = 32
TOP_K = 4
SWIGLU_LIMIT = 7.0
SWIGLU_ALPHA = 1.702
RMS_EPS = 1e-6
NEG_INF = -1e30

LANES = 128
SUBLANES = 8
ROW_CHUNKS = D_MODEL // LANES
QKV_WIDTH = 3 * DSWA_WIDTH + 3 * FOX_WIDTH
VMEM_LIMIT = 56 * 1024 * 1024

TM_PROJ = 256
TQ_FOX = 512
TM_MERGE = 256
T_EXPERT = 256
TC_MOVE = 512
DMA_UNROLL = 16


def _dot(a, b):
    return jnp.dot(a, b, preferred_element_type=F32)


def _dot_nt(a, b):
    return lax.dot_general(a, b, (((1,), (1,)), ((), ())), preferred_element_type=F32)


def _inproj_kernel(x_ref, g1_ref, wq_ref, wf_ref, wg_ref, bf_ref, bg_ref, nrm_ref, e_ref,
                   qkva_ref, qkvb_ref, lf_ref, gate_ref):
    x = x_ref[...]
    ms = jnp.mean(x * x, axis=-1, keepdims=True)
    h = (x * lax.rsqrt(ms + RMS_EPS) * g1_ref[...]).astype(BF16)
    ones_bd = e_ref[...]

    def head_norm(y, gn):
        sq = y * y
        hi = sq.astype(BF16)
        lo = (sq - hi.astype(F32)).astype(BF16)
        ss = _dot(hi, ones_bd) + _dot(lo, ones_bd)
        return y * lax.rsqrt(ss * (1.0 / HEAD_DIM) + RMS_EPS) * gn

    n_a = 3 * DSWA_WIDTH // 256
    for c in range(QKV_WIDTH // 256):
        y = _dot(h, wq_ref[:, c * 256:(c + 1) * 256])
        if c < 3:
            y = head_norm(y, nrm_ref[0:1, :])
        elif c < 6:
            y = head_norm(y, nrm_ref[1:2, :])
        elif 9 <= c < 11:
            y = head_norm(y, nrm_ref[2:3, :])
        elif 11 <= c < 13:
            y = head_norm(y, nrm_ref[3:4, :])
        if c < n_a:
            qkva_ref[:, c * 256:(c + 1) * 256] = y.astype(BF16)
        else:
            qkvb_ref[:, (c - n_a) * 256:(c - n_a + 1) * 256] = y.astype(BF16)

    z = _dot(h, wf_ref[...]) + bf_ref[...]
    lf_ref[...] = jnp.minimum(z, 0.0) - jnp.log1p(jnp.exp(-jnp.abs(z)))
    for c in range(2 * D_MODEL // 256):
        gl = _dot(h, wg_ref[:, c * 256:(c + 1) * 256]) + bg_ref[:, c * 256:(c + 1) * 256]
        gate_ref[:, c * 256:(c + 1) * 256] = jax.nn.sigmoid(gl)


def _in_projection(x2, ln1_g, w_in, b_forget, b_gate, q_norm_a, k_norm_a, q_norm_b, k_norm_b):
    n = x2.shape[0]
    wq = w_in[:, :QKV_WIDTH].astype(BF16)
    wf = jnp.pad(w_in[:, QKV_WIDTH:QKV_WIDTH + FOX_HEADS], ((0, 0), (0, LANES - FOX_HEADS))).astype(BF16)
    wg = w_in[:, QKV_WIDTH + FOX_HEADS:].astype(BF16)
    bf = jnp.pad(b_forget, (0, LANES - FOX_HEADS)).reshape(1, LANES)
    bg = b_gate.reshape(1, 2 * D_MODEL)
    scale = HEAD_DIM ** -0.5
    nrm = jnp.stack([jnp.tile(q_norm_a, 4) * scale, jnp.tile(k_norm_a, 4),
                     jnp.tile(q_norm_b, 4) * scale, jnp.tile(k_norm_b, 4)])
    head_id = np.arange(256) // HEAD_DIM
    ones_bd = jnp.asarray(head_id[:, None] == head_id[None, :], BF16)
    const = lambda shape: pl.BlockSpec(shape, lambda i: (0,) * len(shape))
    row = lambda w: pl.BlockSpec((TM_PROJ, w), lambda i: (i, 0))
    return pl.pallas_call(
        _inproj_kernel,
        grid=(n // TM_PROJ,),
        in_specs=[row(D_MODEL), const((1, D_MODEL)), const((D_MODEL, QKV_WIDTH)), const((D_MODEL, LANES)),
                  const((D_MODEL, 2 * D_MODEL)), const((1, LANES)), const((1, 2 * D_MODEL)),
                  const((4, 256)), const((256, 256))],
        out_specs=[row(3 * DSWA_WIDTH), row(3 * FOX_WIDTH), row(LANES), row(2 * D_MODEL)],
        out_shape=[jax.ShapeDtypeStruct((n, 3 * DSWA_WIDTH), BF16),
                   jax.ShapeDtypeStruct((n, 3 * FOX_WIDTH), BF16),
                   jax.ShapeDtypeStruct((n, LANES), F32),
                   jax.ShapeDtypeStruct((n, 2 * D_MODEL), F32)],
        compiler_params=pltpu.CompilerParams(dimension_semantics=("arbitrary",), vmem_limit_bytes=VMEM_LIMIT),
        name="in_projection",
    )(x2, ln1_g.reshape(1, D_MODEL), wq, wf, wg, bf, bg, nrm, ones_bd)


def _cumsum_kernel(lf_ref, c_ref):
    v = lf_ref[0]
    seq = v.shape[1]
    lane = lax.broadcasted_iota(jnp.int32, v.shape, 1)
    shift = 1
    while shift < seq:
        v = v + jnp.where(lane >= shift, pltpu.roll(v, shift, axis=1), 0.0)
        shift *= 2
    c_ref[0] = v


def _forget_cumsum(lf_rows):
    b, hds, seq = lf_rows.shape
    spec = pl.BlockSpec((1, hds, seq), lambda i: (i, 0, 0))
    return pl.pallas_call(
        _cumsum_kernel, grid=(b,), in_specs=[spec], out_specs=spec,
        out_shape=jax.ShapeDtypeStruct((b, hds, seq), F32), name="forget_cumsum",
    )(lf_rows)


def _t5_bucket(n):
    max_exact = NUM_BUCKETS // 2
    n_safe = np.maximum(n, 1).astype(np.float64)
    large = max_exact + (np.log(n_safe / max_exact) / np.log(MAX_DISTANCE / max_exact)
                         * (NUM_BUCKETS - max_exact)).astype(np.int64)
    large = np.minimum(large, NUM_BUCKETS - 1)
    return np.where(n < max_exact, n, large).astype(np.int32)


def _dswa_kernel(q_ref, kp_ref, kc_ref, vp_ref, vc_ref, bias_ref, o_ref, l_ref):
    has_prev = pl.program_id(2) > 0
    for h in range(DSWA_HG):
        hs = slice(h * HEAD_DIM, (h + 1) * HEAD_DIM)
        q = q_ref[0, :, hs]
        sp = _dot_nt(q, kp_ref[0, :, hs]) + bias_ref[h, :, :ATTN_BLOCK]
        sc = _dot_nt(q, kc_ref[0, :, hs]) + bias_ref[h, :, ATTN_BLOCK:]
        sp = jnp.where(has_prev, sp, NEG_INF)
        m = jnp.maximum(jnp.max(sp, axis=-1, keepdims=True), jnp.max(sc, axis=-1, keepdims=True))
        pp = jnp.exp(sp - m)
        pc = jnp.exp(sc - m)
        den = jnp.sum(pp, axis=-1, keepdims=True) + jnp.sum(pc, axis=-1, keepdims=True)
        o = _dot(pp.astype(BF16), vp_ref[0, :, hs]) + _dot(pc.astype(BF16), vc_ref[0, :, hs])
        o_ref[0, :, hs] = o / den
        l_ref[0, :, hs] = jnp.broadcast_to(m + jnp.log(den), (ATTN_BLOCK, HEAD_DIM))


def _dilated_attention(qkva, rel_bias, g, batch, seq):
    window, dil = DSWA_PATTERNS[g]
    steps = window // dil
    assert steps == ATTN_BLOCK
    length = seq // dil
    assert length % ATTN_BLOCK == 0
    nblk = length // ATTN_BLOCK
    span = 2 * ATTN_BLOCK
    back = np.arange(ATTN_BLOCK)[:, None] + steps - np.arange(span)[None, :]
    valid = (back >= 0) & (back <= steps)
    bucket = _t5_bucket(np.clip(back, 0, steps) * dil)
    cols = rel_bias[:, g * DSWA_HG:(g + 1) * DSWA_HG]
    bias = jnp.transpose(cols[bucket], (2, 0, 1)).astype(F32)
    bias = jnp.where(valid[None], bias, NEG_INF)

    per_pos = 3 * DSWA_WIDTH // 256
    qkv3 = qkva.reshape(batch, length, dil * 3 * DSWA_WIDTH)
    blk = (1, ATTN_BLOCK, 256)
    q_spec = pl.BlockSpec(blk, lambda b, r, n: (b, n, r * per_pos + g))
    kc_spec = pl.BlockSpec(blk, lambda b, r, n: (b, n, r * per_pos + 3 + g))
    kp_spec = pl.BlockSpec(blk, lambda b, r, n: (b, jnp.maximum(n - 1, 0), r * per_pos + 3 + g))
    vc_spec = pl.BlockSpec(blk, lambda b, r, n: (b, n, r * per_pos + 6 + g))
    vp_spec = pl.BlockSpec(blk, lambda b, r, n: (b, jnp.maximum(n - 1, 0), r * per_pos + 6 + g))
    bias_spec = pl.BlockSpec((DSWA_HG, ATTN_BLOCK, span), lambda b, r, n: (0, 0, 0))
    o_spec = pl.BlockSpec(blk, lambda b, r, n: (b, n, r))
    o, lse = pl.pallas_call(
        _dswa_kernel,
        grid=(batch, dil, nblk),
        in_specs=[q_spec, kp_spec, kc_spec, vp_spec, vc_spec, bias_spec],
        out_specs=[o_spec, o_spec],
        out_shape=[jax.ShapeDtypeStruct((batch, length, dil * DSWA_OUT), F32)] * 2,
        compiler_params=pltpu.CompilerParams(dimension_semantics=("arbitrary",) * 3),
        name=f"dilated_attention_{g}",
    )(qkv3, qkv3, qkv3, qkv3, qkv3, bias)
    n = batch * seq
    return o.reshape(n, DSWA_OUT), lse.reshape(n, DSWA_OUT)


def _fox_kernel(q_ref, k_ref, v_ref, cc_ref, cr_ref, o_ref, m_s, l_s, acc_s):
    qi = pl.program_id(1)
    tq = TQ_FOX
    row = lax.broadcasted_iota(jnp.int32, (tq, tq), 0)
    col = lax.broadcasted_iota(jnp.int32, (tq, tq), 1)
    causal = row >= col

    for h in range(FOX_HEADS):
        hs = slice(h * HEAD_DIM, (h + 1) * HEAD_DIM)
        q = q_ref[0, :, hs]
        cq = cc_ref[0, :, h:h + 1]
        m_s[...] = jnp.full((tq, 1), NEG_INF, F32)
        l_s[...] = jnp.zeros((tq, 1), F32)
        acc_s[...] = jnp.zeros((tq, HEAD_DIM), F32)

        def step(kb, masked):
            ks = pl.multiple_of(kb * tq, tq)
            k = k_ref[0, pl.ds(ks, tq), hs]
            v = v_ref[0, pl.ds(ks, tq), hs]
            ck = cr_ref[0, h:h + 1, pl.ds(ks, tq)]
            s = _dot_nt(q, k) + (cq - ck)
            if masked:
                s = jnp.where(causal, s, NEG_INF)
            m_prev = m_s[...]
            m_new = jnp.maximum(m_prev, jnp.max(s, axis=-1, keepdims=True))
            alpha = jnp.exp(m_prev - m_new)
            p = jnp.exp(s - m_new)
            l_s[...] = alpha * l_s[...] + jnp.sum(p, axis=-1, keepdims=True)
            acc_s[...] = alpha * acc_s[...] + _dot(p.astype(BF16), v)
            m_s[...] = m_new

        def body(kb, carry):
            step(kb, False)
            return carry

        lax.fori_loop(0, qi, body, 0)
        step(qi, True)
        o_ref[0, :, hs] = (acc_s[...] / l_s[...]).astype(BF16)


def _forgetting_attention(qkvb, c_rows, batch, seq):
    qkv3 = qkvb.reshape(batch, seq, 3 * FOX_WIDTH)
    c_cols = jnp.transpose(c_rows, (0, 2, 1))
    nq = seq // TQ_FOX
    o = pl.pallas_call(
        _fox_kernel,
        grid=(batch, nq),
        in_specs=[pl.BlockSpec((1, TQ_FOX, FOX_WIDTH), lambda b, i: (b, i, 0)),
                  pl.BlockSpec((1, seq, FOX_WIDTH), lambda b, i: (b, 0, 1)),
                  pl.BlockSpec((1, seq, FOX_WIDTH), lambda b, i: (b, 0, 2)),
                  pl.BlockSpec((1, TQ_FOX, FOX_HEADS), lambda b, i: (b, i, 0)),
                  pl.BlockSpec((1, FOX_HEADS, seq), lambda b, i: (b, 0, 0))],
        out_specs=pl.BlockSpec((1, TQ_FOX, FOX_WIDTH), lambda b, i: (b, i, 0)),
        out_shape=jax.ShapeDtypeStruct((batch, seq, FOX_WIDTH), BF16),
        scratch_shapes=[pltpu.VMEM((TQ_FOX, 1), F32), pltpu.VMEM((TQ_FOX, 1), F32),
                        pltpu.VMEM((TQ_FOX, HEAD_DIM), F32)],
        compiler_params=pltpu.CompilerParams(dimension_semantics=("arbitrary", "arbitrary"),
                                             vmem_limit_bytes=VMEM_LIMIT),
        name="forgetting_attention",
    )(qkv3, qkv3, qkv3, c_cols, c_rows)
    return o.reshape(batch * seq, FOX_WIDTH)


def _merge_kernel(x_ref, oa0_ref, oa1_ref, oa2_ref, ls0_ref, ls1_ref, ls2_ref, ob_ref, gate_ref,
                  wa_ref, wb_ref, wo_ref, g2_ref, wrh_ref, wrl_ref, br_ref,
                  x1_ref, h2_ref, ri_ref, rg_ref, cnt_ref, run_s):
    tm = TM_MERGE

    @pl.when(pl.program_id(0) == 0)
    def _():
        run_s[...] = jnp.zeros_like(run_s)

    l0, l1, l2 = ls0_ref[...], ls1_ref[...], ls2_ref[...]
    lm = jnp.maximum(jnp.maximum(l0, l1), l2)
    e0, e1, e2 = jnp.exp(l0 - lm), jnp.exp(l1 - lm), jnp.exp(l2 - lm)
    o_a = (e0 * oa0_ref[...] + e1 * oa1_ref[...] + e2 * oa2_ref[...]) / (e0 + e1 + e2)
    ya = _dot(o_a.astype(BF16), wa_ref[...])
    yb = _dot(ob_ref[...], wb_ref[...])
    merged = gate_ref[:, :D_MODEL] * ya + gate_ref[:, D_MODEL:] * yb
    x1 = x_ref[...] + _dot(merged.astype(BF16), wo_ref[...])
    x1_ref[...] = x1

    ms = jnp.mean(x1 * x1, axis=-1, keepdims=True)
    h2 = x1 * lax.rsqrt(ms + RMS_EPS) * g2_ref[...]
    for c in range(ROW_CHUNKS):
        h2_ref[pl.ds(c, tm, stride=ROW_CHUNKS), :] = h2[:, c * LANES:(c + 1) * LANES]

    hi = h2.astype(BF16)
    lo = (h2 - hi.astype(F32)).astype(BF16)
    logits = _dot(hi, wrh_ref[...]) + _dot(lo, wrh_ref[...]) + _dot(hi, wrl_ref[...]) + br_ref[...]

    lane = lax.broadcasted_iota(jnp.int32, (tm, LANES), 1).astype(F32)
    work = logits
    top_v, top_i = [], []
    for _ in range(TOP_K):
        mk = jnp.max(work, axis=-1, keepdims=True)
        ik = jnp.min(jnp.where(work == mk, lane, float(LANES)), axis=-1, keepdims=True)
        top_v.append(mk)
        top_i.append(ik)
        work = jnp.where(lane == ik, -jnp.inf, work)
    ex = [jnp.exp(v - top_v[0]) for v in top_v]
    den = ex[0] + ex[1] + ex[2] + ex[3]

    onehot = jnp.zeros((tm, LANES), F32)
    for ik in top_i:
        onehot = onehot + jnp.where(lane == ik, 1.0, 0.0)
    r_i = lax.broadcasted_iota(jnp.int32, (tm, tm), 0)
    c_i = lax.broadcasted_iota(jnp.int32, (tm, tm), 1)
    tri = jnp.where(c_i < r_i, 1.0, 0.0).astype(BF16)
    before = _dot(tri, onehot.astype(BF16)) + run_s[...]
    ri = jnp.zeros((tm, LANES), jnp.int32)
    rg = jnp.zeros((tm, LANES), F32)
    for k in range(TOP_K):
        rank = jnp.sum(jnp.where(lane == top_i[k], before, 0.0), axis=-1, keepdims=True)
        ri = jnp.where(lane == k, top_i[k].astype(jnp.int32), ri)
        ri = jnp.where(lane == TOP_K + k, rank.astype(jnp.int32), ri)
        rg = jnp.where(lane == k, ex[k] / den, rg)
    ri_ref[...] = ri
    rg_ref[...] = rg
    run = run_s[...] + jnp.sum(onehot, axis=0, keepdims=True)
    run_s[...] = run
    cnt_ref[...] = run


def _merge_and_route(x2, oas, lss, ob, gate, w_branch_a, w_branch_b, w_out, ln2_g, w_router, b_router):
    n = x2.shape[0]
    wr = jnp.pad(w_router, ((0, 0), (0, LANES - N_EXPERTS)))
    wr_hi = wr.astype(BF16)
    wr_lo = (wr - wr_hi.astype(F32)).astype(BF16)
    br = jnp.concatenate([b_router, jnp.full((LANES - N_EXPERTS,), NEG_INF, F32)]).reshape(1, LANES)
    const = lambda shape: pl.BlockSpec(shape, lambda i: (0,) * len(shape))
    row = lambda w: pl.BlockSpec((TM_MERGE, w), lambda i: (i, 0))
    return pl.pallas_call(
        _merge_kernel,
        grid=(n // TM_MERGE,),
        in_specs=[row(D_MODEL)] + [row(DSWA_OUT)] * 6 + [row(FOX_WIDTH), row(2 * D_MODEL),
                  const((DSWA_OUT, D_MODEL)), const((FOX_WIDTH, D_MODEL)), const((D_MODEL, D_MODEL)),
                  const((1, D_MODEL)), const((D_MODEL, LANES)), const((D_MODEL, LANES)), const((1, LANES))],
        out_specs=[row(D_MODEL), pl.BlockSpec((TM_MERGE * ROW_CHUNKS, LANES), lambda i: (i, 0)),
                   row(LANES), row(LANES), const((1, LANES))],
        out_shape=[jax.ShapeDtypeStruct((n, D_MODEL), F32),
                   jax.ShapeDtypeStruct((n * ROW_CHUNKS, LANES), F32),
                   jax.ShapeDtypeStruct((n, LANES), jnp.int32),
                   jax.ShapeDtypeStruct((n, LANES), F32),
                   jax.ShapeDtypeStruct((1, LANES), F32)],
        scratch_shapes=[pltpu.VMEM((1, LANES), F32)],
        compiler_params=pltpu.CompilerParams(dimension_semantics=("arbitrary",), vmem_limit_bytes=VMEM_LIMIT),
        name="merge_and_route",
    )(x2, *oas, *lss, ob, gate, w_branch_a.astype(BF16), w_branch_b.astype(BF16), w_out.astype(BF16),
      ln2_g.reshape(1, D_MODEL), wr_hi, wr_lo, br)


def _row_copy(src, src_row, dst, dst_row, sem):
    return pltpu.make_async_copy(src.at[pl.ds(pl.multiple_of(src_row * ROW_CHUNKS, ROW_CHUNKS), ROW_CHUNKS)],
                                 dst.at[pl.ds(pl.multiple_of(dst_row * ROW_CHUNKS, ROW_CHUNKS), ROW_CHUNKS)], sem)


def _dispatch_kernel(zb_ref, zf_ref, dest_hbm, h2_hbm, xs_hbm, idx_s, zeros_v, sem_i, sem_z, sem_r):
    i = pl.program_id(0)
    n_move = TOP_K * TC_MOVE
    idx_copy = pltpu.make_async_copy(dest_hbm.at[i], idx_s, sem_i)
    idx_copy.start()

    def zero_copy(e):
        blk = pl.multiple_of(zb_ref[e] * (T_EXPERT * ROW_CHUNKS), T_EXPERT * ROW_CHUNKS)
        return pltpu.make_async_copy(zeros_v, xs_hbm.at[pl.ds(blk, T_EXPERT * ROW_CHUNKS)], sem_z)

    @pl.when(i == 0)
    def _():
        zeros_v[...] = jnp.zeros_like(zeros_v)
        for e in range(N_EXPERTS):
            @pl.when(zf_ref[e] > 0)
            def _():
                zero_copy(e).start()
        for e in range(N_EXPERTS):
            @pl.when(zf_ref[e] > 0)
            def _():
                zero_copy(e).wait()

    idx_copy.wait()
    base = i * TC_MOVE

    def issue(jo, carry):
        for u in range(DMA_UNROLL):
            j = jo * DMA_UNROLL + u
            tok = base + (j & (TC_MOVE - 1))
            _row_copy(h2_hbm, tok, xs_hbm, idx_s[j], sem_r).start()
        return carry

    lax.fori_loop(0, n_move // DMA_UNROLL, issue, 0)

    def drain(jo, carry):
        for u in range(DMA_UNROLL):
            _row_copy(h2_hbm, 0, xs_hbm, 0, sem_r).wait()
        return carry

    lax.fori_loop(0, n_move // DMA_UNROLL, drain, 0)


def _dispatch(dest_tiles, zero_blk, zero_flag, h2_rows, n_slots):
    nt = dest_tiles.shape[0]
    return pl.pallas_call(
        _dispatch_kernel,
        grid_spec=pltpu.PrefetchScalarGridSpec(
            num_scalar_prefetch=2, grid=(nt,),
            in_specs=[pl.BlockSpec(memory_space=pl.ANY), pl.BlockSpec(memory_space=pl.ANY)],
            out_specs=pl.BlockSpec(memory_space=pl.ANY),
            scratch_shapes=[pltpu.SMEM((TOP_K * TC_MOVE,), jnp.int32),
                            pltpu.VMEM((T_EXPERT * ROW_CHUNKS, LANES), F32),
                            pltpu.SemaphoreType.DMA, pltpu.SemaphoreType.DMA, pltpu.SemaphoreType.DMA]),
        out_shape=jax.ShapeDtypeStruct((n_slots * ROW_CHUNKS, LANES), F32),
        compiler_params=pltpu.CompilerParams(dimension_semantics=("arbitrary",)),
        name="dispatch_rows",
    )(zero_blk, zero_flag, dest_tiles, h2_rows)


def _expert_kernel(be_ref, nu_ref, xs_ref, wgu_ref, bgu_ref, wd_ref, bd_ref, ys_ref, wgu_s, wd_s):
    j = pl.program_id(0)
    t = T_EXPERT

    @pl.when(j < nu_ref[0])
    def _():
        prev = be_ref[jnp.maximum(j - 1, 0)]

        @pl.when((j == 0) | (be_ref[j] != prev))
        def _():
            wgu_s[...] = wgu_ref[0].astype(BF16)
            wd_s[...] = wd_ref[0].astype(BF16)

        x = jnp.concatenate([xs_ref[pl.ds(c, t, stride=ROW_CHUNKS), :] for c in range(ROW_CHUNKS)], axis=1)
        gu = _dot(x.astype(BF16), wgu_s[...]) + bgu_ref[0]
        g = jnp.minimum(gu[:, :D_MODEL], SWIGLU_LIMIT)
        u = jnp.clip(gu[:, D_MODEL:], -SWIGLU_LIMIT, SWIGLU_LIMIT)
        mid = (u + 1.0) * (g * jax.nn.sigmoid(SWIGLU_ALPHA * g))
        y = _dot(mid.astype(BF16), wd_s[...]) + bd_ref[0]
        for c in range(ROW_CHUNKS):
            ys_ref[pl.ds(c, t, stride=ROW_CHUNKS), :] = y[:, c * LANES:(c + 1) * LANES]


def _expert_ffn(blk_e, n_used, xs_rows, w_gate_up, b_gate_up, w_down, b_down):
    n_blk = blk_e.shape[0]
    rows = T_EXPERT * ROW_CHUNKS
    live = lambda j, be, nu: jnp.minimum(j, nu[0] - 1)
    sink = lambda j, be, nu: jnp.where(j < nu[0], j, n_blk - 1)
    return pl.pallas_call(
        _expert_kernel,
        grid_spec=pltpu.PrefetchScalarGridSpec(
            num_scalar_prefetch=2, grid=(n_blk,),
            in_specs=[pl.BlockSpec((rows, LANES), lambda j, be, nu: (live(j, be, nu), 0)),
                      pl.BlockSpec((1, D_MODEL, 2 * D_MODEL), lambda j, be, nu: (be[live(j, be, nu)], 0, 0)),
                      pl.BlockSpec((1, 1, 2 * D_MODEL), lambda j, be, nu: (be[live(j, be, nu)], 0, 0)),
                      pl.BlockSpec((1, D_MODEL, D_MODEL), lambda j, be, nu: (be[live(j, be, nu)], 0, 0)),
                      pl.BlockSpec((1, 1, D_MODEL), lambda j, be, nu: (be[live(j, be, nu)], 0, 0))],
            out_specs=pl.BlockSpec((rows, LANES), lambda j, be, nu: (sink(j, be, nu), 0)),
            scratch_shapes=[pltpu.VMEM((D_MODEL, 2 * D_MODEL), BF16), pltpu.VMEM((D_MODEL, D_MODEL), BF16)]),
        out_shape=jax.ShapeDtypeStruct(xs_rows.shape, F32),
        compiler_params=pltpu.CompilerParams(dimension_semantics=("arbitrary",), vmem_limit_bytes=VMEM_LIMIT),
        name="expert_ffn",
    )(blk_e, n_used, xs_rows, w_gate_up, b_gate_up.reshape(N_EXPERTS, 1, 2 * D_MODEL),
      w_down, b_down.reshape(N_EXPERTS, 1, D_MODEL))


def _combine_kernel(dest_hbm, ys_hbm, x1_ref, rg_ref, o_ref, idx_s, buf, sem_i, sem_r):
    i = pl.program_id(0)
    n_move = TOP_K * TC_MOVE
    idx_copy = pltpu.make_async_copy(dest_hbm.at[i], idx_s, sem_i)
    idx_copy.start()
    idx_copy.wait()

    def issue(jo, carry):
        for u in range(DMA_UNROLL):
            j = jo * DMA_UNROLL + u
            _row_copy(ys_hbm, idx_s[j], buf, j, sem_r).start()
        return carry

    lax.fori_loop(0, n_move // DMA_UNROLL, issue, 0)

    def drain(jo, carry):
        for u in range(DMA_UNROLL):
            _row_copy(ys_hbm, 0, buf, 0, sem_r).wait()
        return carry

    lax.fori_loop(0, n_move // DMA_UNROLL, drain, 0)

    for c in range(ROW_CHUNKS):
        acc = x1_ref[:, c * LANES:(c + 1) * LANES]
        for k in range(TOP_K):
            rows = buf[pl.ds(k * TC_MOVE * ROW_CHUNKS + c, TC_MOVE, stride=ROW_CHUNKS), :]
            acc = acc + rg_ref[:, k:k + 1] * rows
        o_ref[:, c * LANES:(c + 1) * LANES] = acc


def _combine(dest_tiles, ys_rows, x1, rg):
    n = x1.shape[0]
    nt = dest_tiles.shape[0]
    return pl.pallas_call(
        _combine_kernel,
        grid=(nt,),
        in_specs=[pl.BlockSpec(memory_space=pl.ANY), pl.BlockSpec(memory_space=pl.ANY),
                  pl.BlockSpec((TC_MOVE, D_MODEL), lambda i: (i, 0)),
                  pl.BlockSpec((TC_MOVE, LANES), lambda i: (i, 0))],
        out_specs=pl.BlockSpec((TC_MOVE, D_MODEL), lambda i: (i, 0)),
        out_shape=jax.ShapeDtypeStruct((n, D_MODEL), F32),
        scratch_shapes=[pltpu.SMEM((TOP_K * TC_MOVE,), jnp.int32),
                        pltpu.VMEM((TOP_K * TC_MOVE * ROW_CHUNKS, LANES), F32),
                        pltpu.SemaphoreType.DMA, pltpu.SemaphoreType.DMA],
        compiler_params=pltpu.CompilerParams(dimension_semantics=("arbitrary",), vmem_limit_bytes=VMEM_LIMIT),
        name="combine_rows",
    )(dest_tiles, ys_rows, x1, rg)


def _layer(x, ln1_g, w_in, b_forget, b_gate, q_norm_a, k_norm_a, q_norm_b, k_norm_b, rel_bias,
           w_branch_a, w_branch_b, w_out, ln2_g, w_router, b_router, w_gate_up, b_gate_up, w_down, b_down):
    batch, seq, _ = x.shape
    n = batch * seq
    x2 = x.reshape(n, D_MODEL)
    qkva, qkvb, lf, gate = _in_projection(x2, ln1_g, w_in, b_forget, b_gate,
                                          q_norm_a, k_norm_a, q_norm_b, k_norm_b)

    oas, lss = [], []
    for g in range(DSWA_GROUPS):
        o, lse = _dilated_attention(qkva, rel_bias, g, batch, seq)
        oas.append(o)
        lss.append(lse)

    lf_rows = jnp.transpose(lf[:, :FOX_HEADS].reshape(batch, seq, FOX_HEADS), (0, 2, 1))
    c_rows = _forget_cumsum(lf_rows)
    ob = _forgetting_attention(qkvb, c_rows, batch, seq)

    x1, h2_rows, ri, rg, cnt = _merge_and_route(x2, oas, lss, ob, gate, w_branch_a, w_branch_b, w_out,
                                                ln2_g, w_router, b_router)

    counts = cnt[0, :N_EXPERTS].astype(jnp.int32)
    pcounts = ((counts + T_EXPERT - 1) // T_EXPERT) * T_EXPERT
    pends = jnp.cumsum(pcounts)
    pstarts = pends - pcounts
    idx, rank = ri[:, :TOP_K], ri[:, TOP_K:2 * TOP_K]
    start_of = jnp.sum(jnp.where(idx[..., None] == jnp.arange(N_EXPERTS), pstarts, 0), axis=-1)
    dest = start_of + rank
    nt = n // TC_MOVE
    dest_tiles = jnp.transpose(dest.reshape(nt, TC_MOVE, TOP_K), (0, 2, 1)).reshape(nt, TOP_K * TC_MOVE)
    n_slots = n * TOP_K + N_EXPERTS * T_EXPERT
    n_blk = n_slots // T_EXPERT
    blk_e = jnp.minimum(jnp.searchsorted(pends, jnp.arange(n_blk, dtype=jnp.int32) * T_EXPERT, side='right'),
                        N_EXPERTS - 1).astype(jnp.int32)
    n_used = (pends[-1:] // T_EXPERT).astype(jnp.int32)
    zero_blk = jnp.maximum(pends // T_EXPERT - 1, 0).astype(jnp.int32)
    zero_flag = (pcounts > 0).astype(jnp.int32)

    xs_rows = _dispatch(dest_tiles, zero_blk, zero_flag, h2_rows, n_slots)
    ys_rows = _expert_ffn(blk_e, n_used, xs_rows, w_gate_up, b_gate_up, w_down, b_down)
    out = _combine(dest_tiles, ys_rows, x1, rg)
    return out.reshape(batch, seq, D_MODEL)


def kernel(x, ln1_g, w_in, b_forget, b_gate, q_norm_a, k_norm_a, q_norm_b, k_norm_b, rel_bias, w_branch_a, w_branch_b, w_out, ln2_g, w_router, b_router, w_gate_up, b_gate_up, w_down, b_down):
    for layer in range(ln1_g.shape[0]):
        x = _layer(x, ln1_g[layer], w_in[layer], b_forget[layer], b_gate[layer], q_norm_a[layer],
                   k_norm_a[layer], q_norm_b[layer], k_norm_b[layer], rel_bias, w_branch_a[layer],
                   w_branch_b[layer], w_out[layer], ln2_g[layer], w_router[layer], b_router[layer],
                   w_gate_up[layer], b_gate_up[layer], w_down[layer], b_down[layer])
    return x
```

```python
import functools
import math

import numpy as np
import jax
import jax.numpy as jnp
from jax import lax
from jax.experimental import pallas as pl
from jax.experimental.pallas import tpu as pltpu

F32 = jnp.float32
BF16 = jnp.bfloat16

D_MODEL = 1024
HEAD_DIM = 64
DSWA_PATTERNS = ((128, 1), (512, 4), (2048, 16))
DSWA_HG = 4
DSWA_GROUPS = len(DSWA_PATTERNS)
DSWA_WIDTH = DSWA_GROUPS * DSWA_HG * HEAD_DIM
DSWA_OUT = DSWA_HG * HEAD_DIM
FOX_HEADS = 8
FOX_WIDTH = FOX_HEADS * HEAD_DIM
ATTN_BLOCK = 128
NUM_BUCKETS = 32
MAX_DISTANCE = 2048
N_EXPERTS = 32
TOP_K = 4
SWIGLU_LIMIT = 7.0
SWIGLU_ALPHA = 1.702
RMS_EPS = 1e-6
NEG_INF = -1e30
LOG2E = 1.4426950408889634

LANES = 128
SUBLANES = 8
ROW_CHUNKS = D_MODEL // LANES
QKV_WIDTH = 3 * DSWA_WIDTH + 3 * FOX_WIDTH
VMEM_LIMIT = 56 * 1024 * 1024

TM_PROJ = 256
TQ_FOX = 512
FOX_HEADS_PER_LOOP = 2
TM_MERGE = 256
T_EXPERT = 256
TC_MOVE = 512
DMA_UNROLL = 16


def _dot(a, b):
    return jnp.dot(a, b, preferred_element_type=F32)


def _dot_nt(a, b):
    return lax.dot_general(a, b, (((1,), (1,)), ((), ())), preferred_element_type=F32)


def _inproj_kernel(x_ref, g1_ref, wq_ref, wf_ref, wg_ref, bf_ref, bg_ref, nrm_ref, e_ref,
                   qkv0_ref, qkv1_ref, qkv2_ref, qkvb_ref, lf_ref, gate_ref, stage_ref):
    group_refs = (qkv0_ref, qkv1_ref, qkv2_ref)
    x = x_ref[...]
    ms = jnp.mean(x * x, axis=-1, keepdims=True)
    h = (x * lax.rsqrt(ms + RMS_EPS) * g1_ref[...]).astype(BF16)
    ones_bd = e_ref[...]

    def head_norm(y, gn):
        sq = y * y
        hi = sq.astype(BF16)
        lo = (sq - hi.astype(F32)).astype(BF16)
        ss = _dot(hi, ones_bd) + _dot(lo, ones_bd)
        return y * lax.rsqrt(ss * (1.0 / HEAD_DIM) + RMS_EPS) * gn

    n_a = 3 * DSWA_WIDTH // 256
    for c in range(QKV_WIDTH // 256):
        y = _dot(h, wq_ref[:, c * 256:(c + 1) * 256])
        if c < 3:
            y = head_norm(y, nrm_ref[0:1, :])
        elif c < 6:
            y = head_norm(y, nrm_ref[1:2, :])
        elif 9 <= c < 11:
            y = head_norm(y, nrm_ref[2:3, :])
        elif 11 <= c < 13:
            y = head_norm(y, nrm_ref[3:4, :])
        if c < n_a:
            g, part = c % DSWA_GROUPS, c // DSWA_GROUPS
            dil = DSWA_PATTERNS[g][1]
            if dil == 1:
                group_refs[g][0, :, part * 256:(part + 1) * 256] = y.astype(BF16)
            else:
                for half in range(2):
                    stage_ref[half] = y[:, half * LANES:(half + 1) * LANES]
                for r in range(dil):
                    for half in range(2):
                        col = r * DSWA_WIDTH + part * 256 + half * LANES
                        rows = stage_ref[half, pl.ds(r, TM_PROJ // dil, stride=dil), :]
                        group_refs[g][0, :, col:col + LANES] = rows.astype(BF16)
        else:
            qkvb_ref[:, (c - n_a) * 256:(c - n_a + 1) * 256] = y.astype(BF16)

    z = _dot(h, wf_ref[...]) + bf_ref[...]
    lf_ref[...] = jnp.minimum(z, 0.0) - jnp.log1p(jnp.exp(-jnp.abs(z)))
    for c in range(2 * D_MODEL // 256):
        gl = _dot(h, wg_ref[:, c * 256:(c + 1) * 256]) + bg_ref[:, c * 256:(c + 1) * 256]
        gate_ref[:, c * 256:(c + 1) * 256] = jax.nn.sigmoid(gl)


def _in_projection(x2, batch, seq, ln1_g, w_in, b_forget, b_gate, q_norm_a, k_norm_a, q_norm_b, k_norm_b):
    n = x2.shape[0]
    assert seq % TM_PROJ == 0 and all(TM_PROJ % (16 * d) == 0 for _, d in DSWA_PATTERNS)
    tiles_per_seq = seq // TM_PROJ
    wq = w_in[:, :QKV_WIDTH].astype(BF16)
    wf = jnp.pad(w_in[:, QKV_WIDTH:QKV_WIDTH + FOX_HEADS], ((0, 0), (0, LANES - FOX_HEADS))).astype(BF16)
    wg = w_in[:, QKV_WIDTH + FOX_HEADS:].astype(BF16)
    bf = jnp.pad(b_forget, (0, LANES - FOX_HEADS)).reshape(1, LANES)
    bg = b_gate.reshape(1, 2 * D_MODEL)
    scale = HEAD_DIM ** -0.5
    nrm = jnp.stack([jnp.tile(q_norm_a, 4) * scale, jnp.tile(k_norm_a, 4),
                     jnp.tile(q_norm_b, 4) * (scale * LOG2E), jnp.tile(k_norm_b, 4)])
    head_id = np.arange(256) // HEAD_DIM
    ones_bd = jnp.asarray(head_id[:, None] == head_id[None, :], BF16)
    const = lambda shape: pl.BlockSpec(shape, lambda i: (0,) * len(shape))
    row = lambda w: pl.BlockSpec((TM_PROJ, w), lambda i: (i, 0))
    group_spec = lambda d: pl.BlockSpec((1, TM_PROJ // d, d * DSWA_WIDTH),
                                        lambda i: (i // tiles_per_seq, i % tiles_per_seq, 0))
    group_shape = lambda d: jax.ShapeDtypeStruct((batch, seq // d, d * DSWA_WIDTH), BF16)
    dils = [d for _, d in DSWA_PATTERNS]
    return pl.pallas_call(
        _inproj_kernel,
        grid=(n // TM_PROJ,),
        in_specs=[row(D_MODEL), const((1, D_MODEL)), const((D_MODEL, QKV_WIDTH)), const((D_MODEL, LANES)),
                  const((D_MODEL, 2 * D_MODEL)), const((1, LANES)), const((1, 2 * D_MODEL)),
                  const((4, 256)), const((256, 256))],
        out_specs=[group_spec(d) for d in dils] + [row(3 * FOX_WIDTH), row(LANES), row(2 * D_MODEL)],
        out_shape=[group_shape(d) for d in dils] + [jax.ShapeDtypeStruct((n, 3 * FOX_WIDTH), BF16),
                                                    jax.ShapeDtypeStruct((n, LANES), F32),
                                                    jax.ShapeDtypeStruct((n, 2 * D_MODEL), F32)],
        scratch_shapes=[pltpu.VMEM((2, TM_PROJ, LANES), F32)],
        compiler_params=pltpu.CompilerParams(dimension_semantics=("arbitrary",), vmem_limit_bytes=VMEM_LIMIT),
        name="in_projection",
    )(x2, ln1_g.reshape(1, D_MODEL), wq, wf, wg, bf, bg, nrm, ones_bd)


def _cumsum_kernel(lf_ref, c_ref):
    v = lf_ref[0]
    seq = v.shape[1]
    lane = lax.broadcasted_iota(jnp.int32, v.shape, 1)
    shift = 1
    while shift < seq:
        v = v + jnp.where(lane >= shift, pltpu.roll(v, shift, axis=1), 0.0)
        shift *= 2
    c_ref[0] = v


def _forget_cumsum(lf_rows):
    b, hds, seq = lf_rows.shape
    spec = pl.BlockSpec((1, hds, seq), lambda i: (i, 0, 0))
    return pl.pallas_call(
        _cumsum_kernel, grid=(b,), in_specs=[spec], out_specs=spec,
        out_shape=jax.ShapeDtypeStruct((b, hds, seq), F32), name="forget_cumsum",
    )(lf_rows)


def _t5_bucket(n):
    max_exact = NUM_BUCKETS // 2
    n_safe = np.maximum(n, 1).astype(np.float64)
    large = max_exact + (np.log(n_safe / max_exact) / np.log(MAX_DISTANCE / max_exact)
                         * (NUM_BUCKETS - max_exact)).astype(np.int64)
    large = np.minimum(large, NUM_BUCKETS - 1)
    return np.where(n < max_exact, n, large).astype(np.int32)


def _dswa_kernel(q_ref, kp_ref, kc_ref, vp_ref, vc_ref, bias_ref, o_ref, l_ref):
    has_prev = pl.program_id(2) > 0
    for h in range(DSWA_HG):
        hs = slice(h * HEAD_DIM, (h + 1) * HEAD_DIM)
        q = q_ref[0, :, hs]
        sp = _dot_nt(q, kp_ref[0, :, hs]) + bias_ref[h, :, :ATTN_BLOCK]
        sc = _dot_nt(q, kc_ref[0, :, hs]) + bias_ref[h, :, ATTN_BLOCK:]
        sp = jnp.where(has_prev, sp, NEG_INF)
        m = jnp.maximum(jnp.max(sp, axis=-1, keepdims=True), jnp.max(sc, axis=-1, keepdims=True))
        pp = jnp.exp(sp - m)
        pc = jnp.exp(sc - m)
        den = jnp.sum(pp, axis=-1, keepdims=True) + jnp.sum(pc, axis=-1, keepdims=True)
        o = _dot(pp.astype(BF16), vp_ref[0, :, hs]) + _dot(pc.astype(BF16), vc_ref[0, :, hs])
        o_ref[0, :, hs] = o / den
        l_ref[0, :, hs] = jnp.broadcast_to(m + jnp.log(den), (ATTN_BLOCK, HEAD_DIM))


def _dilated_attention(qkv3, rel_bias, g, batch, seq):
    window, dil = DSWA_PATTERNS[g]
    steps = window // dil
    assert steps == ATTN_BLOCK
    length = seq // dil
    assert length % ATTN_BLOCK == 0
    nblk = length // ATTN_BLOCK
    span = 2 * ATTN_BLOCK
    back = np.arange(ATTN_BLOCK)[:, None] + steps - np.arange(span)[None, :]
    valid = (back >= 0) & (back <= steps)
    bucket = _t5_bucket(np.clip(back, 0, steps) * dil)
    cols = rel_bias[:, g * DSWA_HG:(g + 1) * DSWA_HG]
    onehot = np.asarray(bucket.reshape(-1)[:, None] == np.arange(NUM_BUCKETS)[None, :], np.float32)
    bias = jnp.dot(jnp.asarray(onehot), cols.astype(F32), precision=lax.Precision.HIGHEST)
    bias = jnp.transpose(bias.reshape(ATTN_BLOCK, span, DSWA_HG), (2, 0, 1))
    bias = jnp.where(valid[None], bias, NEG_INF)

    per_pos = DSWA_WIDTH // 256
    blk = (1, ATTN_BLOCK, 256)
    q_spec = pl.BlockSpec(blk, lambda b, r, n: (b, n, r * per_pos))
    kc_spec = pl.BlockSpec(blk, lambda b, r, n: (b, n, r * per_pos + 1))
    kp_spec = pl.BlockSpec(blk, lambda b, r, n: (b, jnp.maximum(n - 1, 0), r * per_pos + 1))
    vc_spec = pl.BlockSpec(blk, lambda b, r, n: (b, n, r * per_pos + 2))
    vp_spec = pl.BlockSpec(blk, lambda b, r, n: (b, jnp.maximum(n - 1, 0), r * per_pos + 2))
    bias_spec = pl.BlockSpec((DSWA_HG, ATTN_BLOCK, span), lambda b, r, n: (0, 0, 0))
    o_spec = pl.BlockSpec(blk, lambda b, r, n: (b, n, r))
    return pl.pallas_call(
        _dswa_kernel,
        grid=(batch, dil, nblk),
        in_specs=[q_spec, kp_spec, kc_spec, vp_spec, vc_spec, bias_spec],
        out_specs=[o_spec, o_spec],
        out_shape=[jax.ShapeDtypeStruct((batch, length, dil * DSWA_OUT), F32)] * 2,
        compiler_params=pltpu.CompilerParams(dimension_semantics=("arbitrary",) * 3),
        name=f"dilated_attention_{g}",
    )(qkv3, qkv3, qkv3, qkv3, qkv3, bias)


def _fox_kernel(q_ref, k_ref, vt_ref, o_ref):
    qi = pl.program_id(1)
    tq = TQ_FOX
    krow = lax.broadcasted_iota(jnp.int32, (tq, tq), 0)
    qcol = lax.broadcasted_iota(jnp.int32, (tq, tq), 1)
    causal = krow <= qcol

    for hg in range(FOX_HEADS // FOX_HEADS_PER_LOOP):
        heads = [hg * FOX_HEADS_PER_LOOP + i for i in range(FOX_HEADS_PER_LOOP)]
        qs = [q_ref[0, :, h * LANES:(h + 1) * LANES] for h in heads]

        def step(kb, carry, masked):
            ks = pl.multiple_of(kb * tq, tq)
            out = []
            for i, h in enumerate(heads):
                m, l, acc = carry[i]
                k = k_ref[0, pl.ds(ks, tq), h * LANES:(h + 1) * LANES]
                vt = vt_ref[0, h * HEAD_DIM:(h + 1) * HEAD_DIM, pl.ds(ks, tq)]
                st = _dot_nt(k, qs[i])
                if masked:
                    st = jnp.where(causal, st, NEG_INF)
                m_new = jnp.maximum(m, jnp.max(st, axis=0, keepdims=True))
                alpha = jnp.exp2(m - m_new)
                p = jnp.exp2(st - m_new)
                l = alpha * l + jnp.sum(p, axis=0, keepdims=True)
                acc = alpha * acc + _dot(vt, p.astype(BF16))
                out.append((m_new, l, acc))
            return tuple(out)

        init = tuple((jnp.full((1, tq), NEG_INF, F32), jnp.zeros((1, tq), F32), jnp.zeros((HEAD_DIM, tq), F32))
                     for _ in heads)
        carry = lax.fori_loop(0, qi, lambda kb, c: step(kb, c, False), init)
        carry = step(qi, carry, True)
        for i, h in enumerate(heads):
            _, l, acc = carry[i]
            o_ref[0, h * HEAD_DIM:(h + 1) * HEAD_DIM, :] = (acc / l).astype(BF16)


def _top16(v):
    bits = lax.bitcast_convert_type(v, jnp.uint32) & jnp.uint32(0xFFFF0000)
    return lax.bitcast_convert_type(bits, F32)


def _split3_bf16(c):
    hi = _top16(c)
    r1 = c - hi
    mid = _top16(r1)
    lo = r1 - mid
    return hi.astype(BF16), mid.astype(BF16), lo.astype(BF16)


def _forgetting_attention(qkvb, c_rows, batch, seq):
    qkv = qkvb.reshape(batch, seq, 3, FOX_HEADS, HEAD_DIM)
    hi, mid, lo = _split3_bf16(c_rows * LOG2E)
    one = jnp.ones_like(hi)
    pad = jnp.zeros((batch, seq, FOX_HEADS, LANES - HEAD_DIM - 6), BF16)
    to_cols = lambda parts: jnp.transpose(jnp.stack(parts, axis=-1), (0, 2, 1, 3))
    q_aug = jnp.concatenate([qkv[:, :, 0], to_cols([one, one, one, hi, mid, lo]), pad], axis=-1)
    k_aug = jnp.concatenate([qkv[:, :, 1], to_cols([-hi, -mid, -lo, one, one, one]), pad], axis=-1)
    q_aug = q_aug.reshape(batch, seq, FOX_HEADS * LANES)
    k_aug = k_aug.reshape(batch, seq, FOX_HEADS * LANES)
    v_t = jnp.transpose(qkv[:, :, 2].reshape(batch, seq, FOX_WIDTH), (0, 2, 1))
    nq = seq // TQ_FOX
    return pl.pallas_call(
        _fox_kernel,
        grid=(batch, nq),
        in_specs=[pl.BlockSpec((1, TQ_FOX, FOX_HEADS * LANES), lambda b, i: (b, i, 0)),
                  pl.BlockSpec((1, seq, FOX_HEADS * LANES), lambda b, i: (b, 0, 0)),
                  pl.BlockSpec((1, FOX_WIDTH, seq), lambda b, i: (b, 0, 0))],
        out_specs=pl.BlockSpec((1, FOX_WIDTH, TQ_FOX), lambda b, i: (b, 0, i)),
        out_shape=jax.ShapeDtypeStruct((batch, FOX_WIDTH, seq), BF16),
        compiler_params=pltpu.CompilerParams(dimension_semantics=("arbitrary", "arbitrary"),
                                             vmem_limit_bytes=VMEM_LIMIT),
        name="forgetting_attention",
    )(q_aug, k_aug, v_t)


def _merge_kernel(x_ref, oa0_ref, oa1_ref, oa2_ref, ls0_ref, ls1_ref, ls2_ref, obt_ref, gate_ref,
                  wa_ref, wb_ref, wo_ref, g2_ref, wrh_ref, wrl_ref, br_ref,
                  x1_ref, h2_ref, ri_ref, rg_ref, cnt_ref, run_s, stage_s):
    tm = TM_MERGE

    @pl.when(pl.program_id(0) == 0)
    def _():
        run_s[...] = jnp.zeros_like(run_s)

    def token_order(ref, g, stage):
        dil = DSWA_PATTERNS[g][1]
        if dil == 1:
            return ref[0]
        for r in range(dil):
            for half in range(2):
                col = r * DSWA_OUT + half * LANES
                stage[half, pl.ds(r, tm // dil, stride=dil), :] = ref[0, :, col:col + LANES]
        return jnp.concatenate([stage[0], stage[1]], axis=1)

    l0, l1, l2 = (token_order(ref, g, stage_s.at[g]) for g, ref in enumerate((ls0_ref, ls1_ref, ls2_ref)))
    lm = jnp.maximum(jnp.maximum(l0, l1), l2)
    e0, e1, e2 = jnp.exp(l0 - lm), jnp.exp(l1 - lm), jnp.exp(l2 - lm)
    o0, o1, o2 = (token_order(ref, g, stage_s.at[DSWA_GROUPS + g])
                  for g, ref in enumerate((oa0_ref, oa1_ref, oa2_ref)))
    o_a = (e0 * o0 + e1 * o1 + e2 * o2) / (e0 + e1 + e2)
    ya = _dot(o_a.astype(BF16), wa_ref[...])
    yb = lax.dot_general(obt_ref[0], wb_ref[...], (((0,), (0,)), ((), ())), preferred_element_type=F32)
    merged = gate_ref[:, :D_MODEL] * ya + gate_ref[:, D_MODEL:] * yb
    x1 = x_ref[...] + _dot(merged.astype(BF16), wo_ref[...])
    x1_ref[...] = x1

    ms = jnp.mean(x1 * x1, axis=-1, keepdims=True)
    h2 = x1 * lax.rsqrt(ms + RMS_EPS) * g2_ref[...]
    for c in range(ROW_CHUNKS):
        h2_ref[pl.ds(c, tm, stride=ROW_CHUNKS), :] = h2[:, c * LANES:(c + 1) * LANES]

    hi = h2.astype(BF16)
    lo = (h2 - hi.astype(F32)).astype(BF16)
    logits = _dot(hi, wrh_ref[...]) + _dot(lo, wrh_ref[...]) + _dot(hi, wrl_ref[...]) + br_ref[...]

    lane = lax.broadcasted_iota(jnp.int32, (tm, LANES), 1).astype(F32)
    work = logits
    top_v, top_i = [], []
    for _ in range(TOP_K):
        mk = jnp.max(work, axis=-1, keepdims=True)
        ik = jnp.min(jnp.where(work == mk, lane, float(LANES)), axis=-1, keepdims=True)
        top_v.append(mk)
        top_i.append(ik)
        work = jnp.where(lane == ik, -jnp.inf, work)
    ex = [jnp.exp(v - top_v[0]) for v in top_v]
    den = ex[0] + ex[1] + ex[2] + ex[3]

    onehot = jnp.zeros((tm, LANES), F32)
    for ik in top_i:
        onehot = onehot + jnp.where(lane == ik, 1.0, 0.0)
    r_i = lax.broadcasted_iota(jnp.int32, (tm, tm), 0)
    c_i = lax.broadcasted_iota(jnp.int32, (tm, tm), 1)
    tri = jnp.where(c_i < r_i, 1.0, 0.0).astype(BF16)
    before = _dot(tri, onehot.astype(BF16)) + run_s[...]
    ri = jnp.zeros((tm, LANES), jnp.int32)
    rg = jnp.zeros((tm, LANES), F32)
    for k in range(TOP_K):
        rank = jnp.sum(jnp.where(lane == top_i[k], before, 0.0), axis=-1, keepdims=True)
        ri = jnp.where(lane == k, top_i[k].astype(jnp.int32), ri)
        ri = jnp.where(lane == TOP_K + k, rank.astype(jnp.int32), ri)
        rg = jnp.where(lane == k, ex[k] / den, rg)
    ri_ref[...] = ri
    rg_ref[...] = rg
    run = run_s[...] + jnp.sum(onehot, axis=0, keepdims=True)
    run_s[...] = run
    cnt_ref[...] = run


def _merge_and_route(x2, batch, seq, oas, lss, obt, gate, w_branch_a, w_branch_b, w_out, ln2_g, w_router,
                     b_router):
    n = x2.shape[0]
    assert seq % TM_MERGE == 0 and all(TM_MERGE % (SUBLANES * d) == 0 for _, d in DSWA_PATTERNS)
    tiles_per_seq = seq // TM_MERGE
    seq_tile = lambda i: (i // tiles_per_seq, i % tiles_per_seq)
    group_specs = [pl.BlockSpec((1, TM_MERGE // d, d * DSWA_OUT), lambda i: (*seq_tile(i), 0))
                   for _, d in DSWA_PATTERNS]
    obt_spec = pl.BlockSpec((1, FOX_WIDTH, TM_MERGE), lambda i: (seq_tile(i)[0], 0, seq_tile(i)[1]))
    wr = jnp.pad(w_router, ((0, 0), (0, LANES - N_EXPERTS)))
    wr_top = _top16(wr)
    wr_hi = wr_top.astype(BF16)
    wr_lo = (wr - wr_top).astype(BF16)
    br = jnp.concatenate([b_router, jnp.full((LANES - N_EXPERTS,), NEG_INF, F32)]).reshape(1, LANES)
    const = lambda shape: pl.BlockSpec(shape, lambda i: (0,) * len(shape))
    row = lambda w: pl.BlockSpec((TM_MERGE, w), lambda i: (i, 0))
    return pl.pallas_call(
        _merge_kernel,
        grid=(n // TM_MERGE,),
        in_specs=[row(D_MODEL)] + group_specs + group_specs + [obt_spec, row(2 * D_MODEL),
                  const((DSWA_OUT, D_MODEL)), const((FOX_WIDTH, D_MODEL)), const((D_MODEL, D_MODEL)),
                  const((1, D_MODEL)), const((D_MODEL, LANES)), const((D_MODEL, LANES)), const((1, LANES))],
        out_specs=[row(D_MODEL), pl.BlockSpec((TM_MERGE * ROW_CHUNKS, LANES), lambda i: (i, 0)),
                   row(LANES), row(LANES), const((1, LANES))],
        out_shape=[jax.ShapeDtypeStruct((n, D_MODEL), F32),
                   jax.ShapeDtypeStruct((n * ROW_CHUNKS, LANES), F32),
                   jax.ShapeDtypeStruct((n, LANES), jnp.int32),
                   jax.ShapeDtypeStruct((n, LANES), F32),
                   jax.ShapeDtypeStruct((1, LANES), F32)],
        scratch_shapes=[pltpu.VMEM((1, LANES), F32), pltpu.VMEM((2 * DSWA_GROUPS, 2, TM_MERGE, LANES), F32)],
        compiler_params=pltpu.CompilerParams(dimension_semantics=("arbitrary",), vmem_limit_bytes=VMEM_LIMIT),
        name="merge_and_route",
    )(x2, *oas, *lss, obt, gate, w_branch_a.astype(BF16), w_branch_b.astype(BF16), w_out.astype(BF16),
      ln2_g.reshape(1, D_MODEL), wr_hi, wr_lo, br)


def _row_copy(src, src_row, dst, dst_row, sem):
    return pltpu.make_async_copy(src.at[pl.ds(pl.multiple_of(src_row * ROW_CHUNKS, ROW_CHUNKS), ROW_CHUNKS)],
                                 dst.at[pl.ds(pl.multiple_of(dst_row * ROW_CHUNKS, ROW_CHUNKS), ROW_CHUNKS)], sem)


def _dispatch_kernel(zb_ref, zf_ref, dest_hbm, h2_ref, xs_hbm, idx_s, zeros_v, sem_i, sem_z, sem_r):
    i = pl.program_id(0)
    n_move = TOP_K * TC_MOVE
    idx_copy = pltpu.make_async_copy(dest_hbm.at[i], idx_s, sem_i)
    idx_copy.start()

    def zero_copy(e):
        blk = pl.multiple_of(zb_ref[e] * (T_EXPERT * ROW_CHUNKS), T_EXPERT * ROW_CHUNKS)
        return pltpu.make_async_copy(zeros_v, xs_hbm.at[pl.ds(blk, T_EXPERT * ROW_CHUNKS)], sem_z)

    @pl.when(i == 0)
    def _():
        zeros_v[...] = jnp.zeros_like(zeros_v)
        for e in range(N_EXPERTS):
            @pl.when(zf_ref[e] > 0)
            def _():
                zero_copy(e).start()
        for e in range(N_EXPERTS):
            @pl.when(zf_ref[e] > 0)
            def _():
                zero_copy(e).wait()

    idx_copy.wait()

    def issue(jo, carry):
        for u in range(DMA_UNROLL):
            j = jo * DMA_UNROLL + u
            _row_copy(h2_ref, j & (TC_MOVE - 1), xs_hbm, idx_s[j], sem_r).start()
        return carry

    lax.fori_loop(0, n_move // DMA_UNROLL, issue, 0)

    def drain(jo, carry):
        for u in range(DMA_UNROLL):
            _row_copy(h2_ref, 0, xs_hbm, 0, sem_r).wait()
        return carry

    lax.fori_loop(0, n_move // DMA_UNROLL, drain, 0)


def _dispatch(dest_tiles, zero_blk, zero_flag, h2_rows, n_slots):
    nt = dest_tiles.shape[0]
    return pl.pallas_call(
        _dispatch_kernel,
        grid_spec=pltpu.PrefetchScalarGridSpec(
            num_scalar_prefetch=2, grid=(nt,),
            in_specs=[pl.BlockSpec(memory_space=pl.ANY),
                      pl.BlockSpec((TC_MOVE * ROW_CHUNKS, LANES), lambda i, zb, zf: (i, 0))],
            out_specs=pl.BlockSpec(memory_space=pl.ANY),
            scratch_shapes=[pltpu.SMEM((TOP_K * TC_MOVE,), jnp.int32),
                            pltpu.VMEM((T_EXPERT * ROW_CHUNKS, LANES), F32),
                            pltpu.SemaphoreType.DMA, pltpu.SemaphoreType.DMA, pltpu.SemaphoreType.DMA]),
        out_shape=jax.ShapeDtypeStruct((n_slots * ROW_CHUNKS, LANES), F32),
        compiler_params=pltpu.CompilerParams(dimension_semantics=("arbitrary",)),
        name="dispatch_rows",
    )(zero_blk, zero_flag, dest_tiles, h2_rows)


def _expert_kernel(be_ref, nu_ref, xs_ref, wgu_ref, bgu_ref, wd_ref, bd_ref, ys_ref, wgu_s, wd_s):
    j = pl.program_id(0)
    t = T_EXPERT

    @pl.when(j < nu_ref[0])
    def _():
        prev = be_ref[jnp.maximum(j - 1, 0)]

        @pl.when((j == 0) | (be_ref[j] != prev))
        def _():
            wgu_s[...] = wgu_ref[0].astype(BF16)
            wd_s[...] = wd_ref[0].astype(BF16)

        x = jnp.concatenate([xs_ref[pl.ds(c, t, stride=ROW_CHUNKS), :] for c in range(ROW_CHUNKS)], axis=1)
        gu = _dot(x.astype(BF16), wgu_s[...]) + bgu_ref[0]
        g = jnp.minimum(gu[:, :D_MODEL], SWIGLU_LIMIT)
        u = jnp.clip(gu[:, D_MODEL:], -SWIGLU_LIMIT, SWIGLU_LIMIT)
        mid = (u + 1.0) * (g * jax.nn.sigmoid(SWIGLU_ALPHA * g))
        y = _dot(mid.astype(BF16), wd_s[...]) + bd_ref[0]
        for c in range(ROW_CHUNKS):
            ys_ref[pl.ds(c, t, stride=ROW_CHUNKS), :] = y[:, c * LANES:(c + 1) * LANES]


def _expert_ffn(blk_e, n_used, xs_rows, w_gate_up, b_gate_up, w_down, b_down):
    n_blk = blk_e.shape[0]
    rows = T_EXPERT * ROW_CHUNKS
    live = lambda j, be, nu: jnp.minimum(j, nu[0] - 1)
    sink = lambda j, be, nu: jnp.where(j < nu[0], j, n_blk - 1)
    return pl.pallas_call(
        _expert_kernel,
        grid_spec=pltpu.PrefetchScalarGridSpec(
            num_scalar_prefetch=2, grid=(n_blk,),
            in_specs=[pl.BlockSpec((rows, LANES), lambda j, be, nu: (live(j, be, nu), 0)),
                      pl.BlockSpec((1, D_MODEL, 2 * D_MODEL), lambda j, be, nu: (be[live(j, be, nu)], 0, 0)),
                      pl.BlockSpec((1, 1, 2 * D_MODEL), lambda j, be, nu: (be[live(j, be, nu)], 0, 0)),
                      pl.BlockSpec((1, D_MODEL, D_MODEL), lambda j, be, nu: (be[live(j, be, nu)], 0, 0)),
                      pl.BlockSpec((1, 1, D_MODEL), lambda j, be, nu: (be[live(j, be, nu)], 0, 0))],
            out_specs=pl.BlockSpec((rows, LANES), lambda j, be, nu: (sink(j, be, nu), 0)),
            scratch_shapes=[pltpu.VMEM((D_MODEL, 2 * D_MODEL), BF16), pltpu.VMEM((D_MODEL, D_MODEL), BF16)]),
        out_shape=jax.ShapeDtypeStruct(xs_rows.shape, F32),
        compiler_params=pltpu.CompilerParams(dimension_semantics=("arbitrary",), vmem_limit_bytes=VMEM_LIMIT),
        name="expert_ffn",
    )(blk_e, n_used, xs_rows, w_gate_up, b_gate_up.reshape(N_EXPERTS, 1, 2 * D_MODEL),
      w_down, b_down.reshape(N_EXPERTS, 1, D_MODEL))


def _combine_kernel(dest_hbm, ys_hbm, x1_ref, rg_ref, o_ref, idx_s, buf, sem_i, sem_r):
    i = pl.program_id(0)
    n_move = TOP_K * TC_MOVE
    idx_copy = pltpu.make_async_copy(dest_hbm.at[i], idx_s, sem_i)
    idx_copy.start()
    idx_copy.wait()

    def issue(jo, carry):
        for u in range(DMA_UNROLL):
            j = jo * DMA_UNROLL + u
            _row_copy(ys_hbm, idx_s[j], buf, j, sem_r).start()
        return carry

    lax.fori_loop(0, n_move // DMA_UNROLL, issue, 0)

    def drain(jo, carry):
        for u in range(DMA_UNROLL):
            _row_copy(ys_hbm, 0, buf, 0, sem_r).wait()
        return carry

    lax.fori_loop(0, n_move // DMA_UNROLL, drain, 0)

    for c in range(ROW_CHUNKS):
        acc = x1_ref[:, c * LANES:(c + 1) * LANES]
        for k in range(TOP_K):
            rows = buf[pl.ds(k * TC_MOVE * ROW_CHUNKS + c, TC_MOVE, stride=ROW_CHUNKS), :]
            acc = acc + rg_ref[:, k:k + 1] * rows
        o_ref[:, c * LANES:(c + 1) * LANES] = acc


def _combine(dest_tiles, ys_rows, x1, rg):
    n = x1.shape[0]
    nt = dest_tiles.shape[0]
    return pl.pallas_call(
        _combine_kernel,
        grid=(nt,),
        in_specs=[pl.BlockSpec(memory_space=pl.ANY), pl.BlockSpec(memory_space=pl.ANY),
                  pl.BlockSpec((TC_MOVE, D_MODEL), lambda i: (i, 0)),
                  pl.BlockSpec((TC_MOVE, LANES), lambda i: (i, 0))],
        out_specs=pl.BlockSpec((TC_MOVE, D_MODEL), lambda i: (i, 0)),
        out_shape=jax.ShapeDtypeStruct((n, D_MODEL), F32),
        scratch_shapes=[pltpu.SMEM((TOP_K * TC_MOVE,), jnp.int32),
                        pltpu.VMEM((TOP_K * TC_MOVE * ROW_CHUNKS, LANES), F32),
                        pltpu.SemaphoreType.DMA, pltpu.SemaphoreType.DMA],
        compiler_params=pltpu.CompilerParams(dimension_semantics=("arbitrary",), vmem_limit_bytes=VMEM_LIMIT),
        name="combine_rows",
    )(dest_tiles, ys_rows, x1, rg)


def _layer(x, ln1_g, w_in, b_forget, b_gate, q_norm_a, k_norm_a, q_norm_b, k_norm_b, rel_bias,
           w_branch_a, w_branch_b, w_out, ln2_g, w_router, b_router, w_gate_up, b_gate_up, w_down, b_down):
    batch, seq, _ = x.shape
    n = batch * seq
    x2 = x.reshape(n, D_MODEL)
    *qkv_groups, qkvb, lf, gate = _in_projection(x2, batch, seq, ln1_g, w_in, b_forget, b_gate,
                                                 q_norm_a, k_norm_a, q_norm_b, k_norm_b)

    oas, lss = [], []
    for g in range(DSWA_GROUPS):
        o, lse = _dilated_attention(qkv_groups[g], rel_bias, g, batch, seq)
        oas.append(o)
        lss.append(lse)

    lf_rows = jnp.transpose(lf[:, :FOX_HEADS].reshape(batch, seq, FOX_HEADS), (0, 2, 1))
    c_rows = _forget_cumsum(lf_rows)
    obt = _forgetting_attention(qkvb, c_rows, batch, seq)

    x1, h2_rows, ri, rg, cnt = _merge_and_route(x2, batch, seq, oas, lss, obt, gate, w_branch_a, w_branch_b,
                                                w_out, ln2_g, w_router, b_router)

    counts = cnt[0, :N_EXPERTS].astype(jnp.int32)
    pcounts = ((counts + T_EXPERT - 1) // T_EXPERT) * T_EXPERT
    pends = jnp.cumsum(pcounts)
    pstarts = pends - pcounts
    idx, rank = ri[:, :TOP_K], ri[:, TOP_K:2 * TOP_K]
    start_of = jnp.sum(jnp.where(idx[..., None] == jnp.arange(N_EXPERTS), pstarts, 0), axis=-1)
    dest = start_of + rank
    nt = n // TC_MOVE
    dest_tiles = jnp.transpose(dest.reshape(nt, TC_MOVE, TOP_K), (0, 2, 1)).reshape(nt, TOP_K * TC_MOVE)
    n_slots = n * TOP_K + N_EXPERTS * T_EXPERT
    n_blk = n_slots // T_EXPERT
    blk_first = jnp.arange(n_blk, dtype=jnp.int32) * T_EXPERT
    blk_e = jnp.minimum(jnp.sum((pends[None, :] <= blk_first[:, None]).astype(jnp.int32), axis=1), N_EXPERTS - 1)
    n_used = (pends[-1:] // T_EXPERT).astype(jnp.int32)
    zero_blk = jnp.maximum(pends // T_EXPERT - 1, 0).astype(jnp.int32)
    zero_flag = (pcounts > 0).astype(jnp.int32)

    xs_rows = _dispatch(dest_tiles, zero_blk, zero_flag, h2_rows, n_slots)
    ys_rows = _expert_ffn(blk_e, n_used, xs_rows, w_gate_up, b_gate_up, w_down, b_down)
    out = _combine(dest_tiles, ys_rows, x1, rg)
    return out.reshape(batch, seq, D_MODEL)


def kernel(x, ln1_g, w_in, b_forget, b_gate, q_norm_a, k_norm_a, q_norm_b, k_norm_b, rel_bias, w_branch_a, w_branch_b, w_out, ln2_g, w_router, b_router, w_gate_up, b_gate_up, w_down, b_down):
    for layer in range(ln1_g.shape[0]):
        x = _layer(x, ln1_g[layer], w_in[layer], b_forget[layer], b_gate[layer], q_norm_a[layer],
                   k_norm_a[layer], q_norm_b[layer], k_norm_b[layer], rel_bias, w_branch_a[layer],
                   w_branch_b[layer], w_out[layer], ln2_g[layer], w_router[layer], b_router[layer],
                   w_gate_up[layer], b_gate_up[layer], w_down[layer], b_down[layer])
    return x
```

```python
import functools
import math

import numpy as np
import jax
import jax.numpy as jnp
from jax import lax
from jax.experimental import pallas as pl
from jax.experimental.pallas import tpu as pltpu

F32 = jnp.float32
BF16 = jnp.bfloat16

D_MODEL = 1024
HEAD_DIM = 64
DSWA_PATTERNS = ((128, 1), (512, 4), (2048, 16))
DSWA_HG = 4
DSWA_GROUPS = len(DSWA_PATTERNS)
DSWA_WIDTH = DSWA_GROUPS * DSWA_HG * HEAD_DIM
DSWA_OUT = DSWA_HG * HEAD_DIM
FOX_HEADS = 8
FOX_WIDTH = FOX_HEADS * HEAD_DIM
ATTN_BLOCK = 128
NUM_BUCKETS = 32
MAX_DISTANCE = 2048
N_EXPERTS = 32
TOP_K = 4
SWIGLU_LIMIT = 7.0
SWIGLU_ALPHA = 1.702
RMS_EPS = 1e-6
NEG_INF = -1e30
LOG2E = 1.4426950408889634

LANES = 128
SUBLANES = 8
ROW_CHUNKS = D_MODEL // LANES
QKV_WIDTH = 3 * DSWA_WIDTH + 3 * FOX_WIDTH
VMEM_LIMIT = 56 * 1024 * 1024

TM_PROJ = 256
TQ_FOX = 512
FOX_HEADS_PER_LOOP = 2
DSWA_BLOCKS_PER_STEP = 4
TM_MERGE = 256
T_EXPERT = 512
TC_MOVE = 512
DMA_UNROLL = 16


def _dot(a, b):
    return jnp.dot(a, b, preferred_element_type=F32)


def _dot_nt(a, b):
    return lax.dot_general(a, b, (((1,), (1,)), ((), ())), preferred_element_type=F32)


def _inproj_kernel(x_ref, g1_ref, wq_ref, wf_ref, wg_ref, bf_ref, bg_ref, nrm_ref, e_ref,
                   qkv0_ref, qkv1_ref, qkv2_ref, qkvb_ref, lf_ref, gate_ref, stage_ref):
    group_refs = (qkv0_ref, qkv1_ref, qkv2_ref)
    x = x_ref[...]
    ms = jnp.mean(x * x, axis=-1, keepdims=True)
    h = (x * lax.rsqrt(ms + RMS_EPS) * g1_ref[...]).astype(BF16)
    ones_bd = e_ref[...]

    def head_norm(y, gn):
        sq = y * y
        hi = sq.astype(BF16)
        lo = (sq - hi.astype(F32)).astype(BF16)
        ss = _dot(hi, ones_bd) + _dot(lo, ones_bd)
        return y * lax.rsqrt(ss * (1.0 / HEAD_DIM) + RMS_EPS) * gn

    n_a = 3 * DSWA_WIDTH // 256
    for c in range(QKV_WIDTH // 256):
        y = _dot(h, wq_ref[:, c * 256:(c + 1) * 256])
        if c < 3:
            y = head_norm(y, nrm_ref[0:1, :])
        elif c < 6:
            y = head_norm(y, nrm_ref[1:2, :])
        elif 9 <= c < 11:
            y = head_norm(y, nrm_ref[2:3, :])
        elif 11 <= c < 13:
            y = head_norm(y, nrm_ref[3:4, :])
        if c < n_a:
            g, part = c % DSWA_GROUPS, c // DSWA_GROUPS
            dil = DSWA_PATTERNS[g][1]
            if dil == 1:
                group_refs[g][0, :, part * 256:(part + 1) * 256] = y.astype(BF16)
            else:
                for half in range(2):
                    stage_ref[half] = y[:, half * LANES:(half + 1) * LANES]
                for r in range(dil):
                    for half in range(2):
                        col = r * DSWA_WIDTH + part * 256 + half * LANES
                        rows = stage_ref[half, pl.ds(r, TM_PROJ // dil, stride=dil), :]
                        group_refs[g][0, :, col:col + LANES] = rows.astype(BF16)
        else:
            qkvb_ref[:, (c - n_a) * 256:(c - n_a + 1) * 256] = y.astype(BF16)

    z = _dot(h, wf_ref[...]) + bf_ref[...]
    lf_ref[...] = jnp.minimum(z, 0.0) - jnp.log1p(jnp.exp(-jnp.abs(z)))
    for c in range(2 * D_MODEL // 256):
        gl = _dot(h, wg_ref[:, c * 256:(c + 1) * 256]) + bg_ref[:, c * 256:(c + 1) * 256]
        gate_ref[:, c * 256:(c + 1) * 256] = jax.nn.sigmoid(gl)


def _in_projection(x2, batch, seq, ln1_g, w_in, b_forget, b_gate, q_norm_a, k_norm_a, q_norm_b, k_norm_b):
    n = x2.shape[0]
    assert seq % TM_PROJ == 0 and all(TM_PROJ % (16 * d) == 0 for _, d in DSWA_PATTERNS)
    tiles_per_seq = seq // TM_PROJ
    wq = w_in[:, :QKV_WIDTH].astype(BF16)
    wf = jnp.pad(w_in[:, QKV_WIDTH:QKV_WIDTH + FOX_HEADS], ((0, 0), (0, LANES - FOX_HEADS))).astype(BF16)
    wg = w_in[:, QKV_WIDTH + FOX_HEADS:].astype(BF16)
    bf = jnp.pad(b_forget, (0, LANES - FOX_HEADS)).reshape(1, LANES)
    bg = b_gate.reshape(1, 2 * D_MODEL)
    scale = HEAD_DIM ** -0.5
    nrm = jnp.stack([jnp.tile(q_norm_a, 4) * scale, jnp.tile(k_norm_a, 4),
                     jnp.tile(q_norm_b, 4) * (scale * LOG2E), jnp.tile(k_norm_b, 4)])
    head_id = np.arange(256) // HEAD_DIM
    ones_bd = jnp.asarray(head_id[:, None] == head_id[None, :], BF16)
    const = lambda shape: pl.BlockSpec(shape, lambda i: (0,) * len(shape))
    row = lambda w: pl.BlockSpec((TM_PROJ, w), lambda i: (i, 0))
    group_spec = lambda d: pl.BlockSpec((1, TM_PROJ // d, d * DSWA_WIDTH),
                                        lambda i: (i // tiles_per_seq, i % tiles_per_seq, 0))
    group_shape = lambda d: jax.ShapeDtypeStruct((batch, seq // d, d * DSWA_WIDTH), BF16)
    dils = [d for _, d in DSWA_PATTERNS]
    return pl.pallas_call(
        _inproj_kernel,
        grid=(n // TM_PROJ,),
        in_specs=[row(D_MODEL), const((1, D_MODEL)), const((D_MODEL, QKV_WIDTH)), const((D_MODEL, LANES)),
                  const((D_MODEL, 2 * D_MODEL)), const((1, LANES)), const((1, 2 * D_MODEL)),
                  const((4, 256)), const((256, 256))],
        out_specs=[group_spec(d) for d in dils] + [row(3 * FOX_WIDTH), row(LANES), row(2 * D_MODEL)],
        out_shape=[group_shape(d) for d in dils] + [jax.ShapeDtypeStruct((n, 3 * FOX_WIDTH), BF16),
                                                    jax.ShapeDtypeStruct((n, LANES), F32),
                                                    jax.ShapeDtypeStruct((n, 2 * D_MODEL), F32)],
        scratch_shapes=[pltpu.VMEM((2, TM_PROJ, LANES), F32)],
        compiler_params=pltpu.CompilerParams(dimension_semantics=("arbitrary",), vmem_limit_bytes=VMEM_LIMIT),
        name="in_projection",
    )(x2, ln1_g.reshape(1, D_MODEL), wq, wf, wg, bf, bg, nrm, ones_bd)


def _cumsum_kernel(lf_ref, c_ref):
    v = lf_ref[0]
    seq = v.shape[1]
    lane = lax.broadcasted_iota(jnp.int32, v.shape, 1)
    shift = 1
    while shift < seq:
        v = v + jnp.where(lane >= shift, pltpu.roll(v, shift, axis=1), 0.0)
        shift *= 2
    c_ref[0] = v


def _forget_cumsum(lf_rows):
    b, hds, seq = lf_rows.shape
    spec = pl.BlockSpec((1, hds, seq), lambda i: (i, 0, 0))
    return pl.pallas_call(
        _cumsum_kernel, grid=(b,), in_specs=[spec], out_specs=spec,
        out_shape=jax.ShapeDtypeStruct((b, hds, seq), F32), name="forget_cumsum",
    )(lf_rows)


def _t5_bucket(n):
    max_exact = NUM_BUCKETS // 2
    n_safe = np.maximum(n, 1).astype(np.float64)
    large = max_exact + (np.log(n_safe / max_exact) / np.log(MAX_DISTANCE / max_exact)
                         * (NUM_BUCKETS - max_exact)).astype(np.int64)
    large = np.minimum(large, NUM_BUCKETS - 1)
    return np.where(n < max_exact, n, large).astype(np.int32)


def _dswa_kernel(q_ref, kp_ref, kc_ref, vp_ref, vc_ref, bias_ref, o_ref, l_ref, *, blocks):
    first = pl.program_id(2) == 0
    for j in range(blocks):
        rows = slice(j * ATTN_BLOCK, (j + 1) * ATTN_BLOCK)
        prev_rows = slice((j - 1) * ATTN_BLOCK, j * ATTN_BLOCK)
        for h in range(DSWA_HG):
            hs = slice(h * HEAD_DIM, (h + 1) * HEAD_DIM)
            q = q_ref[0, rows, hs]
            k_prev = kp_ref[0, :, hs] if j == 0 else kc_ref[0, prev_rows, hs]
            v_prev = vp_ref[0, :, hs] if j == 0 else vc_ref[0, prev_rows, hs]
            sp = _dot_nt(q, k_prev) + bias_ref[h, :, :ATTN_BLOCK]
            sc = _dot_nt(q, kc_ref[0, rows, hs]) + bias_ref[h, :, ATTN_BLOCK:]
            if j == 0:
                sp = jnp.where(first, NEG_INF, sp)
            m = jnp.maximum(jnp.max(sp, axis=-1, keepdims=True), jnp.max(sc, axis=-1, keepdims=True))
            pp = jnp.exp(sp - m)
            pc = jnp.exp(sc - m)
            den = jnp.sum(pp, axis=-1, keepdims=True) + jnp.sum(pc, axis=-1, keepdims=True)
            o = _dot(pp.astype(BF16), v_prev) + _dot(pc.astype(BF16), vc_ref[0, rows, hs])
            o_ref[0, rows, hs] = o / den
            l_ref[0, rows, hs] = jnp.broadcast_to(m + jnp.log(den), (ATTN_BLOCK, HEAD_DIM))


def _dilated_attention(qkv3, rel_bias, g, batch, seq):
    window, dil = DSWA_PATTERNS[g]
    steps = window // dil
    assert steps == ATTN_BLOCK
    length = seq // dil
    assert length % ATTN_BLOCK == 0
    nblk = length // ATTN_BLOCK
    span = 2 * ATTN_BLOCK
    back = np.arange(ATTN_BLOCK)[:, None] + steps - np.arange(span)[None, :]
    valid = (back >= 0) & (back <= steps)
    bucket = _t5_bucket(np.clip(back, 0, steps) * dil)
    cols = rel_bias[:, g * DSWA_HG:(g + 1) * DSWA_HG]
    onehot = np.asarray(bucket.reshape(-1)[:, None] == np.arange(NUM_BUCKETS)[None, :], np.float32)
    bias = jnp.dot(jnp.asarray(onehot), cols.astype(F32), precision=lax.Precision.HIGHEST)
    bias = jnp.transpose(bias.reshape(ATTN_BLOCK, span, DSWA_HG), (2, 0, 1))
    bias = jnp.where(valid[None], bias, NEG_INF)

    per_pos = DSWA_WIDTH // 256
    nb = math.gcd(nblk, DSWA_BLOCKS_PER_STEP)
    blk = (1, nb * ATTN_BLOCK, 256)
    one = (1, ATTN_BLOCK, 256)
    before = lambda n: jnp.maximum(n * nb - 1, 0)
    q_spec = pl.BlockSpec(blk, lambda b, r, n: (b, n, r * per_pos))
    kc_spec = pl.BlockSpec(blk, lambda b, r, n: (b, n, r * per_pos + 1))
    kp_spec = pl.BlockSpec(one, lambda b, r, n: (b, before(n), r * per_pos + 1))
    vc_spec = pl.BlockSpec(blk, lambda b, r, n: (b, n, r * per_pos + 2))
    vp_spec = pl.BlockSpec(one, lambda b, r, n: (b, before(n), r * per_pos + 2))
    bias_spec = pl.BlockSpec((DSWA_HG, ATTN_BLOCK, span), lambda b, r, n: (0, 0, 0))
    o_spec = pl.BlockSpec(blk, lambda b, r, n: (b, n, r))
    return pl.pallas_call(
        functools.partial(_dswa_kernel, blocks=nb),
        grid=(batch, dil, nblk // nb),
        in_specs=[q_spec, kp_spec, kc_spec, vp_spec, vc_spec, bias_spec],
        out_specs=[o_spec, o_spec],
        out_shape=[jax.ShapeDtypeStruct((batch, length, dil * DSWA_OUT), F32)] * 2,
        compiler_params=pltpu.CompilerParams(dimension_semantics=("arbitrary",) * 3),
        name=f"dilated_attention_{g}",
    )(qkv3, qkv3, qkv3, qkv3, qkv3, bias)


def _fox_kernel(q_ref, k_ref, vt_ref, o_ref):
    qi = pl.program_id(1)
    tq = TQ_FOX
    krow = lax.broadcasted_iota(jnp.int32, (tq, tq), 0)
    qcol = lax.broadcasted_iota(jnp.int32, (tq, tq), 1)
    causal = krow <= qcol

    for hg in range(FOX_HEADS // FOX_HEADS_PER_LOOP):
        heads = [hg * FOX_HEADS_PER_LOOP + i for i in range(FOX_HEADS_PER_LOOP)]
        qs = [q_ref[0, :, h * LANES:(h + 1) * LANES] for h in heads]

        def step(kb, carry, masked):
            ks = pl.multiple_of(kb * tq, tq)
            out = []
            for i, h in enumerate(heads):
                m, l, acc = carry[i]
                k = k_ref[0, pl.ds(ks, tq), h * LANES:(h + 1) * LANES]
                vt = vt_ref[0, h * HEAD_DIM:(h + 1) * HEAD_DIM, pl.ds(ks, tq)]
                st = _dot_nt(k, qs[i])
                if masked:
                    st = jnp.where(causal, st, NEG_INF)
                m_new = jnp.maximum(m, jnp.max(st, axis=0, keepdims=True))
                alpha = jnp.exp2(m - m_new)
                p = jnp.exp2(st - m_new)
                l = alpha * l + jnp.sum(p, axis=0, keepdims=True)
                acc = alpha * acc + _dot(vt, p.astype(BF16))
                out.append((m_new, l, acc))
            return tuple(out)

        init = tuple((jnp.full((1, tq), NEG_INF, F32), jnp.zeros((1, tq), F32), jnp.zeros((HEAD_DIM, tq), F32))
                     for _ in heads)
        carry = lax.fori_loop(0, qi, lambda kb, c: step(kb, c, False), init)
        carry = step(qi, carry, True)
        for i, h in enumerate(heads):
            _, l, acc = carry[i]
            o_ref[0, h * HEAD_DIM:(h + 1) * HEAD_DIM, :] = (acc / l).astype(BF16)


def _top16(v):
    bits = lax.bitcast_convert_type(v, jnp.uint32) & jnp.uint32(0xFFFF0000)
    return lax.bitcast_convert_type(bits, F32)


def _split3_bf16(c):
    hi = _top16(c)
    r1 = c - hi
    mid = _top16(r1)
    lo = r1 - mid
    return hi.astype(BF16), mid.astype(BF16), lo.astype(BF16)


def _forgetting_attention(qkvb, c_rows, batch, seq):
    qkv = qkvb.reshape(batch, seq, 3, FOX_HEADS, HEAD_DIM)
    hi, mid, lo = _split3_bf16(c_rows * LOG2E)
    one = jnp.ones_like(hi)
    pad = jnp.zeros((batch, seq, FOX_HEADS, LANES - HEAD_DIM - 6), BF16)
    to_cols = lambda parts: jnp.transpose(jnp.stack(parts, axis=-1), (0, 2, 1, 3))
    q_aug = jnp.concatenate([qkv[:, :, 0], to_cols([one, one, one, hi, mid, lo]), pad], axis=-1)
    k_aug = jnp.concatenate([qkv[:, :, 1], to_cols([-hi, -mid, -lo, one, one, one]), pad], axis=-1)
    q_aug = q_aug.reshape(batch, seq, FOX_HEADS * LANES)
    k_aug = k_aug.reshape(batch, seq, FOX_HEADS * LANES)
    v_t = jnp.transpose(qkv[:, :, 2].reshape(batch, seq, FOX_WIDTH), (0, 2, 1))
    nq = seq // TQ_FOX
    return pl.pallas_call(
        _fox_kernel,
        grid=(batch, nq),
        in_specs=[pl.BlockSpec((1, TQ_FOX, FOX_HEADS * LANES), lambda b, i: (b, i, 0)),
                  pl.BlockSpec((1, seq, FOX_HEADS * LANES), lambda b, i: (b, 0, 0)),
                  pl.BlockSpec((1, FOX_WIDTH, seq), lambda b, i: (b, 0, 0))],
        out_specs=pl.BlockSpec((1, FOX_WIDTH, TQ_FOX), lambda b, i: (b, 0, i)),
        out_shape=jax.ShapeDtypeStruct((batch, FOX_WIDTH, seq), BF16),
        compiler_params=pltpu.CompilerParams(dimension_semantics=("arbitrary", "arbitrary"),
                                             vmem_limit_bytes=VMEM_LIMIT),
        name="forgetting_attention",
    )(q_aug, k_aug, v_t)


def _merge_kernel(x_ref, oa0_ref, oa1_ref, oa2_ref, ls0_ref, ls1_ref, ls2_ref, obt_ref, gate_ref,
                  wa_ref, wb_ref, wo_ref, g2_ref, wrh_ref, wrl_ref, br_ref,
                  x1_ref, h2_ref, ri_ref, rg_ref, cnt_ref, run_s, stage_s):
    tm = TM_MERGE

    @pl.when(pl.program_id(0) == 0)
    def _():
        run_s[...] = jnp.zeros_like(run_s)

    def token_order(ref, g, stage):
        dil = DSWA_PATTERNS[g][1]
        if dil == 1:
            return ref[0]
        for r in range(dil):
            for half in range(2):
                col = r * DSWA_OUT + half * LANES
                stage[half, pl.ds(r, tm // dil, stride=dil), :] = ref[0, :, col:col + LANES]
        return jnp.concatenate([stage[0], stage[1]], axis=1)

    l0, l1, l2 = (token_order(ref, g, stage_s.at[g]) for g, ref in enumerate((ls0_ref, ls1_ref, ls2_ref)))
    lm = jnp.maximum(jnp.maximum(l0, l1), l2)
    e0, e1, e2 = jnp.exp(l0 - lm), jnp.exp(l1 - lm), jnp.exp(l2 - lm)
    o0, o1, o2 = (token_order(ref, g, stage_s.at[DSWA_GROUPS + g])
                  for g, ref in enumerate((oa0_ref, oa1_ref, oa2_ref)))
    o_a = (e0 * o0 + e1 * o1 + e2 * o2) / (e0 + e1 + e2)
    ya = _dot(o_a.astype(BF16), wa_ref[...])
    yb = lax.dot_general(obt_ref[0], wb_ref[...], (((0,), (0,)), ((), ())), preferred_element_type=F32)
    merged = gate_ref[:, :D_MODEL] * ya + gate_ref[:, D_MODEL:] * yb
    x1 = x_ref[...] + _dot(merged.astype(BF16), wo_ref[...])
    x1_ref[...] = x1

    ms = jnp.mean(x1 * x1, axis=-1, keepdims=True)
    h2 = x1 * lax.rsqrt(ms + RMS_EPS) * g2_ref[...]
    for c in range(ROW_CHUNKS):
        h2_ref[pl.ds(c, tm, stride=ROW_CHUNKS), :] = h2[:, c * LANES:(c + 1) * LANES]

    hi = h2.astype(BF16)
    lo = (h2 - hi.astype(F32)).astype(BF16)
    logits = _dot(hi, wrh_ref[...]) + _dot(lo, wrh_ref[...]) + _dot(hi, wrl_ref[...]) + br_ref[...]

    lane = lax.broadcasted_iota(jnp.int32, (tm, LANES), 1).astype(F32)
    work = logits
    top_v, top_i = [], []
    for _ in range(TOP_K):
        mk = jnp.max(work, axis=-1, keepdims=True)
        ik = jnp.min(jnp.where(work == mk, lane, float(LANES)), axis=-1, keepdims=True)
        top_v.append(mk)
        top_i.append(ik)
        work = jnp.where(lane == ik, -jnp.inf, work)
    ex = [jnp.exp(v - top_v[0]) for v in top_v]
    den = ex[0] + ex[1] + ex[2] + ex[3]

    onehot = jnp.zeros((tm, LANES), F32)
    for ik in top_i:
        onehot = onehot + jnp.where(lane == ik, 1.0, 0.0)
    r_i = lax.broadcasted_iota(jnp.int32, (tm, tm), 0)
    c_i = lax.broadcasted_iota(jnp.int32, (tm, tm), 1)
    tri = jnp.where(c_i < r_i, 1.0, 0.0).astype(BF16)
    before = _dot(tri, onehot.astype(BF16)) + run_s[...]
    ri = jnp.zeros((tm, LANES), jnp.int32)
    rg = jnp.zeros((tm, LANES), F32)
    for k in range(TOP_K):
        rank = jnp.sum(jnp.where(lane == top_i[k], before, 0.0), axis=-1, keepdims=True)
        ri = jnp.where(lane == k, top_i[k].astype(jnp.int32), ri)
        ri = jnp.where(lane == TOP_K + k, rank.astype(jnp.int32), ri)
        rg = jnp.where(lane == k, ex[k] / den, rg)
    ri_ref[...] = ri
    rg_ref[...] = rg
    run = run_s[...] + jnp.sum(onehot, axis=0, keepdims=True)
    run_s[...] = run
    cnt_ref[...] = run


def _merge_and_route(x2, batch, seq, oas, lss, obt, gate, w_branch_a, w_branch_b, w_out, ln2_g, w_router,
                     b_router):
    n = x2.shape[0]
    assert seq % TM_MERGE == 0 and all(TM_MERGE % (SUBLANES * d) == 0 for _, d in DSWA_PATTERNS)
    tiles_per_seq = seq // TM_MERGE
    seq_tile = lambda i: (i // tiles_per_seq, i % tiles_per_seq)
    group_specs = [pl.BlockSpec((1, TM_MERGE // d, d * DSWA_OUT), lambda i: (*seq_tile(i), 0))
                   for _, d in DSWA_PATTERNS]
    obt_spec = pl.BlockSpec((1, FOX_WIDTH, TM_MERGE), lambda i: (seq_tile(i)[0], 0, seq_tile(i)[1]))
    wr = jnp.pad(w_router, ((0, 0), (0, LANES - N_EXPERTS)))
    wr_top = _top16(wr)
    wr_hi = wr_top.astype(BF16)
    wr_lo = (wr - wr_top).astype(BF16)
    br = jnp.concatenate([b_router, jnp.full((LANES - N_EXPERTS,), NEG_INF, F32)]).reshape(1, LANES)
    const = lambda shape: pl.BlockSpec(shape, lambda i: (0,) * len(shape))
    row = lambda w: pl.BlockSpec((TM_MERGE, w), lambda i: (i, 0))
    return pl.pallas_call(
        _merge_kernel,
        grid=(n // TM_MERGE,),
        in_specs=[row(D_MODEL)] + group_specs + group_specs + [obt_spec, row(2 * D_MODEL),
                  const((DSWA_OUT, D_MODEL)), const((FOX_WIDTH, D_MODEL)), const((D_MODEL, D_MODEL)),
                  const((1, D_MODEL)), const((D_MODEL, LANES)), const((D_MODEL, LANES)), const((1, LANES))],
        out_specs=[row(D_MODEL), pl.BlockSpec((TM_MERGE * ROW_CHUNKS, LANES), lambda i: (i, 0)),
                   row(LANES), row(LANES), const((1, LANES))],
        out_shape=[jax.ShapeDtypeStruct((n, D_MODEL), F32),
                   jax.ShapeDtypeStruct((n * ROW_CHUNKS, LANES), F32),
                   jax.ShapeDtypeStruct((n, LANES), jnp.int32),
                   jax.ShapeDtypeStruct((n, LANES), F32),
                   jax.ShapeDtypeStruct((1, LANES), F32)],
        scratch_shapes=[pltpu.VMEM((1, LANES), F32), pltpu.VMEM((2 * DSWA_GROUPS, 2, TM_MERGE, LANES), F32)],
        compiler_params=pltpu.CompilerParams(dimension_semantics=("arbitrary",), vmem_limit_bytes=VMEM_LIMIT),
        name="merge_and_route",
    )(x2, *oas, *lss, obt, gate, w_branch_a.astype(BF16), w_branch_b.astype(BF16), w_out.astype(BF16),
      ln2_g.reshape(1, D_MODEL), wr_hi, wr_lo, br)


def _row_copy(src, src_row, dst, dst_row, sem):
    return pltpu.make_async_copy(src.at[pl.ds(pl.multiple_of(src_row * ROW_CHUNKS, ROW_CHUNKS), ROW_CHUNKS)],
                                 dst.at[pl.ds(pl.multiple_of(dst_row * ROW_CHUNKS, ROW_CHUNKS), ROW_CHUNKS)], sem)


def _dispatch_kernel(zb_ref, zf_ref, dest_hbm, h2_ref, xs_hbm, idx_s, zeros_v, sem_i, sem_z, sem_r):
    i = pl.program_id(0)
    n_move = TOP_K * TC_MOVE
    idx_copy = pltpu.make_async_copy(dest_hbm.at[i], idx_s, sem_i)
    idx_copy.start()

    def zero_copy(e):
        blk = pl.multiple_of(zb_ref[e] * (T_EXPERT * ROW_CHUNKS), T_EXPERT * ROW_CHUNKS)
        return pltpu.make_async_copy(zeros_v, xs_hbm.at[pl.ds(blk, T_EXPERT * ROW_CHUNKS)], sem_z)

    @pl.when(i == 0)
    def _():
        zeros_v[...] = jnp.zeros_like(zeros_v)
        for e in range(N_EXPERTS):
            @pl.when(zf_ref[e] > 0)
            def _():
                zero_copy(e).start()
        for e in range(N_EXPERTS):
            @pl.when(zf_ref[e] > 0)
            def _():
                zero_copy(e).wait()

    idx_copy.wait()

    def issue(jo, carry):
        for u in range(DMA_UNROLL):
            j = jo * DMA_UNROLL + u
            _row_copy(h2_ref, j & (TC_MOVE - 1), xs_hbm, idx_s[j], sem_r).start(priority=u % 2)
        return carry

    lax.fori_loop(0, n_move // DMA_UNROLL, issue, 0)

    def drain(jo, carry):
        for u in range(DMA_UNROLL):
            _row_copy(h2_ref, 0, xs_hbm, 0, sem_r).wait()
        return carry

    lax.fori_loop(0, n_move // DMA_UNROLL, drain, 0)


def _dispatch(dest_tiles, zero_blk, zero_flag, h2_rows, n_slots):
    nt = dest_tiles.shape[0]
    return pl.pallas_call(
        _dispatch_kernel,
        grid_spec=pltpu.PrefetchScalarGridSpec(
            num_scalar_prefetch=2, grid=(nt,),
            in_specs=[pl.BlockSpec(memory_space=pl.ANY),
                      pl.BlockSpec((TC_MOVE * ROW_CHUNKS, LANES), lambda i, zb, zf: (i, 0))],
            out_specs=pl.BlockSpec(memory_space=pl.ANY),
            scratch_shapes=[pltpu.SMEM((TOP_K * TC_MOVE,), jnp.int32),
                            pltpu.VMEM((T_EXPERT * ROW_CHUNKS, LANES), F32),
                            pltpu.SemaphoreType.DMA, pltpu.SemaphoreType.DMA, pltpu.SemaphoreType.DMA]),
        out_shape=jax.ShapeDtypeStruct((n_slots * ROW_CHUNKS, LANES), F32),
        compiler_params=pltpu.CompilerParams(dimension_semantics=("arbitrary",)),
        name="dispatch_rows",
    )(zero_blk, zero_flag, dest_tiles, h2_rows)


def _expert_kernel(be_ref, nu_ref, xs_ref, wgu_ref, bgu_ref, wd_ref, bd_ref, ys_ref, wgu_s, wd_s):
    j = pl.program_id(0)
    t = T_EXPERT

    @pl.when(j < nu_ref[0])
    def _():
        prev = be_ref[jnp.maximum(j - 1, 0)]

        @pl.when((j == 0) | (be_ref[j] != prev))
        def _():
            wgu_s[...] = wgu_ref[0].astype(BF16)
            wd_s[...] = wd_ref[0].astype(BF16)

        x = jnp.concatenate([xs_ref[pl.ds(c, t, stride=ROW_CHUNKS), :] for c in range(ROW_CHUNKS)], axis=1)
        gu = _dot(x.astype(BF16), wgu_s[...]) + bgu_ref[0]
        g = jnp.minimum(gu[:, :D_MODEL], SWIGLU_LIMIT)
        u = jnp.clip(gu[:, D_MODEL:], -SWIGLU_LIMIT, SWIGLU_LIMIT)
        mid = (u + 1.0) * (g * jax.nn.sigmoid(SWIGLU_ALPHA * g))
        y = _dot(mid.astype(BF16), wd_s[...]) + bd_ref[0]
        for c in range(ROW_CHUNKS):
            ys_ref[pl.ds(c, t, stride=ROW_CHUNKS), :] = y[:, c * LANES:(c + 1) * LANES]


def _expert_ffn(blk_e, n_used, xs_rows, w_gate_up, b_gate_up, w_down, b_down):
    n_blk = blk_e.shape[0]
    rows = T_EXPERT * ROW_CHUNKS
    live = lambda j, be, nu: jnp.minimum(j, nu[0] - 1)
    sink = lambda j, be, nu: jnp.where(j < nu[0], j, n_blk - 1)
    return pl.pallas_call(
        _expert_kernel,
        grid_spec=pltpu.PrefetchScalarGridSpec(
            num_scalar_prefetch=2, grid=(n_blk,),
            in_specs=[pl.BlockSpec((rows, LANES), lambda j, be, nu: (live(j, be, nu), 0)),
                      pl.BlockSpec((1, D_MODEL, 2 * D_MODEL), lambda j, be, nu: (be[live(j, be, nu)], 0, 0)),
                      pl.BlockSpec((1, 1, 2 * D_MODEL), lambda j, be, nu: (be[live(j, be, nu)], 0, 0)),
                      pl.BlockSpec((1, D_MODEL, D_MODEL), lambda j, be, nu: (be[live(j, be, nu)], 0, 0)),
                      pl.BlockSpec((1, 1, D_MODEL), lambda j, be, nu: (be[live(j, be, nu)], 0, 0))],
            out_specs=pl.BlockSpec((rows, LANES), lambda j, be, nu: (sink(j, be, nu), 0)),
            scratch_shapes=[pltpu.VMEM((D_MODEL, 2 * D_MODEL), BF16), pltpu.VMEM((D_MODEL, D_MODEL), BF16)]),
        out_shape=jax.ShapeDtypeStruct(xs_rows.shape, F32),
        compiler_params=pltpu.CompilerParams(dimension_semantics=("arbitrary",), vmem_limit_bytes=VMEM_LIMIT),
        name="expert_ffn",
    )(blk_e, n_used, xs_rows, w_gate_up, b_gate_up.reshape(N_EXPERTS, 1, 2 * D_MODEL),
      w_down, b_down.reshape(N_EXPERTS, 1, D_MODEL))


def _combine_kernel(dest_hbm, ys_hbm, x1_ref, rg_ref, o_ref, idx_s, buf, sem_i, sem_r):
    i = pl.program_id(0)
    n_move = TOP_K * TC_MOVE
    idx_copy = pltpu.make_async_copy(dest_hbm.at[i], idx_s, sem_i)
    idx_copy.start()
    idx_copy.wait()

    def issue(jo, carry):
        for u in range(DMA_UNROLL):
            j = jo * DMA_UNROLL + u
            _row_copy(ys_hbm, idx_s[j], buf, j, sem_r).start(priority=u % 2)
        return carry

    lax.fori_loop(0, n_move // DMA_UNROLL, issue, 0)

    def drain(jo, carry):
        for u in range(DMA_UNROLL):
            _row_copy(ys_hbm, 0, buf, 0, sem_r).wait()
        return carry

    lax.fori_loop(0, n_move // DMA_UNROLL, drain, 0)

    for c in range(ROW_CHUNKS):
        acc = x1_ref[:, c * LANES:(c + 1) * LANES]
        for k in range(TOP_K):
            rows = buf[pl.ds(k * TC_MOVE * ROW_CHUNKS + c, TC_MOVE, stride=ROW_CHUNKS), :]
            acc = acc + rg_ref[:, k:k + 1] * rows
        o_ref[:, c * LANES:(c + 1) * LANES] = acc


def _combine(dest_tiles, ys_rows, x1, rg):
    n = x1.shape[0]
    nt = dest_tiles.shape[0]
    return pl.pallas_call(
        _combine_kernel,
        grid=(nt,),
        in_specs=[pl.BlockSpec(memory_space=pl.ANY), pl.BlockSpec(memory_space=pl.ANY),
                  pl.BlockSpec((TC_MOVE, D_MODEL), lambda i: (i, 0)),
                  pl.BlockSpec((TC_MOVE, LANES), lambda i: (i, 0))],
        out_specs=pl.BlockSpec((TC_MOVE, D_MODEL), lambda i: (i, 0)),
        out_shape=jax.ShapeDtypeStruct((n, D_MODEL), F32),
        scratch_shapes=[pltpu.SMEM((TOP_K * TC_MOVE,), jnp.int32),
                        pltpu.VMEM((TOP_K * TC_MOVE * ROW_CHUNKS, LANES), F32),
                        pltpu.SemaphoreType.DMA, pltpu.SemaphoreType.DMA],
        compiler_params=pltpu.CompilerParams(dimension_semantics=("arbitrary",), vmem_limit_bytes=VMEM_LIMIT),
        name="combine_rows",
    )(dest_tiles, ys_rows, x1, rg)


def _layer(x, ln1_g, w_in, b_forget, b_gate, q_norm_a, k_norm_a, q_norm_b, k_norm_b, rel_bias,
           w_branch_a, w_branch_b, w_out, ln2_g, w_router, b_router, w_gate_up, b_gate_up, w_down, b_down):
    batch, seq, _ = x.shape
    n = batch * seq
    x2 = x.reshape(n, D_MODEL)
    *qkv_groups, qkvb, lf, gate = _in_projection(x2, batch, seq, ln1_g, w_in, b_forget, b_gate,
                                                 q_norm_a, k_norm_a, q_norm_b, k_norm_b)

    oas, lss = [], []
    for g in range(DSWA_GROUPS):
        o, lse = _dilated_attention(qkv_groups[g], rel_bias, g, batch, seq)
        oas.append(o)
        lss.append(lse)

    lf_rows = jnp.transpose(lf[:, :FOX_HEADS].reshape(batch, seq, FOX_HEADS), (0, 2, 1))
    c_rows = _forget_cumsum(lf_rows)
    obt = _forgetting_attention(qkvb, c_rows, batch, seq)

    x1, h2_rows, ri, rg, cnt = _merge_and_route(x2, batch, seq, oas, lss, obt, gate, w_branch_a, w_branch_b,
                                                w_out, ln2_g, w_router, b_router)

    counts = cnt[0, :N_EXPERTS].astype(jnp.int32)
    pcounts = ((counts + T_EXPERT - 1) // T_EXPERT) * T_EXPERT
    pends = jnp.cumsum(pcounts)
    pstarts = pends - pcounts
    idx, rank = ri[:, :TOP_K], ri[:, TOP_K:2 * TOP_K]
    start_of = jnp.sum(jnp.where(idx[..., None] == jnp.arange(N_EXPERTS), pstarts, 0), axis=-1)
    dest = start_of + rank
    nt = n // TC_MOVE
    dest_tiles = jnp.transpose(dest.reshape(nt, TC_MOVE, TOP_K), (0, 2, 1)).reshape(nt, TOP_K * TC_MOVE)
    n_slots = n * TOP_K + N_EXPERTS * T_EXPERT
    n_blk = n_slots // T_EXPERT
    blk_first = jnp.arange(n_blk, dtype=jnp.int32) * T_EXPERT
    blk_e = jnp.minimum(jnp.sum((pends[None, :] <= blk_first[:, None]).astype(jnp.int32), axis=1), N_EXPERTS - 1)
    n_used = (pends[-1:] // T_EXPERT).astype(jnp.int32)
    zero_blk = jnp.maximum(pends // T_EXPERT - 1, 0).astype(jnp.int32)
    zero_flag = (pcounts > 0).astype(jnp.int32)

    xs_rows = _dispatch(dest_tiles, zero_blk, zero_flag, h2_rows, n_slots)
    ys_rows = _expert_ffn(blk_e, n_used, xs_rows, w_gate_up, b_gate_up, w_down, b_down)
    out = _combine(dest_tiles, ys_rows, x1, rg)
    return out.reshape(batch, seq, D_MODEL)


def kernel(x, ln1_g, w_in, b_forget, b_gate, q_norm_a, k_norm_a, q_norm_b, k_norm_b, rel_bias, w_branch_a, w_branch_b, w_out, ln2_g, w_router, b_router, w_gate_up, b_gate_up, w_down, b_down):
    for layer in range(ln1_g.shape[0]):
        x = _layer(x, ln1_g[layer], w_in[layer], b_forget[layer], b_gate[layer], q_norm_a[layer],
                   k_norm_a[layer], q_norm_b[layer], k_norm_b[layer], rel_bias, w_branch_a[layer],
                   w_branch_b[layer], w_out[layer], ln2_g[layer], w_router[layer], b_router[layer],
                   w_gate_up[layer], b_gate_up[layer], w_down[layer], b_down[layer])
    return x
```

```python
import functools
import math

import numpy as np
import jax
import jax.numpy as jnp
from jax import lax
from jax.experimental import pallas as pl
from jax.experimental.pallas import tpu as pltpu

F32 = jnp.float32
BF16 = jnp.bfloat16

D_MODEL = 1024
HEAD_DIM = 64
DSWA_PATTERNS = ((128, 1), (512, 4), (2048, 16))
DSWA_HG = 4
DSWA_GROUPS = len(DSWA_PATTERNS)
DSWA_WIDTH = DSWA_GROUPS * DSWA_HG * HEAD_DIM
DSWA_OUT = DSWA_HG * HEAD_DIM
FOX_HEADS = 8
FOX_WIDTH = FOX_HEADS * HEAD_DIM
ATTN_BLOCK = 128
NUM_BUCKETS = 32
MAX_DISTANCE = 2048
N_EXPERTS = 32
TOP_K = 4
SWIGLU_LIMIT = 7.0
SWIGLU_ALPHA = 1.702
RMS_EPS = 1e-6
NEG_INF = -1e30
LOG2E = 1.4426950408889634

LANES = 128
SUBLANES = 8
ROW_CHUNKS = D_MODEL // LANES
QKV_WIDTH = 3 * DSWA_WIDTH + 3 * FOX_WIDTH
VMEM_LIMIT = 56 * 1024 * 1024

TM_PROJ = 512
TQ_FOX = 512
FOX_HEADS_PER_LOOP = 4
DSWA_BLOCKS_PER_STEP = 4
TM_MERGE = 512
T_EXPERT = 512
TC_MOVE = 512
DMA_UNROLL = 16


def _dot(a, b):
    return jnp.dot(a, b, preferred_element_type=F32)


def _dot_nt(a, b):
    return lax.dot_general(a, b, (((1,), (1,)), ((), ())), preferred_element_type=F32)


def _inproj_kernel(x_ref, g1_ref, wq_ref, wf_ref, wg_ref, bf_ref, bg_ref, nrm_ref, e_ref,
                   qkv0_ref, qkv1_ref, qkv2_ref, qkvb_ref, lf_ref, gate_ref, stage_ref):
    group_refs = (qkv0_ref, qkv1_ref, qkv2_ref)
    x = x_ref[...]
    ms = jnp.mean(x * x, axis=-1, keepdims=True)
    h = (x * lax.rsqrt(ms + RMS_EPS) * g1_ref[...]).astype(BF16)
    ones_bd = e_ref[...]

    def head_norm(y, gn):
        sq = y * y
        hi = sq.astype(BF16)
        lo = (sq - hi.astype(F32)).astype(BF16)
        ss = _dot(hi, ones_bd) + _dot(lo, ones_bd)
        return y * lax.rsqrt(ss * (1.0 / HEAD_DIM) + RMS_EPS) * gn

    n_a = 3 * DSWA_WIDTH // 256
    for c in range(QKV_WIDTH // 256):
        y = _dot(h, wq_ref[:, c * 256:(c + 1) * 256])
        if c < 3:
            y = head_norm(y, nrm_ref[0:1, :])
        elif c < 6:
            y = head_norm(y, nrm_ref[1:2, :])
        elif 9 <= c < 11:
            y = head_norm(y, nrm_ref[2:3, :])
        elif 11 <= c < 13:
            y = head_norm(y, nrm_ref[3:4, :])
        if c < n_a:
            g, part = c % DSWA_GROUPS, c // DSWA_GROUPS
            dil = DSWA_PATTERNS[g][1]
            if dil == 1:
                group_refs[g][0, :, part * 256:(part + 1) * 256] = y.astype(BF16)
            else:
                for half in range(2):
                    stage_ref[half] = y[:, half * LANES:(half + 1) * LANES]
                for r in range(dil):
                    for half in range(2):
                        col = r * DSWA_WIDTH + part * 256 + half * LANES
                        rows = stage_ref[half, pl.ds(r, TM_PROJ // dil, stride=dil), :]
                        group_refs[g][0, :, col:col + LANES] = rows.astype(BF16)
        else:
            qkvb_ref[:, (c - n_a) * 256:(c - n_a + 1) * 256] = y.astype(BF16)

    z = _dot(h, wf_ref[...]) + bf_ref[...]
    lf_ref[...] = jnp.minimum(z, 0.0) - jnp.log1p(jnp.exp(-jnp.abs(z)))
    for c in range(2 * D_MODEL // 256):
        gl = _dot(h, wg_ref[:, c * 256:(c + 1) * 256]) + bg_ref[:, c * 256:(c + 1) * 256]
        gate_ref[:, c * 256:(c + 1) * 256] = jax.nn.sigmoid(gl)


def _in_projection(x2, batch, seq, ln1_g, w_in, b_forget, b_gate, q_norm_a, k_norm_a, q_norm_b, k_norm_b):
    n = x2.shape[0]
    assert seq % TM_PROJ == 0 and all(TM_PROJ % (16 * d) == 0 for _, d in DSWA_PATTERNS)
    tiles_per_seq = seq // TM_PROJ
    wq = w_in[:, :QKV_WIDTH].astype(BF16)
    wf = jnp.pad(w_in[:, QKV_WIDTH:QKV_WIDTH + FOX_HEADS], ((0, 0), (0, LANES - FOX_HEADS))).astype(BF16)
    wg = w_in[:, QKV_WIDTH + FOX_HEADS:].astype(BF16)
    bf = jnp.pad(b_forget, (0, LANES - FOX_HEADS)).reshape(1, LANES)
    bg = b_gate.reshape(1, 2 * D_MODEL)
    scale = HEAD_DIM ** -0.5
    nrm = jnp.stack([jnp.tile(q_norm_a, 4) * scale, jnp.tile(k_norm_a, 4),
                     jnp.tile(q_norm_b, 4) * (scale * LOG2E), jnp.tile(k_norm_b, 4)])
    head_id = np.arange(256) // HEAD_DIM
    ones_bd = jnp.asarray(head_id[:, None] == head_id[None, :], BF16)
    const = lambda shape: pl.BlockSpec(shape, lambda i: (0,) * len(shape), pipeline_mode=pl.Buffered(1))
    row = lambda w: pl.BlockSpec((TM_PROJ, w), lambda i: (i, 0))
    group_spec = lambda d: pl.BlockSpec((1, TM_PROJ // d, d * DSWA_WIDTH),
                                        lambda i: (i // tiles_per_seq, i % tiles_per_seq, 0))
    group_shape = lambda d: jax.ShapeDtypeStruct((batch, seq // d, d * DSWA_WIDTH), BF16)
    dils = [d for _, d in DSWA_PATTERNS]
    return pl.pallas_call(
        _inproj_kernel,
        grid=(n // TM_PROJ,),
        in_specs=[row(D_MODEL), const((1, D_MODEL)), const((D_MODEL, QKV_WIDTH)), const((D_MODEL, LANES)),
                  const((D_MODEL, 2 * D_MODEL)), const((1, LANES)), const((1, 2 * D_MODEL)),
                  const((4, 256)), const((256, 256))],
        out_specs=[group_spec(d) for d in dils] + [row(3 * FOX_WIDTH), row(LANES), row(2 * D_MODEL)],
        out_shape=[group_shape(d) for d in dils] + [jax.ShapeDtypeStruct((n, 3 * FOX_WIDTH), BF16),
                                                    jax.ShapeDtypeStruct((n, LANES), F32),
                                                    jax.ShapeDtypeStruct((n, 2 * D_MODEL), F32)],
        scratch_shapes=[pltpu.VMEM((2, TM_PROJ, LANES), F32)],
        compiler_params=pltpu.CompilerParams(dimension_semantics=("arbitrary",), vmem_limit_bytes=VMEM_LIMIT),
        name="in_projection",
    )(x2, ln1_g.reshape(1, D_MODEL), wq, wf, wg, bf, bg, nrm, ones_bd)


def _cumsum_kernel(lf_ref, c_ref):
    v = lf_ref[0]
    seq = v.shape[1]
    lane = lax.broadcasted_iota(jnp.int32, v.shape, 1)
    shift = 1
    while shift < seq:
        v = v + jnp.where(lane >= shift, pltpu.roll(v, shift, axis=1), 0.0)
        shift *= 2
    c_ref[0] = v


def _forget_cumsum(lf_rows):
    b, hds, seq = lf_rows.shape
    spec = pl.BlockSpec((1, hds, seq), lambda i: (i, 0, 0))
    return pl.pallas_call(
        _cumsum_kernel, grid=(b,), in_specs=[spec], out_specs=spec,
        out_shape=jax.ShapeDtypeStruct((b, hds, seq), F32), name="forget_cumsum",
    )(lf_rows)


def _t5_bucket(n):
    max_exact = NUM_BUCKETS // 2
    n_safe = np.maximum(n, 1).astype(np.float64)
    large = max_exact + (np.log(n_safe / max_exact) / np.log(MAX_DISTANCE / max_exact)
                         * (NUM_BUCKETS - max_exact)).astype(np.int64)
    large = np.minimum(large, NUM_BUCKETS - 1)
    return np.where(n < max_exact, n, large).astype(np.int32)


def _dswa_kernel(q_ref, kp_ref, kc_ref, vp_ref, vc_ref, bias_ref, o_ref, l_ref, *, blocks):
    first = pl.program_id(2) == 0
    items = [(j, h) for j in range(blocks) for h in range(DSWA_HG)]
    rows = lambda j: slice(j * ATTN_BLOCK, (j + 1) * ATTN_BLOCK)
    head = lambda h: slice(h * HEAD_DIM, (h + 1) * HEAD_DIM)

    sps, scs = [], []
    for j, h in items:
        q = q_ref[0, rows(j), head(h)]
        k_prev = kp_ref[0, :, head(h)] if j == 0 else kc_ref[0, rows(j - 1), head(h)]
        sp = _dot_nt(q, k_prev) + bias_ref[h, :, :ATTN_BLOCK]
        if j == 0:
            sp = jnp.where(first, NEG_INF, sp)
        sps.append(sp)
        scs.append(_dot_nt(q, kc_ref[0, rows(j), head(h)]) + bias_ref[h, :, ATTN_BLOCK:])
    ms = [jnp.maximum(jnp.max(sp, axis=-1, keepdims=True), jnp.max(sc, axis=-1, keepdims=True))
          for sp, sc in zip(sps, scs)]
    pps = [jnp.exp(sp - m) for sp, m in zip(sps, ms)]
    pcs = [jnp.exp(sc - m) for sc, m in zip(scs, ms)]
    dens = [jnp.sum(pp, axis=-1, keepdims=True) + jnp.sum(pc, axis=-1, keepdims=True) for pp, pc in zip(pps, pcs)]
    for (j, h), pp, pc, m, den in zip(items, pps, pcs, ms, dens):
        v_prev = vp_ref[0, :, head(h)] if j == 0 else vc_ref[0, rows(j - 1), head(h)]
        o = _dot(pp.astype(BF16), v_prev) + _dot(pc.astype(BF16), vc_ref[0, rows(j), head(h)])
        o_ref[0, rows(j), head(h)] = o / den
        l_ref[0, rows(j), head(h)] = jnp.broadcast_to(m + jnp.log(den), (ATTN_BLOCK, HEAD_DIM))


def _dilated_attention(qkv3, rel_bias, g, batch, seq):
    window, dil = DSWA_PATTERNS[g]
    steps = window // dil
    assert steps == ATTN_BLOCK
    length = seq // dil
    assert length % ATTN_BLOCK == 0
    nblk = length // ATTN_BLOCK
    span = 2 * ATTN_BLOCK
    back = np.arange(ATTN_BLOCK)[:, None] + steps - np.arange(span)[None, :]
    valid = (back >= 0) & (back <= steps)
    bucket = _t5_bucket(np.clip(back, 0, steps) * dil)
    cols = rel_bias[:, g * DSWA_HG:(g + 1) * DSWA_HG]
    onehot = np.asarray(bucket.reshape(-1)[:, None] == np.arange(NUM_BUCKETS)[None, :], np.float32)
    bias = jnp.dot(jnp.asarray(onehot), cols.astype(F32), precision=lax.Precision.HIGHEST)
    bias = jnp.transpose(bias.reshape(ATTN_BLOCK, span, DSWA_HG), (2, 0, 1))
    bias = jnp.where(valid[None], bias, NEG_INF)

    per_pos = DSWA_WIDTH // 256
    nb = math.gcd(nblk, DSWA_BLOCKS_PER_STEP)
    blk = (1, nb * ATTN_BLOCK, 256)
    one = (1, ATTN_BLOCK, 256)
    before = lambda n: jnp.maximum(n * nb - 1, 0)
    q_spec = pl.BlockSpec(blk, lambda b, r, n: (b, n, r * per_pos))
    kc_spec = pl.BlockSpec(blk, lambda b, r, n: (b, n, r * per_pos + 1))
    kp_spec = pl.BlockSpec(one, lambda b, r, n: (b, before(n), r * per_pos + 1))
    vc_spec = pl.BlockSpec(blk, lambda b, r, n: (b, n, r * per_pos + 2))
    vp_spec = pl.BlockSpec(one, lambda b, r, n: (b, before(n), r * per_pos + 2))
    bias_spec = pl.BlockSpec((DSWA_HG, ATTN_BLOCK, span), lambda b, r, n: (0, 0, 0))
    o_spec = pl.BlockSpec(blk, lambda b, r, n: (b, n, r))
    return pl.pallas_call(
        functools.partial(_dswa_kernel, blocks=nb),
        grid=(batch, dil, nblk // nb),
        in_specs=[q_spec, kp_spec, kc_spec, vp_spec, vc_spec, bias_spec],
        out_specs=[o_spec, o_spec],
        out_shape=[jax.ShapeDtypeStruct((batch, length, dil * DSWA_OUT), F32)] * 2,
        compiler_params=pltpu.CompilerParams(dimension_semantics=("arbitrary",) * 3),
        name=f"dilated_attention_{g}",
    )(qkv3, qkv3, qkv3, qkv3, qkv3, bias)


def _fox_kernel(q_ref, k_ref, vt_ref, o_ref):
    qi = pl.program_id(1)
    tq = TQ_FOX
    krow = lax.broadcasted_iota(jnp.int32, (tq, tq), 0)
    qcol = lax.broadcasted_iota(jnp.int32, (tq, tq), 1)
    causal = krow <= qcol

    for hg in range(FOX_HEADS // FOX_HEADS_PER_LOOP):
        heads = [hg * FOX_HEADS_PER_LOOP + i for i in range(FOX_HEADS_PER_LOOP)]
        qs = [q_ref[0, :, h * LANES:(h + 1) * LANES] for h in heads]

        def step(kb, carry, masked):
            ks = pl.multiple_of(kb * tq, tq)
            idx = range(len(heads))
            sts = []
            for i, h in enumerate(heads):
                st = _dot_nt(k_ref[0, pl.ds(ks, tq), h * LANES:(h + 1) * LANES], qs[i])
                sts.append(jnp.where(causal, st, NEG_INF) if masked else st)
            m_new = [jnp.maximum(carry[i][0], jnp.max(sts[i], axis=0, keepdims=True)) for i in idx]
            alpha = [jnp.exp2(carry[i][0] - m_new[i]) for i in idx]
            ps = [jnp.exp2(sts[i] - m_new[i]) for i in idx]
            ls = [alpha[i] * carry[i][1] + jnp.sum(ps[i], axis=0, keepdims=True) for i in idx]
            out = []
            for i, h in enumerate(heads):
                vt = vt_ref[0, h * HEAD_DIM:(h + 1) * HEAD_DIM, pl.ds(ks, tq)]
                out.append((m_new[i], ls[i], alpha[i] * carry[i][2] + _dot(vt, ps[i].astype(BF16))))
            return tuple(out)

        init = tuple((jnp.full((1, tq), NEG_INF, F32), jnp.zeros((1, tq), F32), jnp.zeros((HEAD_DIM, tq), F32))
                     for _ in heads)
        carry = lax.fori_loop(0, qi, lambda kb, c: step(kb, c, False), init)
        carry = step(qi, carry, True)
        for i, h in enumerate(heads):
            _, l, acc = carry[i]
            o_ref[0, h * HEAD_DIM:(h + 1) * HEAD_DIM, :] = (acc / l).astype(BF16)


def _top16(v):
    bits = lax.bitcast_convert_type(v, jnp.uint32) & jnp.uint32(0xFFFF0000)
    return lax.bitcast_convert_type(bits, F32)


def _split3_bf16(c):
    hi = _top16(c)
    r1 = c - hi
    mid = _top16(r1)
    lo = r1 - mid
    return hi.astype(BF16), mid.astype(BF16), lo.astype(BF16)


def _forgetting_attention(qkvb, c_rows, batch, seq):
    qkv = qkvb.reshape(batch, seq, 3, FOX_HEADS, HEAD_DIM)
    hi, mid, lo = _split3_bf16(c_rows * LOG2E)
    one = jnp.ones_like(hi)
    pad = jnp.zeros((batch, seq, FOX_HEADS, LANES - HEAD_DIM - 6), BF16)
    to_cols = lambda parts: jnp.transpose(jnp.stack(parts, axis=-1), (0, 2, 1, 3))
    q_aug = jnp.concatenate([qkv[:, :, 0], to_cols([one, one, one, hi, mid, lo]), pad], axis=-1)
    k_aug = jnp.concatenate([qkv[:, :, 1], to_cols([-hi, -mid, -lo, one, one, one]), pad], axis=-1)
    q_aug = q_aug.reshape(batch, seq, FOX_HEADS * LANES)
    k_aug = k_aug.reshape(batch, seq, FOX_HEADS * LANES)
    v_t = jnp.transpose(qkv[:, :, 2].reshape(batch, seq, FOX_WIDTH), (0, 2, 1))
    nq = seq // TQ_FOX
    return pl.pallas_call(
        _fox_kernel,
        grid=(batch, nq),
        in_specs=[pl.BlockSpec((1, TQ_FOX, FOX_HEADS * LANES), lambda b, i: (b, i, 0)),
                  pl.BlockSpec((1, seq, FOX_HEADS * LANES), lambda b, i: (b, 0, 0)),
                  pl.BlockSpec((1, FOX_WIDTH, seq), lambda b, i: (b, 0, 0))],
        out_specs=pl.BlockSpec((1, FOX_WIDTH, TQ_FOX), lambda b, i: (b, 0, i)),
        out_shape=jax.ShapeDtypeStruct((batch, FOX_WIDTH, seq), BF16),
        compiler_params=pltpu.CompilerParams(dimension_semantics=("arbitrary", "arbitrary"),
                                             vmem_limit_bytes=VMEM_LIMIT),
        name="forgetting_attention",
    )(q_aug, k_aug, v_t)


def _merge_kernel(x_ref, oa0_ref, oa1_ref, oa2_ref, ls0_ref, ls1_ref, ls2_ref, obt_ref, gate_ref,
                  wa_ref, wb_ref, wo_ref, g2_ref, wrh_ref, wrl_ref, br_ref,
                  x1_ref, h2_ref, ri_ref, rg_ref, cnt_ref, run_s, stage_s):
    tm = TM_MERGE

    @pl.when(pl.program_id(0) == 0)
    def _():
        run_s[...] = jnp.zeros_like(run_s)

    def token_order(ref, g, stage):
        dil = DSWA_PATTERNS[g][1]
        if dil == 1:
            return ref[0]
        for r in range(dil):
            for half in range(2):
                col = r * DSWA_OUT + half * LANES
                stage[half, pl.ds(r, tm // dil, stride=dil), :] = ref[0, :, col:col + LANES]
        return jnp.concatenate([stage[0], stage[1]], axis=1)

    l0, l1, l2 = (token_order(ref, g, stage_s.at[g]) for g, ref in enumerate((ls0_ref, ls1_ref, ls2_ref)))
    lm = jnp.maximum(jnp.maximum(l0, l1), l2)
    e0, e1, e2 = jnp.exp(l0 - lm), jnp.exp(l1 - lm), jnp.exp(l2 - lm)
    o0, o1, o2 = (token_order(ref, g, stage_s.at[DSWA_GROUPS + g])
                  for g, ref in enumerate((oa0_ref, oa1_ref, oa2_ref)))
    o_a = (e0 * o0 + e1 * o1 + e2 * o2) / (e0 + e1 + e2)
    ya = _dot(o_a.astype(BF16), wa_ref[...])
    yb = lax.dot_general(obt_ref[0], wb_ref[...], (((0,), (0,)), ((), ())), preferred_element_type=F32)
    merged = gate_ref[:, :D_MODEL] * ya + gate_ref[:, D_MODEL:] * yb
    x1 = x_ref[...] + _dot(merged.astype(BF16), wo_ref[...])
    x1_ref[...] = x1

    ms = jnp.mean(x1 * x1, axis=-1, keepdims=True)
    h2 = x1 * lax.rsqrt(ms + RMS_EPS) * g2_ref[...]
    for c in range(ROW_CHUNKS):
        h2_ref[pl.ds(c, tm, stride=ROW_CHUNKS), :] = h2[:, c * LANES:(c + 1) * LANES]

    hi = h2.astype(BF16)
    lo = (h2 - hi.astype(F32)).astype(BF16)
    logits = _dot(hi, wrh_ref[...]) + _dot(lo, wrh_ref[...]) + _dot(hi, wrl_ref[...]) + br_ref[...]

    lane = lax.broadcasted_iota(jnp.int32, (tm, LANES), 1).astype(F32)
    work = logits
    top_v, top_i = [], []
    for _ in range(TOP_K):
        mk = jnp.max(work, axis=-1, keepdims=True)
        ik = jnp.min(jnp.where(work == mk, lane, float(LANES)), axis=-1, keepdims=True)
        top_v.append(mk)
        top_i.append(ik)
        work = jnp.where(lane == ik, -jnp.inf, work)
    ex = [jnp.exp(v - top_v[0]) for v in top_v]
    den = ex[0] + ex[1] + ex[2] + ex[3]

    onehot = jnp.zeros((tm, LANES), F32)
    for ik in top_i:
        onehot = onehot + jnp.where(lane == ik, 1.0, 0.0)
    r_i = lax.broadcasted_iota(jnp.int32, (tm, tm), 0)
    c_i = lax.broadcasted_iota(jnp.int32, (tm, tm), 1)
    tri = jnp.where(c_i < r_i, 1.0, 0.0).astype(BF16)
    before = _dot(tri, onehot.astype(BF16)) + run_s[...]
    ri = jnp.zeros((tm, LANES), jnp.int32)
    rg = jnp.zeros((tm, LANES), F32)
    for k in range(TOP_K):
        rank = jnp.sum(jnp.where(lane == top_i[k], before, 0.0), axis=-1, keepdims=True)
        ri = jnp.where(lane == k, top_i[k].astype(jnp.int32), ri)
        ri = jnp.where(lane == TOP_K + k, rank.astype(jnp.int32), ri)
        rg = jnp.where(lane == k, ex[k] / den, rg)
    ri_ref[...] = ri
    rg_ref[...] = rg
    run = run_s[...] + jnp.sum(onehot, axis=0, keepdims=True)
    run_s[...] = run
    cnt_ref[...] = run


def _merge_and_route(x2, batch, seq, oas, lss, obt, gate, w_branch_a, w_branch_b, w_out, ln2_g, w_router,
                     b_router):
    n = x2.shape[0]
    assert seq % TM_MERGE == 0 and all(TM_MERGE % (SUBLANES * d) == 0 for _, d in DSWA_PATTERNS)
    tiles_per_seq = seq // TM_MERGE
    seq_tile = lambda i: (i // tiles_per_seq, i % tiles_per_seq)
    group_specs = [pl.BlockSpec((1, TM_MERGE // d, d * DSWA_OUT), lambda i: (*seq_tile(i), 0))
                   for _, d in DSWA_PATTERNS]
    obt_spec = pl.BlockSpec((1, FOX_WIDTH, TM_MERGE), lambda i: (seq_tile(i)[0], 0, seq_tile(i)[1]))
    wr = jnp.pad(w_router, ((0, 0), (0, LANES - N_EXPERTS)))
    wr_top = _top16(wr)
    wr_hi = wr_top.astype(BF16)
    wr_lo = (wr - wr_top).astype(BF16)
    br = jnp.concatenate([b_router, jnp.full((LANES - N_EXPERTS,), NEG_INF, F32)]).reshape(1, LANES)
    const = lambda shape: pl.BlockSpec(shape, lambda i: (0,) * len(shape), pipeline_mode=pl.Buffered(1))
    row = lambda w: pl.BlockSpec((TM_MERGE, w), lambda i: (i, 0))
    return pl.pallas_call(
        _merge_kernel,
        grid=(n // TM_MERGE,),
        in_specs=[row(D_MODEL)] + group_specs + group_specs + [obt_spec, row(2 * D_MODEL),
                  const((DSWA_OUT, D_MODEL)), const((FOX_WIDTH, D_MODEL)), const((D_MODEL, D_MODEL)),
                  const((1, D_MODEL)), const((D_MODEL, LANES)), const((D_MODEL, LANES)), const((1, LANES))],
        out_specs=[row(D_MODEL), pl.BlockSpec((TM_MERGE * ROW_CHUNKS, LANES), lambda i: (i, 0)),
                   row(LANES), row(LANES), pl.BlockSpec((1, LANES), lambda i: (0, 0))],
        out_shape=[jax.ShapeDtypeStruct((n, D_MODEL), F32),
                   jax.ShapeDtypeStruct((n * ROW_CHUNKS, LANES), F32),
                   jax.ShapeDtypeStruct((n, LANES), jnp.int32),
                   jax.ShapeDtypeStruct((n, LANES), F32),
                   jax.ShapeDtypeStruct((1, LANES), F32)],
        scratch_shapes=[pltpu.VMEM((1, LANES), F32), pltpu.VMEM((2 * DSWA_GROUPS, 2, TM_MERGE, LANES), F32)],
        compiler_params=pltpu.CompilerParams(dimension_semantics=("arbitrary",), vmem_limit_bytes=VMEM_LIMIT),
        name="merge_and_route",
    )(x2, *oas, *lss, obt, gate, w_branch_a.astype(BF16), w_branch_b.astype(BF16), w_out.astype(BF16),
      ln2_g.reshape(1, D_MODEL), wr_hi, wr_lo, br)


def _row_copy(src, src_row, dst, dst_row, sem):
    return pltpu.make_async_copy(src.at[pl.ds(pl.multiple_of(src_row * ROW_CHUNKS, ROW_CHUNKS), ROW_CHUNKS)],
                                 dst.at[pl.ds(pl.multiple_of(dst_row * ROW_CHUNKS, ROW_CHUNKS), ROW_CHUNKS)], sem)


def _dispatch_kernel(zb_ref, zf_ref, dest_hbm, h2_ref, xs_hbm, idx_s, zeros_v, sem_i, sem_z, sem_r):
    i = pl.program_id(0)
    n_move = TOP_K * TC_MOVE
    idx_copy = pltpu.make_async_copy(dest_hbm.at[i], idx_s, sem_i)
    idx_copy.start()

    def zero_copy(e):
        blk = pl.multiple_of(zb_ref[e] * (T_EXPERT * ROW_CHUNKS), T_EXPERT * ROW_CHUNKS)
        return pltpu.make_async_copy(zeros_v, xs_hbm.at[pl.ds(blk, T_EXPERT * ROW_CHUNKS)], sem_z)

    @pl.when(i == 0)
    def _():
        zeros_v[...] = jnp.zeros_like(zeros_v)
        for e in range(N_EXPERTS):
            @pl.when(zf_ref[e] > 0)
            def _():
                zero_copy(e).start()
        for e in range(N_EXPERTS):
            @pl.when(zf_ref[e] > 0)
            def _():
                zero_copy(e).wait()

    idx_copy.wait()

    def issue(jo, carry):
        for u in range(DMA_UNROLL):
            j = jo * DMA_UNROLL + u
            _row_copy(h2_ref, j & (TC_MOVE - 1), xs_hbm, idx_s[j], sem_r).start(priority=u % 2)
        return carry

    lax.fori_loop(0, n_move // DMA_UNROLL, issue, 0)

    def drain(jo, carry):
        for u in range(DMA_UNROLL):
            _row_copy(h2_ref, 0, xs_hbm, 0, sem_r).wait()
        return carry

    lax.fori_loop(0, n_move // DMA_UNROLL, drain, 0)


def _dispatch(dest_tiles, zero_blk, zero_flag, h2_rows, n_slots):
    nt = dest_tiles.shape[0]
    return pl.pallas_call(
        _dispatch_kernel,
        grid_spec=pltpu.PrefetchScalarGridSpec(
            num_scalar_prefetch=2, grid=(nt,),
            in_specs=[pl.BlockSpec(memory_space=pl.ANY),
                      pl.BlockSpec((TC_MOVE * ROW_CHUNKS, LANES), lambda i, zb, zf: (i, 0))],
            out_specs=pl.BlockSpec(memory_space=pl.ANY),
            scratch_shapes=[pltpu.SMEM((TOP_K * TC_MOVE,), jnp.int32),
                            pltpu.VMEM((T_EXPERT * ROW_CHUNKS, LANES), F32),
                            pltpu.SemaphoreType.DMA, pltpu.SemaphoreType.DMA, pltpu.SemaphoreType.DMA]),
        out_shape=jax.ShapeDtypeStruct((n_slots * ROW_CHUNKS, LANES), F32),
        compiler_params=pltpu.CompilerParams(dimension_semantics=("arbitrary",)),
        name="dispatch_rows",
    )(zero_blk, zero_flag, dest_tiles, h2_rows)


def _expert_kernel(be_ref, nu_ref, xs_ref, wgu_ref, bgu_ref, wd_ref, bd_ref, ys_ref, wgu_s, wd_s):
    j = pl.program_id(0)
    t = T_EXPERT

    @pl.when(j < nu_ref[0])
    def _():
        prev = be_ref[jnp.maximum(j - 1, 0)]

        @pl.when((j == 0) | (be_ref[j] != prev))
        def _():
            wgu_s[...] = wgu_ref[0].astype(BF16)
            wd_s[...] = wd_ref[0].astype(BF16)

        x = jnp.concatenate([xs_ref[pl.ds(c, t, stride=ROW_CHUNKS), :] for c in range(ROW_CHUNKS)], axis=1)
        gu = _dot(x.astype(BF16), wgu_s[...]) + bgu_ref[0]
        g = jnp.minimum(gu[:, :D_MODEL], SWIGLU_LIMIT)
        u = jnp.clip(gu[:, D_MODEL:], -SWIGLU_LIMIT, SWIGLU_LIMIT)
        mid = (u + 1.0) * (g * jax.nn.sigmoid(SWIGLU_ALPHA * g))
        y = _dot(mid.astype(BF16), wd_s[...]) + bd_ref[0]
        for c in range(ROW_CHUNKS):
            ys_ref[pl.ds(c, t, stride=ROW_CHUNKS), :] = y[:, c * LANES:(c + 1) * LANES]


def _expert_ffn(blk_e, n_used, xs_rows, w_gate_up, b_gate_up, w_down, b_down):
    n_blk = blk_e.shape[0]
    rows = T_EXPERT * ROW_CHUNKS
    live = lambda j, be, nu: jnp.minimum(j, nu[0] - 1)
    sink = lambda j, be, nu: jnp.where(j < nu[0], j, n_blk - 1)
    return pl.pallas_call(
        _expert_kernel,
        grid_spec=pltpu.PrefetchScalarGridSpec(
            num_scalar_prefetch=2, grid=(n_blk,),
            in_specs=[pl.BlockSpec((rows, LANES), lambda j, be, nu: (live(j, be, nu), 0)),
                      pl.BlockSpec((1, D_MODEL, 2 * D_MODEL), lambda j, be, nu: (be[live(j, be, nu)], 0, 0)),
                      pl.BlockSpec((1, 1, 2 * D_MODEL), lambda j, be, nu: (be[live(j, be, nu)], 0, 0)),
                      pl.BlockSpec((1, D_MODEL, D_MODEL), lambda j, be, nu: (be[live(j, be, nu)], 0, 0)),
                      pl.BlockSpec((1, 1, D_MODEL), lambda j, be, nu: (be[live(j, be, nu)], 0, 0))],
            out_specs=pl.BlockSpec((rows, LANES), lambda j, be, nu: (sink(j, be, nu), 0)),
            scratch_shapes=[pltpu.VMEM((D_MODEL, 2 * D_MODEL), BF16), pltpu.VMEM((D_MODEL, D_MODEL), BF16)]),
        out_shape=jax.ShapeDtypeStruct(xs_rows.shape, F32),
        compiler_params=pltpu.CompilerParams(dimension_semantics=("arbitrary",), vmem_limit_bytes=VMEM_LIMIT),
        name="expert_ffn",
    )(blk_e, n_used, xs_rows, w_gate_up, b_gate_up.reshape(N_EXPERTS, 1, 2 * D_MODEL),
      w_down, b_down.reshape(N_EXPERTS, 1, D_MODEL))


def _combine_kernel(dest_hbm, ys_hbm, x1_ref, rg_ref, o_ref, idx_s, buf, sem_i, sem_r):
    i = pl.program_id(0)
    n_move = TOP_K * TC_MOVE
    idx_copy = pltpu.make_async_copy(dest_hbm.at[i], idx_s, sem_i)
    idx_copy.start()
    idx_copy.wait()

    def issue(jo, carry):
        for u in range(DMA_UNROLL):
            j = jo * DMA_UNROLL + u
            _row_copy(ys_hbm, idx_s[j], buf, j, sem_r).start(priority=u % 2)
        return carry

    lax.fori_loop(0, n_move // DMA_UNROLL, issue, 0)

    def drain(jo, carry):
        for u in range(DMA_UNROLL):
            _row_copy(ys_hbm, 0, buf, 0, sem_r).wait()
        return carry

    lax.fori_loop(0, n_move // DMA_UNROLL, drain, 0)

    for c in range(ROW_CHUNKS):
        acc = x1_ref[:, c * LANES:(c + 1) * LANES]
        for k in range(TOP_K):
            rows = buf[pl.ds(k * TC_MOVE * ROW_CHUNKS + c, TC_MOVE, stride=ROW_CHUNKS), :]
            acc = acc + rg_ref[:, k:k + 1] * rows
        o_ref[:, c * LANES:(c + 1) * LANES] = acc


def _combine(dest_tiles, ys_rows, x1, rg):
    n = x1.shape[0]
    nt = dest_tiles.shape[0]
    return pl.pallas_call(
        _combine_kernel,
        grid=(nt,),
        in_specs=[pl.BlockSpec(memory_space=pl.ANY), pl.BlockSpec(memory_space=pl.ANY),
                  pl.BlockSpec((TC_MOVE, D_MODEL), lambda i: (i, 0)),
                  pl.BlockSpec((TC_MOVE, LANES), lambda i: (i, 0))],
        out_specs=pl.BlockSpec((TC_MOVE, D_MODEL), lambda i: (i, 0)),
        out_shape=jax.ShapeDtypeStruct((n, D_MODEL), F32),
        scratch_shapes=[pltpu.SMEM((TOP_K * TC_MOVE,), jnp.int32),
                        pltpu.VMEM((TOP_K * TC_MOVE * ROW_CHUNKS, LANES), F32),
                        pltpu.SemaphoreType.DMA, pltpu.SemaphoreType.DMA],
        compiler_params=pltpu.CompilerParams(dimension_semantics=("arbitrary",), vmem_limit_bytes=VMEM_LIMIT),
        name="combine_rows",
    )(dest_tiles, ys_rows, x1, rg)


def _layer(x, ln1_g, w_in, b_forget, b_gate, q_norm_a, k_norm_a, q_norm_b, k_norm_b, rel_bias,
           w_branch_a, w_branch_b, w_out, ln2_g, w_router, b_router, w_gate_up, b_gate_up, w_down, b_down):
    batch, seq, _ = x.shape
    n = batch * seq
    x2 = x.reshape(n, D_MODEL)
    *qkv_groups, qkvb, lf, gate = _in_projection(x2, batch, seq, ln1_g, w_in, b_forget, b_gate,
                                                 q_norm_a, k_norm_a, q_norm_b, k_norm_b)

    oas, lss = [], []
    for g in range(DSWA_GROUPS):
        o, lse = _dilated_attention(qkv_groups[g], rel_bias, g, batch, seq)
        oas.append(o)
        lss.append(lse)

    lf_rows = jnp.transpose(lf[:, :FOX_HEADS].reshape(batch, seq, FOX_HEADS), (0, 2, 1))
    c_rows = _forget_cumsum(lf_rows)
    obt = _forgetting_attention(qkvb, c_rows, batch, seq)

    x1, h2_rows, ri, rg, cnt = _merge_and_route(x2, batch, seq, oas, lss, obt, gate, w_branch_a, w_branch_b,
                                                w_out, ln2_g, w_router, b_router)

    counts = cnt[0, :N_EXPERTS].astype(jnp.int32)
    pcounts = ((counts + T_EXPERT - 1) // T_EXPERT) * T_EXPERT
    pends = jnp.cumsum(pcounts)
    pstarts = pends - pcounts
    idx, rank = ri[:, :TOP_K], ri[:, TOP_K:2 * TOP_K]
    start_of = jnp.sum(jnp.where(idx[..., None] == jnp.arange(N_EXPERTS), pstarts, 0), axis=-1)
    dest = start_of + rank
    nt = n // TC_MOVE
    dest_tiles = jnp.transpose(dest.reshape(nt, TC_MOVE, TOP_K), (0, 2, 1)).reshape(nt, TOP_K * TC_MOVE)
    n_slots = n * TOP_K + N_EXPERTS * T_EXPERT
    n_blk = n_slots // T_EXPERT
    blk_first = jnp.arange(n_blk, dtype=jnp.int32) * T_EXPERT
    blk_e = jnp.minimum(jnp.sum((pends[None, :] <= blk_first[:, None]).astype(jnp.int32), axis=1), N_EXPERTS - 1)
    n_used = (pends[-1:] // T_EXPERT).astype(jnp.int32)
    zero_blk = jnp.maximum(pends // T_EXPERT - 1, 0).astype(jnp.int32)
    zero_flag = (pcounts > 0).astype(jnp.int32)

    xs_rows = _dispatch(dest_tiles, zero_blk, zero_flag, h2_rows, n_slots)
    ys_rows = _expert_ffn(blk_e, n_used, xs_rows, w_gate_up, b_gate_up, w_down, b_down)
    out = _combine(dest_tiles, ys_rows, x1, rg)
    return out.reshape(batch, seq, D_MODEL)


def kernel(x, ln1_g, w_in, b_forget, b_gate, q_norm_a, k_norm_a, q_norm_b, k_norm_b, rel_bias, w_branch_a, w_branch_b, w_out, ln2_g, w_router, b_router, w_gate_up, b_gate_up, w_down, b_down):
    for layer in range(ln1_g.shape[0]):
        x = _layer(x, ln1_g[layer], w_in[layer], b_forget[layer], b_gate[layer], q_norm_a[layer],
                   k_norm_a[layer], q_norm_b[layer], k_norm_b[layer], rel_bias, w_branch_a[layer],
                   w_branch_b[layer], w_out[layer], ln2_g[layer], w_router[layer], b_router[layer],
                   w_gate_up[layer], b_gate_up[layer], w_down[layer], b_down[layer])
    return x
```

```python
import functools
import math

import numpy as np
import jax
import jax.numpy as jnp
from jax import lax
from jax.experimental import pallas as pl
from jax.experimental.pallas import tpu as pltpu

F32 = jnp.float32
BF16 = jnp.bfloat16

D_MODEL = 1024
HEAD_DIM = 64
DSWA_PATTERNS = ((128, 1), (512, 4), (2048, 16))
DSWA_HG = 4
DSWA_GROUPS = len(DSWA_PATTERNS)
DSWA_WIDTH = DSWA_GROUPS * DSWA_HG * HEAD_DIM
DSWA_OUT = DSWA_HG * HEAD_DIM
FOX_HEADS = 8
FOX_WIDTH = FOX_HEADS * HEAD_DIM
ATTN_BLOCK = 128
NUM_BUCKETS = 32
MAX_DISTANCE = 2048
N_EXPERTS = 32
TOP_K = 4
SWIGLU_LIMIT = 7.0
SWIGLU_ALPHA = 1.702
RMS_EPS = 1e-6
NEG_INF = -1e30
LOG2E = 1.4426950408889634

LANES = 128
SUBLANES = 8
ROW_CHUNKS = D_MODEL // LANES
QKV_WIDTH = 3 * DSWA_WIDTH + 3 * FOX_WIDTH
FOX_Q_ONES = (64, 65, 66)
FOX_K_C = (64, 65, 66)
FOX_Q_C = (67, 68, 69)
FOX_K_ONES = (67, 68, 69)
VMEM_LIMIT = 56 * 1024 * 1024

TM_PROJ = 512
TQ_FOX = 512
FOX_HEADS_PER_LOOP = 4
DSWA_BLOCKS_PER_STEP = 4
TM_MERGE = 512
T_EXPERT = 512
TC_MOVE = 1024
DMA_UNROLL = 16


def _dot(a, b):
    return jnp.dot(a, b, preferred_element_type=F32)


def _dot_nt(a, b):
    return lax.dot_general(a, b, (((1,), (1,)), ((), ())), preferred_element_type=F32)


def _inproj_kernel(x_ref, g1_ref, wq_ref, wf_ref, wg_ref, bf_ref, bg_ref, nrm_ref, e_ref, place_ref, pat_ref,
                   qkv0_ref, qkv1_ref, qkv2_ref, qb_ref, kb_ref, vt_ref, lf_ref, gate_ref, stage_ref):
    group_refs = (qkv0_ref, qkv1_ref, qkv2_ref)
    x = x_ref[...]
    ms = jnp.mean(x * x, axis=-1, keepdims=True)
    h = (x * lax.rsqrt(ms + RMS_EPS) * g1_ref[...]).astype(BF16)
    ones_bd = e_ref[...]

    def head_norm(y, gn):
        sq = y * y
        hi = sq.astype(BF16)
        lo = (sq - hi.astype(F32)).astype(BF16)
        ss = _dot(hi, ones_bd) + _dot(lo, ones_bd)
        return y * lax.rsqrt(ss * (1.0 / HEAD_DIM) + RMS_EPS) * gn

    n_a = 3 * DSWA_WIDTH // 256
    for c in range(QKV_WIDTH // 256):
        y = _dot(h, wq_ref[:, c * 256:(c + 1) * 256])
        if c < 3:
            y = head_norm(y, nrm_ref[0:1, :])
        elif c < 6:
            y = head_norm(y, nrm_ref[1:2, :])
        elif 9 <= c < 11:
            y = head_norm(y, nrm_ref[2:3, :])
        elif 11 <= c < 13:
            y = head_norm(y, nrm_ref[3:4, :])
        if c < n_a:
            g, part = c % DSWA_GROUPS, c // DSWA_GROUPS
            dil = DSWA_PATTERNS[g][1]
            if dil == 1:
                group_refs[g][0, :, part * 256:(part + 1) * 256] = y.astype(BF16)
            else:
                for half in range(2):
                    stage_ref[half] = y[:, half * LANES:(half + 1) * LANES]
                for r in range(dil):
                    for half in range(2):
                        col = r * DSWA_WIDTH + part * 256 + half * LANES
                        rows = stage_ref[half, pl.ds(r, TM_PROJ // dil, stride=dil), :]
                        group_refs[g][0, :, col:col + LANES] = rows.astype(BF16)
        elif c < n_a + 4:
            is_k, half = divmod(c - n_a, 2)
            slots = _dot(y.astype(BF16), place_ref[...]) + pat_ref[is_k:is_k + 1, :]
            (kb_ref if is_k else qb_ref)[:, half * 512:(half + 1) * 512] = slots.astype(BF16)
        else:
            half = c - n_a - 4
            vt_ref[0, half * 256:(half + 1) * 256, :] = y.T.astype(BF16)

    z = _dot(h, wf_ref[...]) + bf_ref[...]
    lf_ref[...] = (jnp.minimum(z, 0.0) - jnp.log1p(jnp.exp(-jnp.abs(z)))) * LOG2E
    for c in range(2 * D_MODEL // 256):
        gl = _dot(h, wg_ref[:, c * 256:(c + 1) * 256]) + bg_ref[:, c * 256:(c + 1) * 256]
        gate_ref[:, c * 256:(c + 1) * 256] = jax.nn.sigmoid(gl)


def _in_projection(x2, batch, seq, ln1_g, w_in, b_forget, b_gate, q_norm_a, k_norm_a, q_norm_b, k_norm_b):
    n = x2.shape[0]
    assert seq % TM_PROJ == 0 and all(TM_PROJ % (16 * d) == 0 for _, d in DSWA_PATTERNS)
    tiles_per_seq = seq // TM_PROJ
    wq = w_in[:, :QKV_WIDTH].astype(BF16)
    wf = jnp.pad(w_in[:, QKV_WIDTH:QKV_WIDTH + FOX_HEADS], ((0, 0), (0, LANES - FOX_HEADS))).astype(BF16)
    wg = w_in[:, QKV_WIDTH + FOX_HEADS:].astype(BF16)
    bf = jnp.pad(b_forget, (0, LANES - FOX_HEADS)).reshape(1, LANES)
    bg = b_gate.reshape(1, 2 * D_MODEL)
    scale = HEAD_DIM ** -0.5
    nrm = jnp.stack([jnp.tile(q_norm_a, 4) * scale, jnp.tile(k_norm_a, 4),
                     jnp.tile(q_norm_b, 4) * (scale * LOG2E), jnp.tile(k_norm_b, 4)])
    head_id = np.arange(256) // HEAD_DIM
    ones_bd = jnp.asarray(head_id[:, None] == head_id[None, :], BF16)
    src = np.arange(256)
    place = np.zeros((256, 4 * LANES), np.float32)
    place[src, (src // HEAD_DIM) * LANES + src % HEAD_DIM] = 1.0
    pat = np.zeros((2, 4 * LANES), np.float32)
    for slot in range(4):
        pat[0, [slot * LANES + l for l in FOX_Q_ONES]] = 1.0
        pat[1, [slot * LANES + l for l in FOX_K_ONES]] = 1.0
    const = lambda shape: pl.BlockSpec(shape, lambda i: (0,) * len(shape), pipeline_mode=pl.Buffered(1))
    row = lambda w: pl.BlockSpec((TM_PROJ, w), lambda i: (i, 0))
    seq_tile = lambda i: (i // tiles_per_seq, i % tiles_per_seq)
    group_spec = lambda d: pl.BlockSpec((1, TM_PROJ // d, d * DSWA_WIDTH), lambda i: (*seq_tile(i), 0))
    group_shape = lambda d: jax.ShapeDtypeStruct((batch, seq // d, d * DSWA_WIDTH), BF16)
    vt_spec = pl.BlockSpec((1, FOX_WIDTH, TM_PROJ), lambda i: (seq_tile(i)[0], 0, seq_tile(i)[1]))
    dils = [d for _, d in DSWA_PATTERNS]
    return pl.pallas_call(
        _inproj_kernel,
        grid=(n // TM_PROJ,),
        in_specs=[row(D_MODEL), const((1, D_MODEL)), const((D_MODEL, QKV_WIDTH)), const((D_MODEL, LANES)),
                  const((D_MODEL, 2 * D_MODEL)), const((1, LANES)), const((1, 2 * D_MODEL)),
                  const((4, 256)), const((256, 256)), const((256, 4 * LANES)), const((2, 4 * LANES))],
        out_specs=[group_spec(d) for d in dils] + [row(FOX_HEADS * LANES), row(FOX_HEADS * LANES), vt_spec,
                                                   row(LANES), row(2 * D_MODEL)],
        out_shape=[group_shape(d) for d in dils] + [jax.ShapeDtypeStruct((n, FOX_HEADS * LANES), BF16),
                                                    jax.ShapeDtypeStruct((n, FOX_HEADS * LANES), BF16),
                                                    jax.ShapeDtypeStruct((batch, FOX_WIDTH, seq), BF16),
                                                    jax.ShapeDtypeStruct((n, LANES), F32),
                                                    jax.ShapeDtypeStruct((n, 2 * D_MODEL), F32)],
        scratch_shapes=[pltpu.VMEM((2, TM_PROJ, LANES), F32)],
        compiler_params=pltpu.CompilerParams(dimension_semantics=("arbitrary",), vmem_limit_bytes=VMEM_LIMIT),
        name="in_projection",
    )(x2, ln1_g.reshape(1, D_MODEL), wq, wf, wg, bf, bg, nrm, ones_bd, jnp.asarray(place, BF16), jnp.asarray(pat))


def _t5_bucket(n):
    max_exact = NUM_BUCKETS // 2
    n_safe = np.maximum(n, 1).astype(np.float64)
    large = max_exact + (np.log(n_safe / max_exact) / np.log(MAX_DISTANCE / max_exact)
                         * (NUM_BUCKETS - max_exact)).astype(np.int64)
    large = np.minimum(large, NUM_BUCKETS - 1)
    return np.where(n < max_exact, n, large).astype(np.int32)


def _dswa_kernel(q_ref, kp_ref, kc_ref, vp_ref, vc_ref, bias_ref, o_ref, l_ref, *, blocks):
    first = pl.program_id(2) == 0
    items = [(j, h) for j in range(blocks) for h in range(DSWA_HG)]
    rows = lambda j: slice(j * ATTN_BLOCK, (j + 1) * ATTN_BLOCK)
    head = lambda h: slice(h * HEAD_DIM, (h + 1) * HEAD_DIM)

    sps, scs = [], []
    for j, h in items:
        q = q_ref[0, rows(j), head(h)]
        k_prev = kp_ref[0, :, head(h)] if j == 0 else kc_ref[0, rows(j - 1), head(h)]
        sp = _dot_nt(q, k_prev) + bias_ref[h, :, :ATTN_BLOCK]
        if j == 0:
            sp = jnp.where(first, NEG_INF, sp)
        sps.append(sp)
        scs.append(_dot_nt(q, kc_ref[0, rows(j), head(h)]) + bias_ref[h, :, ATTN_BLOCK:])
    ms = [jnp.maximum(jnp.max(sp, axis=-1, keepdims=True), jnp.max(sc, axis=-1, keepdims=True))
          for sp, sc in zip(sps, scs)]
    pps = [jnp.exp(sp - m) for sp, m in zip(sps, ms)]
    pcs = [jnp.exp(sc - m) for sc, m in zip(scs, ms)]
    dens = [jnp.sum(pp, axis=-1, keepdims=True) + jnp.sum(pc, axis=-1, keepdims=True) for pp, pc in zip(pps, pcs)]
    for (j, h), pp, pc, m, den in zip(items, pps, pcs, ms, dens):
        v_prev = vp_ref[0, :, head(h)] if j == 0 else vc_ref[0, rows(j - 1), head(h)]
        o = _dot(pp.astype(BF16), v_prev) + _dot(pc.astype(BF16), vc_ref[0, rows(j), head(h)])
        o_ref[0, rows(j), head(h)] = o / den
        l_ref[0, rows(j), head(h)] = jnp.broadcast_to(m + jnp.log(den), (ATTN_BLOCK, HEAD_DIM))


def _dilated_attention(qkv3, rel_bias, g, batch, seq):
    window, dil = DSWA_PATTERNS[g]
    steps = window // dil
    assert steps == ATTN_BLOCK
    length = seq // dil
    assert length % ATTN_BLOCK == 0
    nblk = length // ATTN_BLOCK
    span = 2 * ATTN_BLOCK
    back = np.arange(ATTN_BLOCK)[:, None] + steps - np.arange(span)[None, :]
    valid = (back >= 0) & (back <= steps)
    bucket = _t5_bucket(np.clip(back, 0, steps) * dil)
    cols = rel_bias[:, g * DSWA_HG:(g + 1) * DSWA_HG]
    onehot = np.asarray(bucket.reshape(-1)[:, None] == np.arange(NUM_BUCKETS)[None, :], np.float32)
    bias = jnp.dot(jnp.asarray(onehot), cols.astype(F32), precision=lax.Precision.HIGHEST)
    bias = jnp.transpose(bias.reshape(ATTN_BLOCK, span, DSWA_HG), (2, 0, 1))
    bias = jnp.where(valid[None], bias, NEG_INF)

    per_pos = DSWA_WIDTH // 256
    nb = math.gcd(nblk, DSWA_BLOCKS_PER_STEP)
    blk = (1, nb * ATTN_BLOCK, 256)
    one = (1, ATTN_BLOCK, 256)
    before = lambda n: jnp.maximum(n * nb - 1, 0)
    q_spec = pl.BlockSpec(blk, lambda b, r, n: (b, n, r * per_pos))
    kc_spec = pl.BlockSpec(blk, lambda b, r, n: (b, n, r * per_pos + 1))
    kp_spec = pl.BlockSpec(one, lambda b, r, n: (b, before(n), r * per_pos + 1))
    vc_spec = pl.BlockSpec(blk, lambda b, r, n: (b, n, r * per_pos + 2))
    vp_spec = pl.BlockSpec(one, lambda b, r, n: (b, before(n), r * per_pos + 2))
    bias_spec = pl.BlockSpec((DSWA_HG, ATTN_BLOCK, span), lambda b, r, n: (0, 0, 0))
    o_spec = pl.BlockSpec(blk, lambda b, r, n: (b, n, r))
    return pl.pallas_call(
        functools.partial(_dswa_kernel, blocks=nb),
        grid=(batch, dil, nblk // nb),
        in_specs=[q_spec, kp_spec, kc_spec, vp_spec, vc_spec, bias_spec],
        out_specs=[o_spec, o_spec],
        out_shape=[jax.ShapeDtypeStruct((batch, length, dil * DSWA_OUT), F32)] * 2,
        compiler_params=pltpu.CompilerParams(dimension_semantics=("arbitrary",) * 3),
        name=f"dilated_attention_{g}",
    )(qkv3, qkv3, qkv3, qkv3, qkv3, bias)


def _fox_kernel(q_ref, k_ref, lf_ref, vt_ref, put_ref, o_ref, kaug_s, carry_s):
    qi = pl.program_id(1)
    tq = TQ_FOX
    krow = lax.broadcasted_iota(jnp.int32, (tq, tq), 0)
    qcol = lax.broadcasted_iota(jnp.int32, (tq, tq), 1)
    causal = krow <= qcol

    @pl.when(qi == 0)
    def _():
        carry_s[...] = jnp.zeros_like(carry_s)

    tri = jnp.where(krow >= qcol, 1.0, 0.0).astype(BF16)
    c = carry_s[...]
    for term in _split3_bf16(lf_ref[...]):
        c = c + _dot(tri, term)
    carry_s[...] = c[tq - 1:tq, :]
    terms = _split3_bf16(c)
    q_bias = _dot(terms[0], put_ref[0]) + _dot(terms[1], put_ref[1]) + _dot(terms[2], put_ref[2])
    k_bias = _dot(terms[0], put_ref[3]) + _dot(terms[1], put_ref[4]) + _dot(terms[2], put_ref[5])
    q_all = q_ref[...] + q_bias.astype(BF16)
    kaug_s[pl.ds(pl.multiple_of(qi * tq, tq), tq), :] = k_ref[...] + k_bias.astype(BF16)

    for hg in range(FOX_HEADS // FOX_HEADS_PER_LOOP):
        heads = [hg * FOX_HEADS_PER_LOOP + i for i in range(FOX_HEADS_PER_LOOP)]
        qs = [q_all[:, h * LANES:(h + 1) * LANES] for h in heads]

        def step(kb, carry, masked):
            ks = pl.multiple_of(kb * tq, tq)
            idx = range(len(heads))
            sts = []
            for i, h in enumerate(heads):
                st = _dot_nt(kaug_s[pl.ds(ks, tq), h * LANES:(h + 1) * LANES], qs[i])
                sts.append(jnp.where(causal, st, NEG_INF) if masked else st)
            m_new = [jnp.maximum(carry[i][0], jnp.max(sts[i], axis=0, keepdims=True)) for i in idx]
            alpha = [jnp.exp2(carry[i][0] - m_new[i]) for i in idx]
            ps = [jnp.exp2(sts[i] - m_new[i]) for i in idx]
            ls = [alpha[i] * carry[i][1] + jnp.sum(ps[i], axis=0, keepdims=True) for i in idx]
            out = []
            for i, h in enumerate(heads):
                vt = vt_ref[0, h * HEAD_DIM:(h + 1) * HEAD_DIM, pl.ds(ks, tq)]
                out.append((m_new[i], ls[i], alpha[i] * carry[i][2] + _dot(vt, ps[i].astype(BF16))))
            return tuple(out)

        init = tuple((jnp.full((1, tq), NEG_INF, F32), jnp.zeros((1, tq), F32), jnp.zeros((HEAD_DIM, tq), F32))
                     for _ in heads)
        carry = lax.fori_loop(0, qi, lambda kb, c: step(kb, c, False), init)
        carry = step(qi, carry, True)
        for i, h in enumerate(heads):
            _, l, acc = carry[i]
            o_ref[0, h * HEAD_DIM:(h + 1) * HEAD_DIM, :] = (acc / l).astype(BF16)


def _top16(v):
    bits = lax.bitcast_convert_type(v, jnp.uint32) & jnp.uint32(0xFFFF0000)
    return lax.bitcast_convert_type(bits, F32)


def _split3_bf16(c):
    hi = _top16(c)
    r1 = c - hi
    mid = _top16(r1)
    lo = r1 - mid
    return hi.astype(BF16), mid.astype(BF16), lo.astype(BF16)


def _forgetting_attention(qb, kb, lf, vt, batch, seq):
    nq = seq // TQ_FOX
    put = np.zeros((6, LANES, FOX_HEADS * LANES), np.float32)
    for h in range(FOX_HEADS):
        for j in range(3):
            put[j, h, h * LANES + FOX_Q_C[j]] = 1.0
            put[3 + j, h, h * LANES + FOX_K_C[j]] = -1.0
    tile = lambda w: pl.BlockSpec((TQ_FOX, w), lambda b, i: (b * nq + i, 0))
    return pl.pallas_call(
        _fox_kernel,
        grid=(batch, nq),
        in_specs=[tile(FOX_HEADS * LANES), tile(FOX_HEADS * LANES), tile(LANES),
                  pl.BlockSpec((1, FOX_WIDTH, seq), lambda b, i: (b, 0, 0)),
                  pl.BlockSpec(put.shape, lambda b, i: (0, 0, 0), pipeline_mode=pl.Buffered(1))],
        out_specs=pl.BlockSpec((1, FOX_WIDTH, TQ_FOX), lambda b, i: (b, 0, i)),
        out_shape=jax.ShapeDtypeStruct((batch, FOX_WIDTH, seq), BF16),
        scratch_shapes=[pltpu.VMEM((seq, FOX_HEADS * LANES), BF16), pltpu.VMEM((1, LANES), F32)],
        compiler_params=pltpu.CompilerParams(dimension_semantics=("arbitrary", "arbitrary"),
                                             vmem_limit_bytes=VMEM_LIMIT),
        name="forgetting_attention",
    )(qb, kb, lf, vt, jnp.asarray(put, BF16))


def _merge_kernel(x_ref, oa0_ref, oa1_ref, oa2_ref, ls0_ref, ls1_ref, ls2_ref, obt_ref, gate_ref,
                  wa_ref, wb_ref, wo_ref, g2_ref, wrh_ref, wrl_ref, br_ref,
                  x1_ref, h2_ref, ri_ref, rg_ref, cnt_ref, run_s, stage_s):
    tm = TM_MERGE

    @pl.when(pl.program_id(0) == 0)
    def _():
        run_s[...] = jnp.zeros_like(run_s)

    def token_order(ref, g, stage):
        dil = DSWA_PATTERNS[g][1]
        if dil == 1:
            return ref[0]
        for r in range(dil):
            for half in range(2):
                col = r * DSWA_OUT + half * LANES
                stage[half, pl.ds(r, tm // dil, stride=dil), :] = ref[0, :, col:col + LANES]
        return jnp.concatenate([stage[0], stage[1]], axis=1)

    l0, l1, l2 = (token_order(ref, g, stage_s.at[g]) for g, ref in enumerate((ls0_ref, ls1_ref, ls2_ref)))
    lm = jnp.maximum(jnp.maximum(l0, l1), l2)
    e0, e1, e2 = jnp.exp(l0 - lm), jnp.exp(l1 - lm), jnp.exp(l2 - lm)
    o0, o1, o2 = (token_order(ref, g, stage_s.at[DSWA_GROUPS + g])
                  for g, ref in enumerate((oa0_ref, oa1_ref, oa2_ref)))
    o_a = (e0 * o0 + e1 * o1 + e2 * o2) / (e0 + e1 + e2)
    ya = _dot(o_a.astype(BF16), wa_ref[...])
    yb = lax.dot_general(obt_ref[0], wb_ref[...], (((0,), (0,)), ((), ())), preferred_element_type=F32)
    merged = gate_ref[:, :D_MODEL] * ya + gate_ref[:, D_MODEL:] * yb
    x1 = x_ref[...] + _dot(merged.astype(BF16), wo_ref[...])
    x1_ref[...] = x1

    ms = jnp.mean(x1 * x1, axis=-1, keepdims=True)
    h2 = x1 * lax.rsqrt(ms + RMS_EPS) * g2_ref[...]
    for c in range(ROW_CHUNKS):
        h2_ref[pl.ds(c, tm, stride=ROW_CHUNKS), :] = h2[:, c * LANES:(c + 1) * LANES]

    hi = h2.astype(BF16)
    lo = (h2 - hi.astype(F32)).astype(BF16)
    logits = _dot(hi, wrh_ref[...]) + _dot(lo, wrh_ref[...]) + _dot(hi, wrl_ref[...]) + br_ref[...]

    lane = lax.broadcasted_iota(jnp.int32, (tm, LANES), 1).astype(F32)
    work = logits
    top_v, top_i = [], []
    for _ in range(TOP_K):
        mk = jnp.max(work, axis=-1, keepdims=True)
        ik = jnp.min(jnp.where(work == mk, lane, float(LANES)), axis=-1, keepdims=True)
        top_v.append(mk)
        top_i.append(ik)
        work = jnp.where(lane == ik, -jnp.inf, work)
    ex = [jnp.exp(v - top_v[0]) for v in top_v]
    den = ex[0] + ex[1] + ex[2] + ex[3]

    onehot = jnp.zeros((tm, LANES), F32)
    for ik in top_i:
        onehot = onehot + jnp.where(lane == ik, 1.0, 0.0)
    r_i = lax.broadcasted_iota(jnp.int32, (tm, tm), 0)
    c_i = lax.broadcasted_iota(jnp.int32, (tm, tm), 1)
    tri = jnp.where(c_i < r_i, 1.0, 0.0).astype(BF16)
    before = _dot(tri, onehot.astype(BF16)) + run_s[...]
    ri = jnp.zeros((tm, LANES), jnp.int32)
    rg = jnp.zeros((tm, LANES), F32)
    for k in range(TOP_K):
        rank = jnp.sum(jnp.where(lane == top_i[k], before, 0.0), axis=-1, keepdims=True)
        ri = jnp.where(lane == k, top_i[k].astype(jnp.int32), ri)
        ri = jnp.where(lane == TOP_K + k, rank.astype(jnp.int32), ri)
        rg = jnp.where(lane == k, ex[k] / den, rg)
    ri_ref[...] = ri
    rg_ref[...] = rg
    run = run_s[...] + jnp.sum(onehot, axis=0, keepdims=True)
    run_s[...] = run
    cnt_ref[...] = run


def _merge_and_route(x2, batch, seq, oas, lss, obt, gate, w_branch_a, w_branch_b, w_out, ln2_g, w_router,
                     b_router):
    n = x2.shape[0]
    assert seq % TM_MERGE == 0 and all(TM_MERGE % (SUBLANES * d) == 0 for _, d in DSWA_PATTERNS)
    tiles_per_seq = seq // TM_MERGE
    seq_tile = lambda i: (i // tiles_per_seq, i % tiles_per_seq)
    group_specs = [pl.BlockSpec((1, TM_MERGE // d, d * DSWA_OUT), lambda i: (*seq_tile(i), 0))
                   for _, d in DSWA_PATTERNS]
    obt_spec = pl.BlockSpec((1, FOX_WIDTH, TM_MERGE), lambda i: (seq_tile(i)[0], 0, seq_tile(i)[1]))
    wr = jnp.pad(w_router, ((0, 0), (0, LANES - N_EXPERTS)))
    wr_top = _top16(wr)
    wr_hi = wr_top.astype(BF16)
    wr_lo = (wr - wr_top).astype(BF16)
    br = jnp.concatenate([b_router, jnp.full((LANES - N_EXPERTS,), NEG_INF, F32)]).reshape(1, LANES)
    const = lambda shape: pl.BlockSpec(shape, lambda i: (0,) * len(shape), pipeline_mode=pl.Buffered(1))
    row = lambda w: pl.BlockSpec((TM_MERGE, w), lambda i: (i, 0))
    return pl.pallas_call(
        _merge_kernel,
        grid=(n // TM_MERGE,),
        in_specs=[row(D_MODEL)] + group_specs + group_specs + [obt_spec, row(2 * D_MODEL),
                  const((DSWA_OUT, D_MODEL)), const((FOX_WIDTH, D_MODEL)), const((D_MODEL, D_MODEL)),
                  const((1, D_MODEL)), const((D_MODEL, LANES)), const((D_MODEL, LANES)), const((1, LANES))],
        out_specs=[row(D_MODEL), pl.BlockSpec((TM_MERGE * ROW_CHUNKS, LANES), lambda i: (i, 0)),
                   row(LANES), row(LANES), pl.BlockSpec((1, LANES), lambda i: (0, 0))],
        out_shape=[jax.ShapeDtypeStruct((n, D_MODEL), F32),
                   jax.ShapeDtypeStruct((n * ROW_CHUNKS, LANES), F32),
                   jax.ShapeDtypeStruct((n, LANES), jnp.int32),
                   jax.ShapeDtypeStruct((n, LANES), F32),
                   jax.ShapeDtypeStruct((1, LANES), F32)],
        scratch_shapes=[pltpu.VMEM((1, LANES), F32), pltpu.VMEM((2 * DSWA_GROUPS, 2, TM_MERGE, LANES), F32)],
        compiler_params=pltpu.CompilerParams(dimension_semantics=("arbitrary",), vmem_limit_bytes=VMEM_LIMIT),
        name="merge_and_route",
    )(x2, *oas, *lss, obt, gate, w_branch_a.astype(BF16), w_branch_b.astype(BF16), w_out.astype(BF16),
      ln2_g.reshape(1, D_MODEL), wr_hi, wr_lo, br)


def _row_copy(src, src_row, dst, dst_row, sem):
    return pltpu.make_async_copy(src.at[pl.ds(pl.multiple_of(src_row * ROW_CHUNKS, ROW_CHUNKS), ROW_CHUNKS)],
                                 dst.at[pl.ds(pl.multiple_of(dst_row * ROW_CHUNKS, ROW_CHUNKS), ROW_CHUNKS)], sem)


def _dispatch_kernel(zb_ref, zf_ref, dest_hbm, h2_ref, xs_hbm, idx_s, zeros_v, sem_i, sem_z, sem_r):
    i = pl.program_id(0)
    n_move = TOP_K * TC_MOVE
    idx_copy = pltpu.make_async_copy(dest_hbm.at[i], idx_s, sem_i)
    idx_copy.start()

    def zero_copy(e):
        blk = pl.multiple_of(zb_ref[e] * (T_EXPERT * ROW_CHUNKS), T_EXPERT * ROW_CHUNKS)
        return pltpu.make_async_copy(zeros_v, xs_hbm.at[pl.ds(blk, T_EXPERT * ROW_CHUNKS)], sem_z)

    @pl.when(i == 0)
    def _():
        zeros_v[...] = jnp.zeros_like(zeros_v)
        for e in range(N_EXPERTS):
            @pl.when(zf_ref[e] > 0)
            def _():
                zero_copy(e).start()
        for e in range(N_EXPERTS):
            @pl.when(zf_ref[e] > 0)
            def _():
                zero_copy(e).wait()

    idx_copy.wait()

    def issue(jo, carry):
        for u in range(DMA_UNROLL):
            j = jo * DMA_UNROLL + u
            _row_copy(h2_ref, j & (TC_MOVE - 1), xs_hbm, idx_s[j], sem_r).start(priority=u % 2)
        return carry

    lax.fori_loop(0, n_move // DMA_UNROLL, issue, 0)

    def drain(jo, carry):
        for u in range(DMA_UNROLL):
            _row_copy(h2_ref, 0, xs_hbm, 0, sem_r).wait()
        return carry

    lax.fori_loop(0, n_move // DMA_UNROLL, drain, 0)


def _dispatch(dest_tiles, zero_blk, zero_flag, h2_rows, n_slots):
    nt = dest_tiles.shape[0]
    return pl.pallas_call(
        _dispatch_kernel,
        grid_spec=pltpu.PrefetchScalarGridSpec(
            num_scalar_prefetch=2, grid=(nt,),
            in_specs=[pl.BlockSpec(memory_space=pl.ANY),
                      pl.BlockSpec((TC_MOVE * ROW_CHUNKS, LANES), lambda i, zb, zf: (i, 0))],
            out_specs=pl.BlockSpec(memory_space=pl.ANY),
            scratch_shapes=[pltpu.SMEM((TOP_K * TC_MOVE,), jnp.int32),
                            pltpu.VMEM((T_EXPERT * ROW_CHUNKS, LANES), F32),
                            pltpu.SemaphoreType.DMA, pltpu.SemaphoreType.DMA, pltpu.SemaphoreType.DMA]),
        out_shape=jax.ShapeDtypeStruct((n_slots * ROW_CHUNKS, LANES), F32),
        compiler_params=pltpu.CompilerParams(dimension_semantics=("arbitrary",)),
        name="dispatch_rows",
    )(zero_blk, zero_flag, dest_tiles, h2_rows)


def _expert_kernel(be_ref, nu_ref, xs_ref, wgu_ref, bgu_ref, wd_ref, bd_ref, ys_ref, wgu_s, wd_s):
    j = pl.program_id(0)
    t = T_EXPERT

    @pl.when(j < nu_ref[0])
    def _():
        prev = be_ref[jnp.maximum(j - 1, 0)]

        @pl.when((j == 0) | (be_ref[j] != prev))
        def _():
            wgu_s[...] = wgu_ref[0].astype(BF16)
            wd_s[...] = wd_ref[0].astype(BF16)

        x = jnp.concatenate([xs_ref[pl.ds(c, t, stride=ROW_CHUNKS), :] for c in range(ROW_CHUNKS)], axis=1)
        gu = _dot(x.astype(BF16), wgu_s[...]) + bgu_ref[0]
        g = jnp.minimum(gu[:, :D_MODEL], SWIGLU_LIMIT)
        u = jnp.clip(gu[:, D_MODEL:], -SWIGLU_LIMIT, SWIGLU_LIMIT)
        mid = (u + 1.0) * (g * jax.nn.sigmoid(SWIGLU_ALPHA * g))
        y = _dot(mid.astype(BF16), wd_s[...]) + bd_ref[0]
        for c in range(ROW_CHUNKS):
            ys_ref[pl.ds(c, t, stride=ROW_CHUNKS), :] = y[:, c * LANES:(c + 1) * LANES]


def _expert_ffn(blk_e, n_used, xs_rows, w_gate_up, b_gate_up, w_down, b_down):
    n_blk = blk_e.shape[0]
    rows = T_EXPERT * ROW_CHUNKS
    live = lambda j, be, nu: jnp.minimum(j, nu[0] - 1)
    sink = lambda j, be, nu: jnp.where(j < nu[0], j, n_blk - 1)
    return pl.pallas_call(
        _expert_kernel,
        grid_spec=pltpu.PrefetchScalarGridSpec(
            num_scalar_prefetch=2, grid=(n_blk,),
            in_specs=[pl.BlockSpec((rows, LANES), lambda j, be, nu: (live(j, be, nu), 0)),
                      pl.BlockSpec((1, D_MODEL, 2 * D_MODEL), lambda j, be, nu: (be[live(j, be, nu)], 0, 0)),
                      pl.BlockSpec((1, 1, 2 * D_MODEL), lambda j, be, nu: (be[live(j, be, nu)], 0, 0)),
                      pl.BlockSpec((1, D_MODEL, D_MODEL), lambda j, be, nu: (be[live(j, be, nu)], 0, 0)),
                      pl.BlockSpec((1, 1, D_MODEL), lambda j, be, nu: (be[live(j, be, nu)], 0, 0))],
            out_specs=pl.BlockSpec((rows, LANES), lambda j, be, nu: (sink(j, be, nu), 0)),
            scratch_shapes=[pltpu.VMEM((D_MODEL, 2 * D_MODEL), BF16), pltpu.VMEM((D_MODEL, D_MODEL), BF16)]),
        out_shape=jax.ShapeDtypeStruct(xs_rows.shape, F32),
        compiler_params=pltpu.CompilerParams(dimension_semantics=("arbitrary",), vmem_limit_bytes=VMEM_LIMIT),
        name="expert_ffn",
    )(blk_e, n_used, xs_rows, w_gate_up, b_gate_up.reshape(N_EXPERTS, 1, 2 * D_MODEL),
      w_down, b_down.reshape(N_EXPERTS, 1, D_MODEL))


def _combine_kernel(dest_hbm, ys_hbm, x1_ref, rg_ref, o_ref, idx_s, buf, sem_i, sem_r):
    i = pl.program_id(0)
    n_move = TOP_K * TC_MOVE
    idx_copy = pltpu.make_async_copy(dest_hbm.at[i], idx_s, sem_i)
    idx_copy.start()
    idx_copy.wait()

    def issue(jo, carry):
        for u in range(DMA_UNROLL):
            j = jo * DMA_UNROLL + u
            _row_copy(ys_hbm, idx_s[j], buf, j, sem_r).start(priority=u % 2)
        return carry

    lax.fori_loop(0, n_move // DMA_UNROLL, issue, 0)

    def drain(jo, carry):
        for u in range(DMA_UNROLL):
            _row_copy(ys_hbm, 0, buf, 0, sem_r).wait()
        return carry

    lax.fori_loop(0, n_move // DMA_UNROLL, drain, 0)

    for c in range(ROW_CHUNKS):
        acc = x1_ref[:, c * LANES:(c + 1) * LANES]
        for k in range(TOP_K):
            rows = buf[pl.ds(k * TC_MOVE * ROW_CHUNKS + c, TC_MOVE, stride=ROW_CHUNKS), :]
            acc = acc + rg_ref[:, k:k + 1] * rows
        o_ref[:, c * LANES:(c + 1) * LANES] = acc


def _combine(dest_tiles, ys_rows, x1, rg):
    n = x1.shape[0]
    nt = dest_tiles.shape[0]
    return pl.pallas_call(
        _combine_kernel,
        grid=(nt,),
        in_specs=[pl.BlockSpec(memory_space=pl.ANY), pl.BlockSpec(memory_space=pl.ANY),
                  pl.BlockSpec((TC_MOVE, D_MODEL), lambda i: (i, 0)),
                  pl.BlockSpec((TC_MOVE, LANES), lambda i: (i, 0))],
        out_specs=pl.BlockSpec((TC_MOVE, D_MODEL), lambda i: (i, 0)),
        out_shape=jax.ShapeDtypeStruct((n, D_MODEL), F32),
        scratch_shapes=[pltpu.SMEM((TOP_K * TC_MOVE,), jnp.int32),
                        pltpu.VMEM((TOP_K * TC_MOVE * ROW_CHUNKS, LANES), F32),
                        pltpu.SemaphoreType.DMA, pltpu.SemaphoreType.DMA],
        compiler_params=pltpu.CompilerParams(dimension_semantics=("arbitrary",), vmem_limit_bytes=VMEM_LIMIT),
        name="combine_rows",
    )(dest_tiles, ys_rows, x1, rg)


def _layer(x, ln1_g, w_in, b_forget, b_gate, q_norm_a, k_norm_a, q_norm_b, k_norm_b, rel_bias,
           w_branch_a, w_branch_b, w_out, ln2_g, w_router, b_router, w_gate_up, b_gate_up, w_down, b_down):
    batch, seq, _ = x.shape
    n = batch * seq
    x2 = x.reshape(n, D_MODEL)
    *qkv_groups, qb, kb, vt, lf, gate = _in_projection(x2, batch, seq, ln1_g, w_in, b_forget, b_gate,
                                                       q_norm_a, k_norm_a, q_norm_b, k_norm_b)

    oas, lss = [], []
    for g in range(DSWA_GROUPS):
        o, lse = _dilated_attention(qkv_groups[g], rel_bias, g, batch, seq)
        oas.append(o)
        lss.append(lse)

    obt = _forgetting_attention(qb, kb, lf, vt, batch, seq)

    x1, h2_rows, ri, rg, cnt = _merge_and_route(x2, batch, seq, oas, lss, obt, gate, w_branch_a, w_branch_b,
                                                w_out, ln2_g, w_router, b_router)

    counts = cnt[0, :N_EXPERTS].astype(jnp.int32)
    pcounts = ((counts + T_EXPERT - 1) // T_EXPERT) * T_EXPERT
    pends = jnp.cumsum(pcounts)
    pstarts = pends - pcounts
    idx, rank = ri[:, :TOP_K], ri[:, TOP_K:2 * TOP_K]
    start_of = jnp.sum(jnp.where(idx[..., None] == jnp.arange(N_EXPERTS), pstarts, 0), axis=-1)
    dest = start_of + rank
    nt = n // TC_MOVE
    dest_tiles = jnp.transpose(dest.reshape(nt, TC_MOVE, TOP_K), (0, 2, 1)).reshape(nt, TOP_K * TC_MOVE)
    n_slots = n * TOP_K + N_EXPERTS * T_EXPERT
    n_blk = n_slots // T_EXPERT
    blk_first = jnp.arange(n_blk, dtype=jnp.int32) * T_EXPERT
    blk_e = jnp.minimum(jnp.sum((pends[None, :] <= blk_first[:, None]).astype(jnp.int32), axis=1), N_EXPERTS - 1)
    n_used = (pends[-1:] // T_EXPERT).astype(jnp.int32)
    zero_blk = jnp.maximum(pends // T_EXPERT - 1, 0).astype(jnp.int32)
    zero_flag = (pcounts > 0).astype(jnp.int32)

    xs_rows = _dispatch(dest_tiles, zero_blk, zero_flag, h2_rows, n_slots)
    ys_rows = _expert_ffn(blk_e, n_used, xs_rows, w_gate_up, b_gate_up, w_down, b_down)
    out = _combine(dest_tiles, ys_rows, x1, rg)
    return out.reshape(batch, seq, D_MODEL)


def kernel(x, ln1_g, w_in, b_forget, b_gate, q_norm_a, k_norm_a, q_norm_b, k_norm_b, rel_bias, w_branch_a, w_branch_b, w_out, ln2_g, w_router, b_router, w_gate_up, b_gate_up, w_down, b_down):
    for layer in range(ln1_g.shape[0]):
        x = _layer(x, ln1_g[layer], w_in[layer], b_forget[layer], b_gate[layer], q_norm_a[layer],
                   k_norm_a[layer], q_norm_b[layer], k_norm_b[layer], rel_bias, w_branch_a[layer],
                   w_branch_b[layer], w_out[layer], ln2_g[layer], w_router[layer], b_router[layer],
                   w_gate_up[layer], b_gate_up[layer], w_down[layer], b_down[layer])
    return x
```

```python
import functools
import math

import numpy as np
import jax
import jax.numpy as jnp
from jax import lax
from jax.experimental import pallas as pl
from jax.experimental.pallas import tpu as pltpu

F32 = jnp.float32
BF16 = jnp.bfloat16

D_MODEL = 1024
HEAD_DIM = 64
DSWA_PATTERNS = ((128, 1), (512, 4), (2048, 16))
DSWA_HG = 4
DSWA_GROUPS = len(DSWA_PATTERNS)
DSWA_WIDTH = DSWA_GROUPS * DSWA_HG * HEAD_DIM
DSWA_OUT = DSWA_HG * HEAD_DIM
FOX_HEADS = 8
FOX_WIDTH = FOX_HEADS * HEAD_DIM
ATTN_BLOCK = 128
NUM_BUCKETS = 32
MAX_DISTANCE = 2048
N_EXPERTS = 32
TOP_K = 4
SWIGLU_LIMIT = 7.0
SWIGLU_ALPHA = 1.702
RMS_EPS = 1e-6
NEG_INF = -1e30
LOG2E = 1.4426950408889634

LANES = 128
SUBLANES = 8
ROW_CHUNKS = D_MODEL // LANES
QKV_WIDTH = 3 * DSWA_WIDTH + 3 * FOX_WIDTH
FOX_Q_ONES = (64, 65, 66)
FOX_K_C = (64, 65, 66)
FOX_Q_C = (67, 68, 69)
FOX_K_ONES = (67, 68, 69)
VMEM_LIMIT = 56 * 1024 * 1024

TM_PROJ = 512
TQ_FOX = 512
FOX_HEADS_PER_LOOP = 4
DSWA_BLOCKS_PER_STEP = 4
TM_MERGE = 512
T_EXPERT = 512
TC_MOVE = 1024
DMA_UNROLL = 16


def _dot(a, b):
    return jnp.dot(a, b, preferred_element_type=F32)


def _dot_nt(a, b):
    return lax.dot_general(a, b, (((1,), (1,)), ((), ())), preferred_element_type=F32)


def _inproj_kernel(x_ref, g1_ref, wq_ref, wf_ref, wg_ref, bf_ref, bg_ref, nrm_ref, e_ref, place_ref, pat_ref,
                   qkv0_ref, qkv1_ref, qkv2_ref, qb_ref, kb_ref, vt_ref, lf_ref, gate_ref, stage_ref):
    group_refs = (qkv0_ref, qkv1_ref, qkv2_ref)
    x = x_ref[...]
    ms = jnp.mean(x * x, axis=-1, keepdims=True)
    h = (x * lax.rsqrt(ms + RMS_EPS) * g1_ref[...]).astype(BF16)
    ones_bd = e_ref[...]

    def head_norm(y, gn):
        sq = y * y
        hi = sq.astype(BF16)
        lo = (sq - hi.astype(F32)).astype(BF16)
        ss = _dot(hi, ones_bd) + _dot(lo, ones_bd)
        return y * lax.rsqrt(ss * (1.0 / HEAD_DIM) + RMS_EPS) * gn

    n_a = 3 * DSWA_WIDTH // 256
    for c in range(QKV_WIDTH // 256):
        y = _dot(h, wq_ref[:, c * 256:(c + 1) * 256])
        if c < 3:
            y = head_norm(y, nrm_ref[0:1, :])
        elif c < 6:
            y = head_norm(y, nrm_ref[1:2, :])
        elif 9 <= c < 11:
            y = head_norm(y, nrm_ref[2:3, :])
        elif 11 <= c < 13:
            y = head_norm(y, nrm_ref[3:4, :])
        if c < n_a:
            g, part = c % DSWA_GROUPS, c // DSWA_GROUPS
            dil = DSWA_PATTERNS[g][1]
            if dil == 1:
                group_refs[g][0, :, part * 256:(part + 1) * 256] = y.astype(BF16)
            else:
                for half in range(2):
                    stage_ref[half] = y[:, half * LANES:(half + 1) * LANES]
                for r in range(dil):
                    for half in range(2):
                        col = r * DSWA_WIDTH + part * 256 + half * LANES
                        rows = stage_ref[half, pl.ds(r, TM_PROJ // dil, stride=dil), :]
                        group_refs[g][0, :, col:col + LANES] = rows.astype(BF16)
        elif c < n_a + 4:
            is_k, half = divmod(c - n_a, 2)
            slots = _dot(y.astype(BF16), place_ref[...]) + pat_ref[is_k:is_k + 1, :]
            (kb_ref if is_k else qb_ref)[:, half * 512:(half + 1) * 512] = slots.astype(BF16)
        else:
            half = c - n_a - 4
            vt_ref[0, half * 256:(half + 1) * 256, :] = y.T.astype(BF16)

    z = _dot(h, wf_ref[...]) + bf_ref[...]
    lf_ref[...] = (jnp.minimum(z, 0.0) - jnp.log1p(jnp.exp(-jnp.abs(z)))) * LOG2E
    for c in range(2 * D_MODEL // 256):
        gl = _dot(h, wg_ref[:, c * 256:(c + 1) * 256]) + bg_ref[:, c * 256:(c + 1) * 256]
        gate_ref[:, c * 256:(c + 1) * 256] = jax.nn.sigmoid(gl)


def _in_projection(x2, batch, seq, ln1_g, w_in, b_forget, b_gate, q_norm_a, k_norm_a, q_norm_b, k_norm_b):
    n = x2.shape[0]
    assert seq % TM_PROJ == 0 and all(TM_PROJ % (16 * d) == 0 for _, d in DSWA_PATTERNS)
    tiles_per_seq = seq // TM_PROJ
    wq = w_in[:, :QKV_WIDTH].astype(BF16)
    wf = jnp.pad(w_in[:, QKV_WIDTH:QKV_WIDTH + FOX_HEADS], ((0, 0), (0, LANES - FOX_HEADS))).astype(BF16)
    wg = w_in[:, QKV_WIDTH + FOX_HEADS:].astype(BF16)
    bf = jnp.pad(b_forget, (0, LANES - FOX_HEADS)).reshape(1, LANES)
    bg = b_gate.reshape(1, 2 * D_MODEL)
    scale = HEAD_DIM ** -0.5
    nrm = jnp.stack([jnp.tile(q_norm_a, 4) * scale, jnp.tile(k_norm_a, 4),
                     jnp.tile(q_norm_b, 4) * (scale * LOG2E), jnp.tile(k_norm_b, 4)])
    head_id = np.arange(256) // HEAD_DIM
    ones_bd = jnp.asarray(head_id[:, None] == head_id[None, :], BF16)
    src = np.arange(256)
    place = np.zeros((256, 4 * LANES), np.float32)
    place[src, (src // HEAD_DIM) * LANES + src % HEAD_DIM] = 1.0
    pat = np.zeros((2, 4 * LANES), np.float32)
    for slot in range(4):
        pat[0, [slot * LANES + l for l in FOX_Q_ONES]] = 1.0
        pat[1, [slot * LANES + l for l in FOX_K_ONES]] = 1.0
    const = lambda shape: pl.BlockSpec(shape, lambda i: (0,) * len(shape), pipeline_mode=pl.Buffered(1))
    row = lambda w: pl.BlockSpec((TM_PROJ, w), lambda i: (i, 0))
    seq_tile = lambda i: (i // tiles_per_seq, i % tiles_per_seq)
    group_spec = lambda d: pl.BlockSpec((1, TM_PROJ // d, d * DSWA_WIDTH), lambda i: (*seq_tile(i), 0))
    group_shape = lambda d: jax.ShapeDtypeStruct((batch, seq // d, d * DSWA_WIDTH), BF16)
    vt_spec = pl.BlockSpec((1, FOX_WIDTH, TM_PROJ), lambda i: (seq_tile(i)[0], 0, seq_tile(i)[1]))
    dils = [d for _, d in DSWA_PATTERNS]
    return pl.pallas_call(
        _inproj_kernel,
        grid=(n // TM_PROJ,),
        in_specs=[row(D_MODEL), const((1, D_MODEL)), const((D_MODEL, QKV_WIDTH)), const((D_MODEL, LANES)),
                  const((D_MODEL, 2 * D_MODEL)), const((1, LANES)), const((1, 2 * D_MODEL)),
                  const((4, 256)), const((256, 256)), const((256, 4 * LANES)), const((2, 4 * LANES))],
        out_specs=[group_spec(d) for d in dils] + [row(FOX_HEADS * LANES), row(FOX_HEADS * LANES), vt_spec,
                                                   row(LANES), row(2 * D_MODEL)],
        out_shape=[group_shape(d) for d in dils] + [jax.ShapeDtypeStruct((n, FOX_HEADS * LANES), BF16),
                                                    jax.ShapeDtypeStruct((n, FOX_HEADS * LANES), BF16),
                                                    jax.ShapeDtypeStruct((batch, FOX_WIDTH, seq), BF16),
                                                    jax.ShapeDtypeStruct((n, LANES), F32),
                                                    jax.ShapeDtypeStruct((n, 2 * D_MODEL), F32)],
        scratch_shapes=[pltpu.VMEM((2, TM_PROJ, LANES), F32)],
        compiler_params=pltpu.CompilerParams(dimension_semantics=("arbitrary",), vmem_limit_bytes=VMEM_LIMIT),
        name="in_projection",
    )(x2, ln1_g.reshape(1, D_MODEL), wq, wf, wg, bf, bg, nrm, ones_bd, jnp.asarray(place, BF16), jnp.asarray(pat))


def _t5_bucket(n):
    max_exact = NUM_BUCKETS // 2
    n_safe = np.maximum(n, 1).astype(np.float64)
    large = max_exact + (np.log(n_safe / max_exact) / np.log(MAX_DISTANCE / max_exact)
                         * (NUM_BUCKETS - max_exact)).astype(np.int64)
    large = np.minimum(large, NUM_BUCKETS - 1)
    return np.where(n < max_exact, n, large).astype(np.int32)


def _dswa_kernel(q_ref, kp_ref, kc_ref, vp_ref, vc_ref, bias_ref, o_ref, l_ref, *, blocks):
    first = pl.program_id(2) == 0
    items = [(j, h) for j in range(blocks) for h in range(DSWA_HG)]
    rows = lambda j: slice(j * ATTN_BLOCK, (j + 1) * ATTN_BLOCK)
    head = lambda h: slice(h * HEAD_DIM, (h + 1) * HEAD_DIM)

    sps, scs = [], []
    for j, h in items:
        q = q_ref[0, rows(j), head(h)]
        k_prev = kp_ref[0, :, head(h)] if j == 0 else kc_ref[0, rows(j - 1), head(h)]
        sp = _dot_nt(q, k_prev) + bias_ref[h, :, :ATTN_BLOCK]
        if j == 0:
            sp = jnp.where(first, NEG_INF, sp)
        sps.append(sp)
        scs.append(_dot_nt(q, kc_ref[0, rows(j), head(h)]) + bias_ref[h, :, ATTN_BLOCK:])
    ms = [jnp.maximum(jnp.max(sp, axis=-1, keepdims=True), jnp.max(sc, axis=-1, keepdims=True))
          for sp, sc in zip(sps, scs)]
    pps = [jnp.exp(sp - m) for sp, m in zip(sps, ms)]
    pcs = [jnp.exp(sc - m) for sc, m in zip(scs, ms)]
    dens = [jnp.sum(pp, axis=-1, keepdims=True) + jnp.sum(pc, axis=-1, keepdims=True) for pp, pc in zip(pps, pcs)]
    for (j, h), pp, pc, m, den in zip(items, pps, pcs, ms, dens):
        v_prev = vp_ref[0, :, head(h)] if j == 0 else vc_ref[0, rows(j - 1), head(h)]
        o = _dot(pp.astype(BF16), v_prev) + _dot(pc.astype(BF16), vc_ref[0, rows(j), head(h)])
        o_ref[0, rows(j), head(h)] = o / den
        l_ref[0, rows(j), head(h)] = jnp.broadcast_to(m + jnp.log(den), (ATTN_BLOCK, HEAD_DIM))


def _dilated_attention(qkv3, rel_bias, g, batch, seq):
    window, dil = DSWA_PATTERNS[g]
    steps = window // dil
    assert steps == ATTN_BLOCK
    length = seq // dil
    assert length % ATTN_BLOCK == 0
    nblk = length // ATTN_BLOCK
    span = 2 * ATTN_BLOCK
    back = np.arange(ATTN_BLOCK)[:, None] + steps - np.arange(span)[None, :]
    valid = (back >= 0) & (back <= steps)
    bucket = _t5_bucket(np.clip(back, 0, steps) * dil)
    cols = rel_bias[:, g * DSWA_HG:(g + 1) * DSWA_HG]
    onehot = np.asarray(bucket.reshape(-1)[:, None] == np.arange(NUM_BUCKETS)[None, :], np.float32)
    bias = jnp.dot(jnp.asarray(onehot), cols.astype(F32), precision=lax.Precision.HIGHEST)
    bias = jnp.transpose(bias.reshape(ATTN_BLOCK, span, DSWA_HG), (2, 0, 1))
    bias = jnp.where(valid[None], bias, NEG_INF)

    per_pos = DSWA_WIDTH // 256
    nb = math.gcd(nblk, DSWA_BLOCKS_PER_STEP)
    blk = (1, nb * ATTN_BLOCK, 256)
    one = (1, ATTN_BLOCK, 256)
    before = lambda n: jnp.maximum(n * nb - 1, 0)
    q_spec = pl.BlockSpec(blk, lambda b, r, n: (b, n, r * per_pos))
    kc_spec = pl.BlockSpec(blk, lambda b, r, n: (b, n, r * per_pos + 1))
    kp_spec = pl.BlockSpec(one, lambda b, r, n: (b, before(n), r * per_pos + 1))
    vc_spec = pl.BlockSpec(blk, lambda b, r, n: (b, n, r * per_pos + 2))
    vp_spec = pl.BlockSpec(one, lambda b, r, n: (b, before(n), r * per_pos + 2))
    bias_spec = pl.BlockSpec((DSWA_HG, ATTN_BLOCK, span), lambda b, r, n: (0, 0, 0))
    o_spec = pl.BlockSpec(blk, lambda b, r, n: (b, n, r))
    return pl.pallas_call(
        functools.partial(_dswa_kernel, blocks=nb),
        grid=(batch, dil, nblk // nb),
        in_specs=[q_spec, kp_spec, kc_spec, vp_spec, vc_spec, bias_spec],
        out_specs=[o_spec, o_spec],
        out_shape=[jax.ShapeDtypeStruct((batch, length, dil * DSWA_OUT), F32)] * 2,
        compiler_params=pltpu.CompilerParams(dimension_semantics=("arbitrary",) * 3),
        name=f"dilated_attention_{g}",
    )(qkv3, qkv3, qkv3, qkv3, qkv3, bias)


def _fox_kernel(q_ref, k_ref, lf_ref, vt_ref, put_ref, o_ref, kaug_s, carry_s):
    qi = pl.program_id(1)
    tq = TQ_FOX
    krow = lax.broadcasted_iota(jnp.int32, (tq, tq), 0)
    qcol = lax.broadcasted_iota(jnp.int32, (tq, tq), 1)
    causal = krow <= qcol

    @pl.when(qi == 0)
    def _():
        carry_s[...] = jnp.zeros_like(carry_s)

    tri = jnp.where(krow >= qcol, 1.0, 0.0).astype(BF16)
    c = carry_s[...]
    for term in _split3_bf16(lf_ref[...]):
        c = c + _dot(tri, term)
    carry_s[...] = c[tq - 1:tq, :]
    terms = _split3_bf16(c)
    q_bias = _dot(terms[0], put_ref[0]) + _dot(terms[1], put_ref[1]) + _dot(terms[2], put_ref[2])
    k_bias = _dot(terms[0], put_ref[3]) + _dot(terms[1], put_ref[4]) + _dot(terms[2], put_ref[5])
    q_all = q_ref[...] + q_bias.astype(BF16)
    kaug_s[pl.ds(pl.multiple_of(qi * tq, tq), tq), :] = k_ref[...] + k_bias.astype(BF16)

    for hg in range(FOX_HEADS // FOX_HEADS_PER_LOOP):
        heads = [hg * FOX_HEADS_PER_LOOP + i for i in range(FOX_HEADS_PER_LOOP)]
        qs = [q_all[:, h * LANES:(h + 1) * LANES] for h in heads]

        def step(kb, carry, masked):
            ks = pl.multiple_of(kb * tq, tq)
            idx = range(len(heads))
            sts = []
            for i, h in enumerate(heads):
                st = _dot_nt(kaug_s[pl.ds(ks, tq), h * LANES:(h + 1) * LANES], qs[i])
                sts.append(jnp.where(causal, st, NEG_INF) if masked else st)
            m_new = [jnp.maximum(carry[i][0], jnp.max(sts[i], axis=0, keepdims=True)) for i in idx]
            alpha = [jnp.exp2(carry[i][0] - m_new[i]) for i in idx]
            ps = [jnp.exp2(sts[i] - m_new[i]) for i in idx]
            ls = [alpha[i] * carry[i][1] + jnp.sum(ps[i], axis=0, keepdims=True) for i in idx]
            out = []
            for i, h in enumerate(heads):
                vt = vt_ref[0, h * HEAD_DIM:(h + 1) * HEAD_DIM, pl.ds(ks, tq)]
                out.append((m_new[i], ls[i], alpha[i] * carry[i][2] + _dot(vt, ps[i].astype(BF16))))
            return tuple(out)

        init = tuple((jnp.full((1, tq), NEG_INF, F32), jnp.zeros((1, tq), F32), jnp.zeros((HEAD_DIM, tq), F32))
                     for _ in heads)
        carry = lax.fori_loop(0, qi, lambda kb, c: step(kb, c, False), init)
        carry = step(qi, carry, True)
        for i, h in enumerate(heads):
            _, l, acc = carry[i]
            o_ref[0, h * HEAD_DIM:(h + 1) * HEAD_DIM, :] = (acc / l).astype(BF16)


def _top16(v):
    bits = lax.bitcast_convert_type(v, jnp.uint32) & jnp.uint32(0xFFFF0000)
    return lax.bitcast_convert_type(bits, F32)


def _split3_bf16(c):
    hi = _top16(c)
    r1 = c - hi
    mid = _top16(r1)
    lo = r1 - mid
    return hi.astype(BF16), mid.astype(BF16), lo.astype(BF16)


def _forgetting_attention(qb, kb, lf, vt, batch, seq):
    nq = seq // TQ_FOX
    put = np.zeros((6, LANES, FOX_HEADS * LANES), np.float32)
    for h in range(FOX_HEADS):
        for j in range(3):
            put[j, h, h * LANES + FOX_Q_C[j]] = 1.0
            put[3 + j, h, h * LANES + FOX_K_C[j]] = -1.0
    tile = lambda w: pl.BlockSpec((TQ_FOX, w), lambda b, i: (b * nq + i, 0))
    return pl.pallas_call(
        _fox_kernel,
        grid=(batch, nq),
        in_specs=[tile(FOX_HEADS * LANES), tile(FOX_HEADS * LANES), tile(LANES),
                  pl.BlockSpec((1, FOX_WIDTH, seq), lambda b, i: (b, 0, 0)),
                  pl.BlockSpec(put.shape, lambda b, i: (0, 0, 0), pipeline_mode=pl.Buffered(1))],
        out_specs=pl.BlockSpec((1, FOX_WIDTH, TQ_FOX), lambda b, i: (b, 0, i)),
        out_shape=jax.ShapeDtypeStruct((batch, FOX_WIDTH, seq), BF16),
        scratch_shapes=[pltpu.VMEM((seq, FOX_HEADS * LANES), BF16), pltpu.VMEM((1, LANES), F32)],
        compiler_params=pltpu.CompilerParams(dimension_semantics=("arbitrary", "arbitrary"),
                                             vmem_limit_bytes=VMEM_LIMIT),
        name="forgetting_attention",
    )(qb, kb, lf, vt, jnp.asarray(put, BF16))


def _merge_kernel(x_ref, oa0_ref, oa1_ref, oa2_ref, ls0_ref, ls1_ref, ls2_ref, obt_ref, gate_ref,
                  wa_ref, wb_ref, wo_ref, g2_ref, wrh_ref, wrl_ref, br_ref,
                  x1_ref, h2_ref, ri_ref, rg_ref, cnt_ref, run_s, stage_s):
    tm = TM_MERGE

    @pl.when(pl.program_id(0) == 0)
    def _():
        run_s[...] = jnp.zeros_like(run_s)

    def token_order(ref, g, stage):
        dil = DSWA_PATTERNS[g][1]
        if dil == 1:
            return ref[0]
        for r in range(dil):
            for half in range(2):
                col = r * DSWA_OUT + half * LANES
                stage[half, pl.ds(r, tm // dil, stride=dil), :] = ref[0, :, col:col + LANES]
        return jnp.concatenate([stage[0], stage[1]], axis=1)

    l0, l1, l2 = (token_order(ref, g, stage_s.at[g]) for g, ref in enumerate((ls0_ref, ls1_ref, ls2_ref)))
    lm = jnp.maximum(jnp.maximum(l0, l1), l2)
    e0, e1, e2 = jnp.exp(l0 - lm), jnp.exp(l1 - lm), jnp.exp(l2 - lm)
    o0, o1, o2 = (token_order(ref, g, stage_s.at[DSWA_GROUPS + g])
                  for g, ref in enumerate((oa0_ref, oa1_ref, oa2_ref)))
    o_a = (e0 * o0 + e1 * o1 + e2 * o2) / (e0 + e1 + e2)
    ya = _dot(o_a.astype(BF16), wa_ref[...])
    yb = lax.dot_general(obt_ref[0], wb_ref[...], (((0,), (0,)), ((), ())), preferred_element_type=F32)
    merged = gate_ref[:, :D_MODEL] * ya + gate_ref[:, D_MODEL:] * yb
    x1 = x_ref[...] + _dot(merged.astype(BF16), wo_ref[...])
    x1_ref[...] = x1

    ms = jnp.mean(x1 * x1, axis=-1, keepdims=True)
    h2 = x1 * lax.rsqrt(ms + RMS_EPS) * g2_ref[...]
    for c in range(ROW_CHUNKS):
        h2_ref[pl.ds(c, tm, stride=ROW_CHUNKS), :] = h2[:, c * LANES:(c + 1) * LANES]

    hi = h2.astype(BF16)
    lo = (h2 - hi.astype(F32)).astype(BF16)
    logits = _dot(hi, wrh_ref[...]) + _dot(lo, wrh_ref[...]) + _dot(hi, wrl_ref[...]) + br_ref[...]

    lane = lax.broadcasted_iota(jnp.int32, (tm, LANES), 1).astype(F32)
    work = logits
    top_v, top_i = [], []
    for _ in range(TOP_K):
        mk = jnp.max(work, axis=-1, keepdims=True)
        ik = jnp.min(jnp.where(work == mk, lane, float(LANES)), axis=-1, keepdims=True)
        top_v.append(mk)
        top_i.append(ik)
        work = jnp.where(lane == ik, -jnp.inf, work)
    ex = [jnp.exp(v - top_v[0]) for v in top_v]
    den = ex[0] + ex[1] + ex[2] + ex[3]

    onehot = jnp.zeros((tm, LANES), F32)
    for ik in top_i:
        onehot = onehot + jnp.where(lane == ik, 1.0, 0.0)
    r_i = lax.broadcasted_iota(jnp.int32, (tm, tm), 0)
    c_i = lax.broadcasted_iota(jnp.int32, (tm, tm), 1)
    tri = jnp.where(c_i < r_i, 1.0, 0.0).astype(BF16)
    before = _dot(tri, onehot.astype(BF16)) + run_s[...]
    ri = jnp.zeros((tm, LANES), jnp.int32)
    rg = jnp.zeros((tm, LANES), F32)
    for k in range(TOP_K):
        rank = jnp.sum(jnp.where(lane == top_i[k], before, 0.0), axis=-1, keepdims=True)
        ri = jnp.where(lane == k, top_i[k].astype(jnp.int32), ri)
        ri = jnp.where(lane == TOP_K + k, rank.astype(jnp.int32), ri)
        rg = jnp.where(lane == k, ex[k] / den, rg)
    ri_ref[...] = ri
    rg_ref[...] = rg
    run = run_s[...] + jnp.sum(onehot, axis=0, keepdims=True)
    run_s[...] = run
    cnt_ref[...] = run


def _merge_and_route(x2, batch, seq, oas, lss, obt, gate, w_branch_a, w_branch_b, w_out, ln2_g, w_router,
                     b_router):
    n = x2.shape[0]
    assert seq % TM_MERGE == 0 and all(TM_MERGE % (SUBLANES * d) == 0 for _, d in DSWA_PATTERNS)
    tiles_per_seq = seq // TM_MERGE
    seq_tile = lambda i: (i // tiles_per_seq, i % tiles_per_seq)
    group_specs = [pl.BlockSpec((1, TM_MERGE // d, d * DSWA_OUT), lambda i: (*seq_tile(i), 0))
                   for _, d in DSWA_PATTERNS]
    obt_spec = pl.BlockSpec((1, FOX_WIDTH, TM_MERGE), lambda i: (seq_tile(i)[0], 0, seq_tile(i)[1]))
    wr = jnp.pad(w_router, ((0, 0), (0, LANES - N_EXPERTS)))
    wr_top = _top16(wr)
    wr_hi = wr_top.astype(BF16)
    wr_lo = (wr - wr_top).astype(BF16)
    br = jnp.concatenate([b_router, jnp.full((LANES - N_EXPERTS,), NEG_INF, F32)]).reshape(1, LANES)
    const = lambda shape: pl.BlockSpec(shape, lambda i: (0,) * len(shape), pipeline_mode=pl.Buffered(1))
    row = lambda w: pl.BlockSpec((TM_MERGE, w), lambda i: (i, 0))
    return pl.pallas_call(
        _merge_kernel,
        grid=(n // TM_MERGE,),
        in_specs=[row(D_MODEL)] + group_specs + group_specs + [obt_spec, row(2 * D_MODEL),
                  const((DSWA_OUT, D_MODEL)), const((FOX_WIDTH, D_MODEL)), const((D_MODEL, D_MODEL)),
                  const((1, D_MODEL)), const((D_MODEL, LANES)), const((D_MODEL, LANES)), const((1, LANES))],
        out_specs=[row(D_MODEL), pl.BlockSpec((TM_MERGE * ROW_CHUNKS, LANES), lambda i: (i, 0)),
                   row(LANES), row(LANES), pl.BlockSpec((1, LANES), lambda i: (0, 0))],
        out_shape=[jax.ShapeDtypeStruct((n, D_MODEL), F32),
                   jax.ShapeDtypeStruct((n * ROW_CHUNKS, LANES), F32),
                   jax.ShapeDtypeStruct((n, LANES), jnp.int32),
                   jax.ShapeDtypeStruct((n, LANES), F32),
                   jax.ShapeDtypeStruct((1, LANES), F32)],
        scratch_shapes=[pltpu.VMEM((1, LANES), F32), pltpu.VMEM((2 * DSWA_GROUPS, 2, TM_MERGE, LANES), F32)],
        compiler_params=pltpu.CompilerParams(dimension_semantics=("arbitrary",), vmem_limit_bytes=VMEM_LIMIT),
        name="merge_and_route",
    )(x2, *oas, *lss, obt, gate, w_branch_a.astype(BF16), w_branch_b.astype(BF16), w_out.astype(BF16),
      ln2_g.reshape(1, D_MODEL), wr_hi, wr_lo, br)


def _row_copy(src, src_row, dst, dst_row, sem):
    return pltpu.make_async_copy(src.at[pl.ds(pl.multiple_of(src_row * ROW_CHUNKS, ROW_CHUNKS), ROW_CHUNKS)],
                                 dst.at[pl.ds(pl.multiple_of(dst_row * ROW_CHUNKS, ROW_CHUNKS), ROW_CHUNKS)], sem)


def _dispatch_kernel(zb_ref, zf_ref, dest_hbm, h2_ref, xs_hbm, idx_s, zeros_v, sem_i, sem_z, sem_r):
    i = pl.program_id(0)
    n_move = TOP_K * TC_MOVE
    idx_copy = pltpu.make_async_copy(dest_hbm.at[i], idx_s, sem_i)
    idx_copy.start()

    def zero_copy(e):
        blk = pl.multiple_of(zb_ref[e] * (T_EXPERT * ROW_CHUNKS), T_EXPERT * ROW_CHUNKS)
        return pltpu.make_async_copy(zeros_v, xs_hbm.at[pl.ds(blk, T_EXPERT * ROW_CHUNKS)], sem_z)

    @pl.when(i == 0)
    def _():
        zeros_v[...] = jnp.zeros_like(zeros_v)
        for e in range(N_EXPERTS):
            @pl.when(zf_ref[e] > 0)
            def _():
                zero_copy(e).start()
        for e in range(N_EXPERTS):
            @pl.when(zf_ref[e] > 0)
            def _():
                zero_copy(e).wait()

    idx_copy.wait()

    for k in range(TOP_K):
        def issue(to, carry, k=k):
            t0 = to * DMA_UNROLL
            for u in range(DMA_UNROLL):
                _row_copy(h2_ref, t0 + u, xs_hbm, idx_s[k * TC_MOVE + t0 + u], sem_r).start(priority=u % 2)
            return carry

        lax.fori_loop(0, TC_MOVE // DMA_UNROLL, issue, 0)

    def drain(jo, carry):
        for u in range(DMA_UNROLL):
            _row_copy(h2_ref, 0, xs_hbm, 0, sem_r).wait()
        return carry

    lax.fori_loop(0, n_move // DMA_UNROLL, drain, 0)


def _dispatch(dest_tiles, zero_blk, zero_flag, h2_rows, n_slots):
    nt = dest_tiles.shape[0]
    return pl.pallas_call(
        _dispatch_kernel,
        grid_spec=pltpu.PrefetchScalarGridSpec(
            num_scalar_prefetch=2, grid=(nt,),
            in_specs=[pl.BlockSpec(memory_space=pl.ANY),
                      pl.BlockSpec((TC_MOVE * ROW_CHUNKS, LANES), lambda i, zb, zf: (i, 0))],
            out_specs=pl.BlockSpec(memory_space=pl.ANY),
            scratch_shapes=[pltpu.SMEM((TOP_K * TC_MOVE,), jnp.int32),
                            pltpu.VMEM((T_EXPERT * ROW_CHUNKS, LANES), F32),
                            pltpu.SemaphoreType.DMA, pltpu.SemaphoreType.DMA, pltpu.SemaphoreType.DMA]),
        out_shape=jax.ShapeDtypeStruct((n_slots * ROW_CHUNKS, LANES), F32),
        compiler_params=pltpu.CompilerParams(dimension_semantics=("arbitrary",)),
        name="dispatch_rows",
    )(zero_blk, zero_flag, dest_tiles, h2_rows)


def _expert_kernel(be_ref, nu_ref, xs_ref, wgu_ref, bgu_ref, wd_ref, bd_ref, ys_ref, wgu_s, wd_s):
    j = pl.program_id(0)
    t = T_EXPERT

    @pl.when(j < nu_ref[0])
    def _():
        prev = be_ref[jnp.maximum(j - 1, 0)]

        @pl.when((j == 0) | (be_ref[j] != prev))
        def _():
            wgu_s[...] = wgu_ref[0].astype(BF16)
            wd_s[...] = wd_ref[0].astype(BF16)

        x = jnp.concatenate([xs_ref[pl.ds(c, t, stride=ROW_CHUNKS), :] for c in range(ROW_CHUNKS)], axis=1)
        gu = _dot(x.astype(BF16), wgu_s[...]) + bgu_ref[0]
        g = jnp.minimum(gu[:, :D_MODEL], SWIGLU_LIMIT)
        u = jnp.clip(gu[:, D_MODEL:], -SWIGLU_LIMIT, SWIGLU_LIMIT)
        mid = (u + 1.0) * (g * jax.nn.sigmoid(SWIGLU_ALPHA * g))
        y = _dot(mid.astype(BF16), wd_s[...]) + bd_ref[0]
        for c in range(ROW_CHUNKS):
            ys_ref[pl.ds(c, t, stride=ROW_CHUNKS), :] = y[:, c * LANES:(c + 1) * LANES]


def _expert_ffn(blk_e, n_used, xs_rows, w_gate_up, b_gate_up, w_down, b_down):
    n_blk = blk_e.shape[0]
    rows = T_EXPERT * ROW_CHUNKS
    live = lambda j, be, nu: jnp.minimum(j, nu[0] - 1)
    sink = lambda j, be, nu: jnp.where(j < nu[0], j, n_blk - 1)
    return pl.pallas_call(
        _expert_kernel,
        grid_spec=pltpu.PrefetchScalarGridSpec(
            num_scalar_prefetch=2, grid=(n_blk,),
            in_specs=[pl.BlockSpec((rows, LANES), lambda j, be, nu: (live(j, be, nu), 0)),
                      pl.BlockSpec((1, D_MODEL, 2 * D_MODEL), lambda j, be, nu: (be[live(j, be, nu)], 0, 0)),
                      pl.BlockSpec((1, 1, 2 * D_MODEL), lambda j, be, nu: (be[live(j, be, nu)], 0, 0)),
                      pl.BlockSpec((1, D_MODEL, D_MODEL), lambda j, be, nu: (be[live(j, be, nu)], 0, 0)),
                      pl.BlockSpec((1, 1, D_MODEL), lambda j, be, nu: (be[live(j, be, nu)], 0, 0))],
            out_specs=pl.BlockSpec((rows, LANES), lambda j, be, nu: (sink(j, be, nu), 0)),
            scratch_shapes=[pltpu.VMEM((D_MODEL, 2 * D_MODEL), BF16), pltpu.VMEM((D_MODEL, D_MODEL), BF16)]),
        out_shape=jax.ShapeDtypeStruct(xs_rows.shape, F32),
        compiler_params=pltpu.CompilerParams(dimension_semantics=("arbitrary",), vmem_limit_bytes=VMEM_LIMIT),
        name="expert_ffn",
    )(blk_e, n_used, xs_rows, w_gate_up, b_gate_up.reshape(N_EXPERTS, 1, 2 * D_MODEL),
      w_down, b_down.reshape(N_EXPERTS, 1, D_MODEL))


def _combine_kernel(dest_hbm, ys_hbm, x1_ref, rg_ref, o_ref, idx_s, buf, gate_s, sem_i, sem_r):
    i = pl.program_id(0)
    n_move = TOP_K * TC_MOVE
    idx_copy = pltpu.make_async_copy(dest_hbm.at[i], idx_s, sem_i)
    idx_copy.start()
    idx_copy.wait()

    def issue(jo, carry):
        for u in range(DMA_UNROLL):
            j = jo * DMA_UNROLL + u
            _row_copy(ys_hbm, idx_s[j], buf, j, sem_r).start(priority=u % 2)
        return carry

    lax.fori_loop(0, n_move // DMA_UNROLL, issue, 0)

    def drain(jo, carry):
        for u in range(DMA_UNROLL):
            _row_copy(ys_hbm, 0, buf, 0, sem_r).wait()
        return carry

    lax.fori_loop(0, n_move // DMA_UNROLL, drain, 0)

    for k in range(TOP_K):
        gate_s[k] = jnp.broadcast_to(rg_ref[:, k:k + 1], (TC_MOVE, LANES))
    for c in range(ROW_CHUNKS):
        acc = x1_ref[:, c * LANES:(c + 1) * LANES]
        for k in range(TOP_K):
            rows = buf[pl.ds(k * TC_MOVE * ROW_CHUNKS + c, TC_MOVE, stride=ROW_CHUNKS), :]
            acc = acc + gate_s[k] * rows
        o_ref[:, c * LANES:(c + 1) * LANES] = acc


def _combine(dest_tiles, ys_rows, x1, rg):
    n = x1.shape[0]
    nt = dest_tiles.shape[0]
    return pl.pallas_call(
        _combine_kernel,
        grid=(nt,),
        in_specs=[pl.BlockSpec(memory_space=pl.ANY), pl.BlockSpec(memory_space=pl.ANY),
                  pl.BlockSpec((TC_MOVE, D_MODEL), lambda i: (i, 0)),
                  pl.BlockSpec((TC_MOVE, LANES), lambda i: (i, 0))],
        out_specs=pl.BlockSpec((TC_MOVE, D_MODEL), lambda i: (i, 0)),
        out_shape=jax.ShapeDtypeStruct((n, D_MODEL), F32),
        scratch_shapes=[pltpu.SMEM((TOP_K * TC_MOVE,), jnp.int32),
                        pltpu.VMEM((TOP_K * TC_MOVE * ROW_CHUNKS, LANES), F32),
                        pltpu.VMEM((TOP_K, TC_MOVE, LANES), F32),
                        pltpu.SemaphoreType.DMA, pltpu.SemaphoreType.DMA],
        compiler_params=pltpu.CompilerParams(dimension_semantics=("arbitrary",), vmem_limit_bytes=VMEM_LIMIT),
        name="combine_rows",
    )(dest_tiles, ys_rows, x1, rg)


def _layer(x, ln1_g, w_in, b_forget, b_gate, q_norm_a, k_norm_a, q_norm_b, k_norm_b, rel_bias,
           w_branch_a, w_branch_b, w_out, ln2_g, w_router, b_router, w_gate_up, b_gate_up, w_down, b_down):
    batch, seq, _ = x.shape
    n = batch * seq
    x2 = x.reshape(n, D_MODEL)
    *qkv_groups, qb, kb, vt, lf, gate = _in_projection(x2, batch, seq, ln1_g, w_in, b_forget, b_gate,
                                                       q_norm_a, k_norm_a, q_norm_b, k_norm_b)

    oas, lss = [], []
    for g in range(DSWA_GROUPS):
        o, lse = _dilated_attention(qkv_groups[g], rel_bias, g, batch, seq)
        oas.append(o)
        lss.append(lse)

    obt = _forgetting_attention(qb, kb, lf, vt, batch, seq)

    x1, h2_rows, ri, rg, cnt = _merge_and_route(x2, batch, seq, oas, lss, obt, gate, w_branch_a, w_branch_b,
                                                w_out, ln2_g, w_router, b_router)

    counts = cnt[0, :N_EXPERTS].astype(jnp.int32)
    pcounts = ((counts + T_EXPERT - 1) // T_EXPERT) * T_EXPERT
    pends = jnp.cumsum(pcounts)
    pstarts = pends - pcounts
    idx, rank = ri[:, :TOP_K], ri[:, TOP_K:2 * TOP_K]
    start_of = jnp.sum(jnp.where(idx[..., None] == jnp.arange(N_EXPERTS), pstarts, 0), axis=-1)
    dest = start_of + rank
    nt = n // TC_MOVE
    dest_tiles = jnp.transpose(dest.reshape(nt, TC_MOVE, TOP_K), (0, 2, 1)).reshape(nt, TOP_K * TC_MOVE)
    n_slots = n * TOP_K + N_EXPERTS * T_EXPERT
    n_blk = n_slots // T_EXPERT
    blk_first = jnp.arange(n_blk, dtype=jnp.int32) * T_EXPERT
    blk_e = jnp.minimum(jnp.sum((pends[None, :] <= blk_first[:, None]).astype(jnp.int32), axis=1), N_EXPERTS - 1)
    n_used = (pends[-1:] // T_EXPERT).astype(jnp.int32)
    zero_blk = jnp.maximum(pends // T_EXPERT - 1, 0).astype(jnp.int32)
    zero_flag = (pcounts > 0).astype(jnp.int32)

    xs_rows = _dispatch(dest_tiles, zero_blk, zero_flag, h2_rows, n_slots)
    ys_rows = _expert_ffn(blk_e, n_used, xs_rows, w_gate_up, b_gate_up, w_down, b_down)
    out = _combine(dest_tiles, ys_rows, x1, rg)
    return out.reshape(batch, seq, D_MODEL)


def kernel(x, ln1_g, w_in, b_forget, b_gate, q_norm_a, k_norm_a, q_norm_b, k_norm_b, rel_bias, w_branch_a, w_branch_b, w_out, ln2_g, w_router, b_router, w_gate_up, b_gate_up, w_down, b_down):
    for layer in range(ln1_g.shape[0]):
        x = _layer(x, ln1_g[layer], w_in[layer], b_forget[layer], b_gate[layer], q_norm_a[layer],
                   k_norm_a[layer], q_norm_b[layer], k_norm_b[layer], rel_bias, w_branch_a[layer],
                   w_branch_b[layer], w_out[layer], ln2_g[layer], w_router[layer], b_router[layer],
                   w_gate_up[layer], b_gate_up[layer], w_down[layer], b_down[layer])
    return x
```

```python
import functools
import math

import numpy as np
import jax
import jax.numpy as jnp
from jax import lax
from jax.experimental import pallas as pl
from jax.experimental.pallas import tpu as pltpu

F32 = jnp.float32
BF16 = jnp.bfloat16

D_MODEL = 1024
HEAD_DIM = 64
DSWA_PATTERNS = ((128, 1), (512, 4), (2048, 16))
DSWA_HG = 4
DSWA_GROUPS = len(DSWA_PATTERNS)
DSWA_WIDTH = DSWA_GROUPS * DSWA_HG * HEAD_DIM
DSWA_OUT = DSWA_HG * HEAD_DIM
FOX_HEADS = 8
FOX_WIDTH = FOX_HEADS * HEAD_DIM
ATTN_BLOCK = 128
NUM_BUCKETS = 32
MAX_DISTANCE = 2048
N_EXPERTS = 32
TOP_K = 4
SWIGLU_LIMIT = 7.0
SWIGLU_ALPHA = 1.702
RMS_EPS = 1e-6
NEG_INF = -1e30
LOG2E = 1.4426950408889634

LANES = 128
SUBLANES = 8
ROW_CHUNKS = D_MODEL // LANES
QKV_WIDTH = 3 * DSWA_WIDTH + 3 * FOX_WIDTH
FOX_Q_ONES = (64, 65, 66)
FOX_K_C = (64, 65, 66)
FOX_Q_C = (67, 68, 69)
FOX_K_ONES = (67, 68, 69)
VMEM_LIMIT = 56 * 1024 * 1024

TM_PROJ = 512
PROJ_LOOKAHEAD = 1
TQ_FOX = 512
FOX_HEADS_PER_LOOP = 4
DSWA_BLOCKS_PER_STEP = 4
TM_MERGE = 512
T_EXPERT = 512
TC_MOVE = 1024
DMA_UNROLL = 16


def _dot(a, b):
    return jnp.dot(a, b, preferred_element_type=F32)


def _dot_nt(a, b):
    return lax.dot_general(a, b, (((1,), (1,)), ((), ())), preferred_element_type=F32)


def _inproj_kernel(x_ref, g1_ref, wq_ref, wf_ref, wg_ref, bf_ref, bg_ref, nrm_ref, e_ref, place_ref, pat_ref,
                   qkv0_ref, qkv1_ref, qkv2_ref, qb_ref, kb_ref, vt_ref, lf_ref, gate_ref, stage_ref):
    group_refs = (qkv0_ref, qkv1_ref, qkv2_ref)
    x = x_ref[...]
    ms = jnp.mean(x * x, axis=-1, keepdims=True)
    h = (x * lax.rsqrt(ms + RMS_EPS) * g1_ref[...]).astype(BF16)
    ones_bd = e_ref[...]

    def head_norm(y, gn):
        sq = y * y
        hi = sq.astype(BF16)
        lo = (sq - hi.astype(F32)).astype(BF16)
        ss = _dot(hi, ones_bd) + _dot(lo, ones_bd)
        return y * lax.rsqrt(ss * (1.0 / HEAD_DIM) + RMS_EPS) * gn

    n_a = 3 * DSWA_WIDTH // 256

    def qkv_epilogue(c, y):
        if c < 3:
            y = head_norm(y, nrm_ref[0:1, :])
        elif c < 6:
            y = head_norm(y, nrm_ref[1:2, :])
        elif 9 <= c < 11:
            y = head_norm(y, nrm_ref[2:3, :])
        elif 11 <= c < 13:
            y = head_norm(y, nrm_ref[3:4, :])
        if c < n_a:
            g, part = c % DSWA_GROUPS, c // DSWA_GROUPS
            dil = DSWA_PATTERNS[g][1]
            if dil == 1:
                group_refs[g][0, :, part * 256:(part + 1) * 256] = y.astype(BF16)
            else:
                for half in range(2):
                    stage_ref[half] = y[:, half * LANES:(half + 1) * LANES]
                for r in range(dil):
                    for half in range(2):
                        col = r * DSWA_WIDTH + part * 256 + half * LANES
                        rows = stage_ref[half, pl.ds(r, TM_PROJ // dil, stride=dil), :]
                        group_refs[g][0, :, col:col + LANES] = rows.astype(BF16)
        elif c < n_a + 4:
            is_k, half = divmod(c - n_a, 2)
            slots = _dot(y.astype(BF16), place_ref[...]) + pat_ref[is_k:is_k + 1, :]
            (kb_ref if is_k else qb_ref)[:, half * 512:(half + 1) * 512] = slots.astype(BF16)
        else:
            half = c - n_a - 4
            vt_ref[0, half * 256:(half + 1) * 256, :] = y.T.astype(BF16)

    def forget_epilogue(y):
        z = y + bf_ref[...]
        lf_ref[...] = (jnp.minimum(z, 0.0) - jnp.log1p(jnp.exp(-jnp.abs(z)))) * LOG2E

    def gate_epilogue(c, y):
        gate_ref[:, c * 256:(c + 1) * 256] = jax.nn.sigmoid(y + bg_ref[:, c * 256:(c + 1) * 256])

    jobs = [(functools.partial(lambda c: _dot(h, wq_ref[:, c * 256:(c + 1) * 256]), c),
             functools.partial(qkv_epilogue, c)) for c in range(QKV_WIDTH // 256)]
    jobs.append((lambda: _dot(h, wf_ref[...]), forget_epilogue))
    jobs += [(functools.partial(lambda c: _dot(h, wg_ref[:, c * 256:(c + 1) * 256]), c),
              functools.partial(gate_epilogue, c)) for c in range(2 * D_MODEL // 256)]
    pending = [matmul() for matmul, _ in jobs[:PROJ_LOOKAHEAD]]
    for j, (_, epilogue) in enumerate(jobs):
        if j + PROJ_LOOKAHEAD < len(jobs):
            pending.append(jobs[j + PROJ_LOOKAHEAD][0]())
        epilogue(pending.pop(0))


def _in_projection(x2, batch, seq, ln1_g, w_in, b_forget, b_gate, q_norm_a, k_norm_a, q_norm_b, k_norm_b):
    n = x2.shape[0]
    assert seq % TM_PROJ == 0 and all(TM_PROJ % (16 * d) == 0 for _, d in DSWA_PATTERNS)
    tiles_per_seq = seq // TM_PROJ
    wq = w_in[:, :QKV_WIDTH].astype(BF16)
    wf = jnp.pad(w_in[:, QKV_WIDTH:QKV_WIDTH + FOX_HEADS], ((0, 0), (0, LANES - FOX_HEADS))).astype(BF16)
    wg = w_in[:, QKV_WIDTH + FOX_HEADS:].astype(BF16)
    bf = jnp.pad(b_forget, (0, LANES - FOX_HEADS)).reshape(1, LANES)
    bg = b_gate.reshape(1, 2 * D_MODEL)
    scale = HEAD_DIM ** -0.5
    nrm = jnp.stack([jnp.tile(q_norm_a, 4) * scale, jnp.tile(k_norm_a, 4),
                     jnp.tile(q_norm_b, 4) * (scale * LOG2E), jnp.tile(k_norm_b, 4)])
    head_id = np.arange(256) // HEAD_DIM
    ones_bd = jnp.asarray(head_id[:, None] == head_id[None, :], BF16)
    src = np.arange(256)
    place = np.zeros((256, 4 * LANES), np.float32)
    place[src, (src // HEAD_DIM) * LANES + src % HEAD_DIM] = 1.0
    pat = np.zeros((2, 4 * LANES), np.float32)
    for slot in range(4):
        pat[0, [slot * LANES + l for l in FOX_Q_ONES]] = 1.0
        pat[1, [slot * LANES + l for l in FOX_K_ONES]] = 1.0
    const = lambda shape: pl.BlockSpec(shape, lambda i: (0,) * len(shape), pipeline_mode=pl.Buffered(1))
    row = lambda w: pl.BlockSpec((TM_PROJ, w), lambda i: (i, 0))
    seq_tile = lambda i: (i // tiles_per_seq, i % tiles_per_seq)
    group_spec = lambda d: pl.BlockSpec((1, TM_PROJ // d, d * DSWA_WIDTH), lambda i: (*seq_tile(i), 0))
    group_shape = lambda d: jax.ShapeDtypeStruct((batch, seq // d, d * DSWA_WIDTH), BF16)
    vt_spec = pl.BlockSpec((1, FOX_WIDTH, TM_PROJ), lambda i: (seq_tile(i)[0], 0, seq_tile(i)[1]))
    dils = [d for _, d in DSWA_PATTERNS]
    return pl.pallas_call(
        _inproj_kernel,
        grid=(n // TM_PROJ,),
        in_specs=[row(D_MODEL), const((1, D_MODEL)), const((D_MODEL, QKV_WIDTH)), const((D_MODEL, LANES)),
                  const((D_MODEL, 2 * D_MODEL)), const((1, LANES)), const((1, 2 * D_MODEL)),
                  const((4, 256)), const((256, 256)), const((256, 4 * LANES)), const((2, 4 * LANES))],
        out_specs=[group_spec(d) for d in dils] + [row(FOX_HEADS * LANES), row(FOX_HEADS * LANES), vt_spec,
                                                   row(LANES), row(2 * D_MODEL)],
        out_shape=[group_shape(d) for d in dils] + [jax.ShapeDtypeStruct((n, FOX_HEADS * LANES), BF16),
                                                    jax.ShapeDtypeStruct((n, FOX_HEADS * LANES), BF16),
                                                    jax.ShapeDtypeStruct((batch, FOX_WIDTH, seq), BF16),
                                                    jax.ShapeDtypeStruct((n, LANES), F32),
                                                    jax.ShapeDtypeStruct((n, 2 * D_MODEL), F32)],
        scratch_shapes=[pltpu.VMEM((2, TM_PROJ, LANES), F32)],
        compiler_params=pltpu.CompilerParams(dimension_semantics=("arbitrary",), vmem_limit_bytes=VMEM_LIMIT),
        name="in_projection",
    )(x2, ln1_g.reshape(1, D_MODEL), wq, wf, wg, bf, bg, nrm, ones_bd, jnp.asarray(place, BF16), jnp.asarray(pat))


def _t5_bucket(n):
    max_exact = NUM_BUCKETS // 2
    n_safe = np.maximum(n, 1).astype(np.float64)
    large = max_exact + (np.log(n_safe / max_exact) / np.log(MAX_DISTANCE / max_exact)
                         * (NUM_BUCKETS - max_exact)).astype(np.int64)
    large = np.minimum(large, NUM_BUCKETS - 1)
    return np.where(n < max_exact, n, large).astype(np.int32)


def _dswa_kernel(q_ref, kp_ref, kc_ref, vp_ref, vc_ref, bias_ref, o_ref, l_ref, *, blocks):
    first = pl.program_id(2) == 0
    items = [(j, h) for j in range(blocks) for h in range(DSWA_HG)]
    rows = lambda j: slice(j * ATTN_BLOCK, (j + 1) * ATTN_BLOCK)
    head = lambda h: slice(h * HEAD_DIM, (h + 1) * HEAD_DIM)

    sps, scs = [], []
    for j, h in items:
        q = q_ref[0, rows(j), head(h)]
        k_prev = kp_ref[0, :, head(h)] if j == 0 else kc_ref[0, rows(j - 1), head(h)]
        sp = _dot_nt(q, k_prev) + bias_ref[h, :, :ATTN_BLOCK]
        if j == 0:
            sp = jnp.where(first, NEG_INF, sp)
        sps.append(sp)
        scs.append(_dot_nt(q, kc_ref[0, rows(j), head(h)]) + bias_ref[h, :, ATTN_BLOCK:])
    ms = [jnp.maximum(jnp.max(sp, axis=-1, keepdims=True), jnp.max(sc, axis=-1, keepdims=True))
          for sp, sc in zip(sps, scs)]
    pps = [jnp.exp(sp - m) for sp, m in zip(sps, ms)]
    pcs = [jnp.exp(sc - m) for sc, m in zip(scs, ms)]
    dens = [jnp.sum(pp, axis=-1, keepdims=True) + jnp.sum(pc, axis=-1, keepdims=True) for pp, pc in zip(pps, pcs)]
    for (j, h), pp, pc, m, den in zip(items, pps, pcs, ms, dens):
        v_prev = vp_ref[0, :, head(h)] if j == 0 else vc_ref[0, rows(j - 1), head(h)]
        o = _dot(pp.astype(BF16), v_prev) + _dot(pc.astype(BF16), vc_ref[0, rows(j), head(h)])
        o_ref[0, rows(j), head(h)] = o / den
        l_ref[0, rows(j), head(h)] = jnp.broadcast_to(m + jnp.log(den), (ATTN_BLOCK, HEAD_DIM))


def _dilated_attention(qkv3, rel_bias, g, batch, seq):
    window, dil = DSWA_PATTERNS[g]
    steps = window // dil
    assert steps == ATTN_BLOCK
    length = seq // dil
    assert length % ATTN_BLOCK == 0
    nblk = length // ATTN_BLOCK
    span = 2 * ATTN_BLOCK
    back = np.arange(ATTN_BLOCK)[:, None] + steps - np.arange(span)[None, :]
    valid = (back >= 0) & (back <= steps)
    bucket = _t5_bucket(np.clip(back, 0, steps) * dil)
    cols = rel_bias[:, g * DSWA_HG:(g + 1) * DSWA_HG]
    onehot = np.asarray(bucket.reshape(-1)[:, None] == np.arange(NUM_BUCKETS)[None, :], np.float32)
    bias = jnp.dot(jnp.asarray(onehot), cols.astype(F32), precision=lax.Precision.HIGHEST)
    bias = jnp.transpose(bias.reshape(ATTN_BLOCK, span, DSWA_HG), (2, 0, 1))
    bias = jnp.where(valid[None], bias, NEG_INF)

    per_pos = DSWA_WIDTH // 256
    nb = math.gcd(nblk, DSWA_BLOCKS_PER_STEP)
    blk = (1, nb * ATTN_BLOCK, 256)
    one = (1, ATTN_BLOCK, 256)
    before = lambda n: jnp.maximum(n * nb - 1, 0)
    q_spec = pl.BlockSpec(blk, lambda b, r, n: (b, n, r * per_pos))
    kc_spec = pl.BlockSpec(blk, lambda b, r, n: (b, n, r * per_pos + 1))
    kp_spec = pl.BlockSpec(one, lambda b, r, n: (b, before(n), r * per_pos + 1))
    vc_spec = pl.BlockSpec(blk, lambda b, r, n: (b, n, r * per_pos + 2))
    vp_spec = pl.BlockSpec(one, lambda b, r, n: (b, before(n), r * per_pos + 2))
    bias_spec = pl.BlockSpec((DSWA_HG, ATTN_BLOCK, span), lambda b, r, n: (0, 0, 0))
    o_spec = pl.BlockSpec(blk, lambda b, r, n: (b, n, r))
    return pl.pallas_call(
        functools.partial(_dswa_kernel, blocks=nb),
        grid=(batch, dil, nblk // nb),
        in_specs=[q_spec, kp_spec, kc_spec, vp_spec, vc_spec, bias_spec],
        out_specs=[o_spec, o_spec],
        out_shape=[jax.ShapeDtypeStruct((batch, length, dil * DSWA_OUT), F32)] * 2,
        compiler_params=pltpu.CompilerParams(dimension_semantics=("arbitrary",) * 3),
        name=f"dilated_attention_{g}",
    )(qkv3, qkv3, qkv3, qkv3, qkv3, bias)


def _fox_kernel(q_ref, k_ref, lf_ref, vt_ref, put_ref, o_ref, kaug_s, carry_s):
    qi = pl.program_id(1)
    tq = TQ_FOX
    krow = lax.broadcasted_iota(jnp.int32, (tq, tq), 0)
    qcol = lax.broadcasted_iota(jnp.int32, (tq, tq), 1)
    causal = krow <= qcol

    @pl.when(qi == 0)
    def _():
        carry_s[...] = jnp.zeros_like(carry_s)

    tri = jnp.where(krow >= qcol, 1.0, 0.0).astype(BF16)
    lane = lax.broadcasted_iota(jnp.int32, (tq, LANES), 1)

    def pack3(v):
        hi, mid, lo = _split3(v)
        packed = jnp.where(lane < FOX_HEADS, hi,
                           jnp.where(lane < 2 * FOX_HEADS, pltpu.roll(mid, FOX_HEADS, axis=1),
                                     jnp.where(lane < 3 * FOX_HEADS, pltpu.roll(lo, 2 * FOX_HEADS, axis=1), 0.0)))
        return packed.astype(BF16)

    sums = _dot(tri, pack3(lf_ref[...]))
    c = (carry_s[...] + sums + pltpu.roll(sums, LANES - FOX_HEADS, axis=1)
         + pltpu.roll(sums, LANES - 2 * FOX_HEADS, axis=1))
    carry_s[...] = c[tq - 1:tq, :]
    bias = _dot(pack3(c), put_ref[...])
    q_all = q_ref[...] + bias[:, :FOX_HEADS * LANES].astype(BF16)
    kaug_s[pl.ds(pl.multiple_of(qi * tq, tq), tq), :] = k_ref[...] + bias[:, FOX_HEADS * LANES:].astype(BF16)

    for hg in range(FOX_HEADS // FOX_HEADS_PER_LOOP):
        heads = [hg * FOX_HEADS_PER_LOOP + i for i in range(FOX_HEADS_PER_LOOP)]
        qs = [q_all[:, h * LANES:(h + 1) * LANES] for h in heads]

        def step(kb, carry, masked):
            ks = pl.multiple_of(kb * tq, tq)
            idx = range(len(heads))
            sts = []
            for i, h in enumerate(heads):
                st = _dot_nt(kaug_s[pl.ds(ks, tq), h * LANES:(h + 1) * LANES], qs[i])
                sts.append(jnp.where(causal, st, NEG_INF) if masked else st)
            m_new = [jnp.maximum(carry[i][0], jnp.max(sts[i], axis=0, keepdims=True)) for i in idx]
            alpha = [jnp.exp2(carry[i][0] - m_new[i]) for i in idx]
            ps = [jnp.exp2(sts[i] - m_new[i]) for i in idx]
            ls = [alpha[i] * carry[i][1] + jnp.sum(ps[i], axis=0, keepdims=True) for i in idx]
            out = []
            for i, h in enumerate(heads):
                vt = vt_ref[0, h * HEAD_DIM:(h + 1) * HEAD_DIM, pl.ds(ks, tq)]
                out.append((m_new[i], ls[i], alpha[i] * carry[i][2] + _dot(vt, ps[i].astype(BF16))))
            return tuple(out)

        init = tuple((jnp.full((1, tq), NEG_INF, F32), jnp.zeros((1, tq), F32), jnp.zeros((HEAD_DIM, tq), F32))
                     for _ in heads)
        carry = lax.fori_loop(0, qi, lambda kb, c: step(kb, c, False), init)
        carry = step(qi, carry, True)
        for i, h in enumerate(heads):
            _, l, acc = carry[i]
            o_ref[0, h * HEAD_DIM:(h + 1) * HEAD_DIM, :] = (acc / l).astype(BF16)


def _top16(v):
    bits = lax.bitcast_convert_type(v, jnp.uint32) & jnp.uint32(0xFFFF0000)
    return lax.bitcast_convert_type(bits, F32)


def _split3(c):
    hi = _top16(c)
    r1 = c - hi
    mid = _top16(r1)
    return hi, mid, r1 - mid


def _forgetting_attention(qb, kb, lf, vt, batch, seq):
    nq = seq // TQ_FOX
    put = np.zeros((LANES, 2 * FOX_HEADS * LANES), np.float32)
    for h in range(FOX_HEADS):
        for j in range(3):
            put[j * FOX_HEADS + h, h * LANES + FOX_Q_C[j]] = 1.0
            put[j * FOX_HEADS + h, FOX_HEADS * LANES + h * LANES + FOX_K_C[j]] = -1.0
    tile = lambda w: pl.BlockSpec((TQ_FOX, w), lambda b, i: (b * nq + i, 0))
    return pl.pallas_call(
        _fox_kernel,
        grid=(batch, nq),
        in_specs=[tile(FOX_HEADS * LANES), tile(FOX_HEADS * LANES), tile(LANES),
                  pl.BlockSpec((1, FOX_WIDTH, seq), lambda b, i: (b, 0, 0)),
                  pl.BlockSpec(put.shape, lambda b, i: (0, 0), pipeline_mode=pl.Buffered(1))],
        out_specs=pl.BlockSpec((1, FOX_WIDTH, TQ_FOX), lambda b, i: (b, 0, i)),
        out_shape=jax.ShapeDtypeStruct((batch, FOX_WIDTH, seq), BF16),
        scratch_shapes=[pltpu.VMEM((seq, FOX_HEADS * LANES), BF16), pltpu.VMEM((1, LANES), F32)],
        compiler_params=pltpu.CompilerParams(dimension_semantics=("arbitrary", "arbitrary"),
                                             vmem_limit_bytes=VMEM_LIMIT),
        name="forgetting_attention",
    )(qb, kb, lf, vt, jnp.asarray(put, BF16))


def _merge_kernel(x_ref, oa0_ref, oa1_ref, oa2_ref, ls0_ref, ls1_ref, ls2_ref, obt_ref, gate_ref,
                  wa_ref, wb_ref, wo_ref, g2_ref, wrh_ref, wrl_ref, br_ref,
                  x1_ref, h2_ref, ri_ref, rg_ref, cnt_ref, run_s, stage_s):
    tm = TM_MERGE

    @pl.when(pl.program_id(0) == 0)
    def _():
        run_s[...] = jnp.zeros_like(run_s)

    def token_order(ref, g, stage):
        dil = DSWA_PATTERNS[g][1]
        if dil == 1:
            return ref[0]
        for r in range(dil):
            for half in range(2):
                col = r * DSWA_OUT + half * LANES
                stage[half, pl.ds(r, tm // dil, stride=dil), :] = ref[0, :, col:col + LANES]
        return jnp.concatenate([stage[0], stage[1]], axis=1)

    l0, l1, l2 = (token_order(ref, g, stage_s.at[g]) for g, ref in enumerate((ls0_ref, ls1_ref, ls2_ref)))
    lm = jnp.maximum(jnp.maximum(l0, l1), l2)
    e0, e1, e2 = jnp.exp(l0 - lm), jnp.exp(l1 - lm), jnp.exp(l2 - lm)
    o0, o1, o2 = (token_order(ref, g, stage_s.at[DSWA_GROUPS + g])
                  for g, ref in enumerate((oa0_ref, oa1_ref, oa2_ref)))
    o_a = (e0 * o0 + e1 * o1 + e2 * o2) / (e0 + e1 + e2)
    ya = _dot(o_a.astype(BF16), wa_ref[...])
    yb = lax.dot_general(obt_ref[0], wb_ref[...], (((0,), (0,)), ((), ())), preferred_element_type=F32)
    merged = gate_ref[:, :D_MODEL] * ya + gate_ref[:, D_MODEL:] * yb
    x1 = x_ref[...] + _dot(merged.astype(BF16), wo_ref[...])
    x1_ref[...] = x1

    ms = jnp.mean(x1 * x1, axis=-1, keepdims=True)
    h2 = x1 * lax.rsqrt(ms + RMS_EPS) * g2_ref[...]
    for c in range(ROW_CHUNKS):
        h2_ref[pl.ds(c, tm, stride=ROW_CHUNKS), :] = h2[:, c * LANES:(c + 1) * LANES]

    hi = h2.astype(BF16)
    lo = (h2 - hi.astype(F32)).astype(BF16)
    logits = _dot(hi, wrh_ref[...]) + _dot(lo, wrh_ref[...]) + _dot(hi, wrl_ref[...]) + br_ref[...]

    lane = lax.broadcasted_iota(jnp.int32, (tm, LANES), 1).astype(F32)
    work = logits
    top_v, top_i = [], []
    for _ in range(TOP_K):
        mk = jnp.max(work, axis=-1, keepdims=True)
        ik = jnp.min(jnp.where(work == mk, lane, float(LANES)), axis=-1, keepdims=True)
        top_v.append(mk)
        top_i.append(ik)
        work = jnp.where(lane == ik, -jnp.inf, work)
    ex = [jnp.exp(v - top_v[0]) for v in top_v]
    den = ex[0] + ex[1] + ex[2] + ex[3]

    onehot = jnp.zeros((tm, LANES), F32)
    for ik in top_i:
        onehot = onehot + jnp.where(lane == ik, 1.0, 0.0)
    r_i = lax.broadcasted_iota(jnp.int32, (tm, tm), 0)
    c_i = lax.broadcasted_iota(jnp.int32, (tm, tm), 1)
    tri = jnp.where(c_i < r_i, 1.0, 0.0).astype(BF16)
    before = _dot(tri, onehot.astype(BF16)) + run_s[...]
    ri = jnp.zeros((tm, LANES), jnp.int32)
    rg = jnp.zeros((tm, LANES), F32)
    for k in range(TOP_K):
        rank = jnp.sum(jnp.where(lane == top_i[k], before, 0.0), axis=-1, keepdims=True)
        ri = jnp.where(lane == k, top_i[k].astype(jnp.int32), ri)
        ri = jnp.where(lane == TOP_K + k, rank.astype(jnp.int32), ri)
        rg = jnp.where(lane == k, ex[k] / den, rg)
    ri_ref[...] = ri
    rg_ref[...] = rg
    run = run_s[...] + jnp.sum(onehot, axis=0, keepdims=True)
    run_s[...] = run
    cnt_ref[...] = run


def _merge_and_route(x2, batch, seq, oas, lss, obt, gate, w_branch_a, w_branch_b, w_out, ln2_g, w_router,
                     b_router):
    n = x2.shape[0]
    assert seq % TM_MERGE == 0 and all(TM_MERGE % (SUBLANES * d) == 0 for _, d in DSWA_PATTERNS)
    tiles_per_seq = seq // TM_MERGE
    seq_tile = lambda i: (i // tiles_per_seq, i % tiles_per_seq)
    group_specs = [pl.BlockSpec((1, TM_MERGE // d, d * DSWA_OUT), lambda i: (*seq_tile(i), 0))
                   for _, d in DSWA_PATTERNS]
    obt_spec = pl.BlockSpec((1, FOX_WIDTH, TM_MERGE), lambda i: (seq_tile(i)[0], 0, seq_tile(i)[1]))
    wr = jnp.pad(w_router, ((0, 0), (0, LANES - N_EXPERTS)))
    wr_top = _top16(wr)
    wr_hi = wr_top.astype(BF16)
    wr_lo = (wr - wr_top).astype(BF16)
    br = jnp.concatenate([b_router, jnp.full((LANES - N_EXPERTS,), NEG_INF, F32)]).reshape(1, LANES)
    const = lambda shape: pl.BlockSpec(shape, lambda i: (0,) * len(shape), pipeline_mode=pl.Buffered(1))
    row = lambda w: pl.BlockSpec((TM_MERGE, w), lambda i: (i, 0))
    return pl.pallas_call(
        _merge_kernel,
        grid=(n // TM_MERGE,),
        in_specs=[row(D_MODEL)] + group_specs + group_specs + [obt_spec, row(2 * D_MODEL),
                  const((DSWA_OUT, D_MODEL)), const((FOX_WIDTH, D_MODEL)), const((D_MODEL, D_MODEL)),
                  const((1, D_MODEL)), const((D_MODEL, LANES)), const((D_MODEL, LANES)), const((1, LANES))],
        out_specs=[row(D_MODEL), pl.BlockSpec((TM_MERGE * ROW_CHUNKS, LANES), lambda i: (i, 0)),
                   row(LANES), row(LANES), pl.BlockSpec((1, LANES), lambda i: (0, 0))],
        out_shape=[jax.ShapeDtypeStruct((n, D_MODEL), F32),
                   jax.ShapeDtypeStruct((n * ROW_CHUNKS, LANES), F32),
                   jax.ShapeDtypeStruct((n, LANES), jnp.int32),
                   jax.ShapeDtypeStruct((n, LANES), F32),
                   jax.ShapeDtypeStruct((1, LANES), F32)],
        scratch_shapes=[pltpu.VMEM((1, LANES), F32), pltpu.VMEM((2 * DSWA_GROUPS, 2, TM_MERGE, LANES), F32)],
        compiler_params=pltpu.CompilerParams(dimension_semantics=("arbitrary",), vmem_limit_bytes=VMEM_LIMIT),
        name="merge_and_route",
    )(x2, *oas, *lss, obt, gate, w_branch_a.astype(BF16), w_branch_b.astype(BF16), w_out.astype(BF16),
      ln2_g.reshape(1, D_MODEL), wr_hi, wr_lo, br)


def _row_copy(src, src_row, dst, dst_row, sem):
    return pltpu.make_async_copy(src.at[pl.ds(pl.multiple_of(src_row * ROW_CHUNKS, ROW_CHUNKS), ROW_CHUNKS)],
                                 dst.at[pl.ds(pl.multiple_of(dst_row * ROW_CHUNKS, ROW_CHUNKS), ROW_CHUNKS)], sem)


def _dispatch_kernel(zb_ref, zf_ref, dest_hbm, h2_ref, xs_hbm, idx_s, zeros_v, sem_i, sem_z, sem_r):
    i = pl.program_id(0)
    n_move = TOP_K * TC_MOVE
    idx_copy = pltpu.make_async_copy(dest_hbm.at[i], idx_s, sem_i)
    idx_copy.start()

    def zero_copy(e):
        blk = pl.multiple_of(zb_ref[e] * (T_EXPERT * ROW_CHUNKS), T_EXPERT * ROW_CHUNKS)
        return pltpu.make_async_copy(zeros_v, xs_hbm.at[pl.ds(blk, T_EXPERT * ROW_CHUNKS)], sem_z)

    @pl.when(i == 0)
    def _():
        zeros_v[...] = jnp.zeros_like(zeros_v)
        for e in range(N_EXPERTS):
            @pl.when(zf_ref[e] > 0)
            def _():
                zero_copy(e).start()
        for e in range(N_EXPERTS):
            @pl.when(zf_ref[e] > 0)
            def _():
                zero_copy(e).wait()

    idx_copy.wait()

    for k in range(TOP_K):
        def issue(to, carry, k=k):
            t0 = to * DMA_UNROLL
            for u in range(DMA_UNROLL):
                _row_copy(h2_ref, t0 + u, xs_hbm, idx_s[k * TC_MOVE + t0 + u], sem_r).start(priority=u % 2)
            return carry

        lax.fori_loop(0, TC_MOVE // DMA_UNROLL, issue, 0)

    def drain(jo, carry):
        for u in range(DMA_UNROLL):
            _row_copy(h2_ref, 0, xs_hbm, 0, sem_r).wait()
        return carry

    lax.fori_loop(0, n_move // DMA_UNROLL, drain, 0)


def _dispatch(dest_tiles, zero_blk, zero_flag, h2_rows, n_slots):
    nt = dest_tiles.shape[0]
    return pl.pallas_call(
        _dispatch_kernel,
        grid_spec=pltpu.PrefetchScalarGridSpec(
            num_scalar_prefetch=2, grid=(nt,),
            in_specs=[pl.BlockSpec(memory_space=pl.ANY),
                      pl.BlockSpec((TC_MOVE * ROW_CHUNKS, LANES), lambda i, zb, zf: (i, 0))],
            out_specs=pl.BlockSpec(memory_space=pl.ANY),
            scratch_shapes=[pltpu.SMEM((TOP_K * TC_MOVE,), jnp.int32),
                            pltpu.VMEM((T_EXPERT * ROW_CHUNKS, LANES), F32),
                            pltpu.SemaphoreType.DMA, pltpu.SemaphoreType.DMA, pltpu.SemaphoreType.DMA]),
        out_shape=jax.ShapeDtypeStruct((n_slots * ROW_CHUNKS, LANES), F32),
        compiler_params=pltpu.CompilerParams(dimension_semantics=("arbitrary",)),
        name="dispatch_rows",
    )(zero_blk, zero_flag, dest_tiles, h2_rows)


def _expert_kernel(be_ref, nu_ref, xs_ref, wgu_ref, bgu_ref, wd_ref, bd_ref, ys_ref, wgu_s, wd_s):
    j = pl.program_id(0)
    t = T_EXPERT

    @pl.when(j < nu_ref[0])
    def _():
        prev = be_ref[jnp.maximum(j - 1, 0)]

        @pl.when((j == 0) | (be_ref[j] != prev))
        def _():
            wgu_s[...] = wgu_ref[0].astype(BF16)
            wd_s[...] = wd_ref[0].astype(BF16)

        x = jnp.concatenate([xs_ref[pl.ds(c, t, stride=ROW_CHUNKS), :] for c in range(ROW_CHUNKS)], axis=1)
        gu = _dot(x.astype(BF16), wgu_s[...]) + bgu_ref[0]
        g = jnp.minimum(gu[:, :D_MODEL], SWIGLU_LIMIT)
        u = jnp.clip(gu[:, D_MODEL:], -SWIGLU_LIMIT, SWIGLU_LIMIT)
        mid = (u + 1.0) * (g * jax.nn.sigmoid(SWIGLU_ALPHA * g))
        y = _dot(mid.astype(BF16), wd_s[...]) + bd_ref[0]
        for c in range(ROW_CHUNKS):
            ys_ref[pl.ds(c, t, stride=ROW_CHUNKS), :] = y[:, c * LANES:(c + 1) * LANES]


def _expert_ffn(blk_e, n_used, xs_rows, w_gate_up, b_gate_up, w_down, b_down):
    n_blk = blk_e.shape[0]
    rows = T_EXPERT * ROW_CHUNKS
    live = lambda j, be, nu: jnp.minimum(j, nu[0] - 1)
    sink = lambda j, be, nu: jnp.where(j < nu[0], j, n_blk - 1)
    return pl.pallas_call(
        _expert_kernel,
        grid_spec=pltpu.PrefetchScalarGridSpec(
            num_scalar_prefetch=2, grid=(n_blk,),
            in_specs=[pl.BlockSpec((rows, LANES), lambda j, be, nu: (live(j, be, nu), 0)),
                      pl.BlockSpec((1, D_MODEL, 2 * D_MODEL), lambda j, be, nu: (be[live(j, be, nu)], 0, 0)),
                      pl.BlockSpec((1, 1, 2 * D_MODEL), lambda j, be, nu: (be[live(j, be, nu)], 0, 0)),
                      pl.BlockSpec((1, D_MODEL, D_MODEL), lambda j, be, nu: (be[live(j, be, nu)], 0, 0)),
                      pl.BlockSpec((1, 1, D_MODEL), lambda j, be, nu: (be[live(j, be, nu)], 0, 0))],
            out_specs=pl.BlockSpec((rows, LANES), lambda j, be, nu: (sink(j, be, nu), 0)),
            scratch_shapes=[pltpu.VMEM((D_MODEL, 2 * D_MODEL), BF16), pltpu.VMEM((D_MODEL, D_MODEL), BF16)]),
        out_shape=jax.ShapeDtypeStruct(xs_rows.shape, F32),
        compiler_params=pltpu.CompilerParams(dimension_semantics=("arbitrary",), vmem_limit_bytes=VMEM_LIMIT),
        name="expert_ffn",
    )(blk_e, n_used, xs_rows, w_gate_up, b_gate_up.reshape(N_EXPERTS, 1, 2 * D_MODEL),
      w_down, b_down.reshape(N_EXPERTS, 1, D_MODEL))


def _combine_kernel(dest_hbm, ys_hbm, x1_ref, rg_ref, o_ref, idx_s, buf, gate_s, sem_i, sem_r):
    i = pl.program_id(0)
    n_move = TOP_K * TC_MOVE
    idx_copy = pltpu.make_async_copy(dest_hbm.at[i], idx_s, sem_i)
    idx_copy.start()
    idx_copy.wait()

    def issue(jo, carry):
        for u in range(DMA_UNROLL):
            j = jo * DMA_UNROLL + u
            _row_copy(ys_hbm, idx_s[j], buf, j, sem_r).start(priority=u % 2)
        return carry

    lax.fori_loop(0, n_move // DMA_UNROLL, issue, 0)

    def drain(jo, carry):
        for u in range(DMA_UNROLL):
            _row_copy(ys_hbm, 0, buf, 0, sem_r).wait()
        return carry

    lax.fori_loop(0, n_move // DMA_UNROLL, drain, 0)

    for k in range(TOP_K):
        gate_s[k] = jnp.broadcast_to(rg_ref[:, k:k + 1], (TC_MOVE, LANES))
    for c in range(ROW_CHUNKS):
        acc = x1_ref[:, c * LANES:(c + 1) * LANES]
        for k in range(TOP_K):
            rows = buf[pl.ds(k * TC_MOVE * ROW_CHUNKS + c, TC_MOVE, stride=ROW_CHUNKS), :]
            acc = acc + gate_s[k] * rows
        o_ref[:, c * LANES:(c + 1) * LANES] = acc


def _combine(dest_tiles, ys_rows, x1, rg):
    n = x1.shape[0]
    nt = dest_tiles.shape[0]
    return pl.pallas_call(
        _combine_kernel,
        grid=(nt,),
        in_specs=[pl.BlockSpec(memory_space=pl.ANY), pl.BlockSpec(memory_space=pl.ANY),
                  pl.BlockSpec((TC_MOVE, D_MODEL), lambda i: (i, 0)),
                  pl.BlockSpec((TC_MOVE, LANES), lambda i: (i, 0))],
        out_specs=pl.BlockSpec((TC_MOVE, D_MODEL), lambda i: (i, 0)),
        out_shape=jax.ShapeDtypeStruct((n, D_MODEL), F32),
        scratch_shapes=[pltpu.SMEM((TOP_K * TC_MOVE,), jnp.int32),
                        pltpu.VMEM((TOP_K * TC_MOVE * ROW_CHUNKS, LANES), F32),
                        pltpu.VMEM((TOP_K, TC_MOVE, LANES), F32),
                        pltpu.SemaphoreType.DMA, pltpu.SemaphoreType.DMA],
        compiler_params=pltpu.CompilerParams(dimension_semantics=("arbitrary",), vmem_limit_bytes=VMEM_LIMIT),
        name="combine_rows",
    )(dest_tiles, ys_rows, x1, rg)


def _layer(x, ln1_g, w_in, b_forget, b_gate, q_norm_a, k_norm_a, q_norm_b, k_norm_b, rel_bias,
           w_branch_a, w_branch_b, w_out, ln2_g, w_router, b_router, w_gate_up, b_gate_up, w_down, b_down):
    batch, seq, _ = x.shape
    n = batch * seq
    x2 = x.reshape(n, D_MODEL)
    *qkv_groups, qb, kb, vt, lf, gate = _in_projection(x2, batch, seq, ln1_g, w_in, b_forget, b_gate,
                                                       q_norm_a, k_norm_a, q_norm_b, k_norm_b)

    oas, lss = [], []
    for g in range(DSWA_GROUPS):
        o, lse = _dilated_attention(qkv_groups[g], rel_bias, g, batch, seq)
        oas.append(o)
        lss.append(lse)

    obt = _forgetting_attention(qb, kb, lf, vt, batch, seq)

    x1, h2_rows, ri, rg, cnt = _merge_and_route(x2, batch, seq, oas, lss, obt, gate, w_branch_a, w_branch_b,
                                                w_out, ln2_g, w_router, b_router)

    counts = cnt[0, :N_EXPERTS].astype(jnp.int32)
    pcounts = ((counts + T_EXPERT - 1) // T_EXPERT) * T_EXPERT
    pends = jnp.cumsum(pcounts)
    pstarts = pends - pcounts
    idx, rank = ri[:, :TOP_K], ri[:, TOP_K:2 * TOP_K]
    start_of = jnp.sum(jnp.where(idx[..., None] == jnp.arange(N_EXPERTS), pstarts, 0), axis=-1)
    dest = start_of + rank
    nt = n // TC_MOVE
    dest_tiles = jnp.transpose(dest.reshape(nt, TC_MOVE, TOP_K), (0, 2, 1)).reshape(nt, TOP_K * TC_MOVE)
    n_slots = n * TOP_K + N_EXPERTS * T_EXPERT
    n_blk = n_slots // T_EXPERT
    blk_first = jnp.arange(n_blk, dtype=jnp.int32) * T_EXPERT
    blk_e = jnp.minimum(jnp.sum((pends[None, :] <= blk_first[:, None]).astype(jnp.int32), axis=1), N_EXPERTS - 1)
    n_used = (pends[-1:] // T_EXPERT).astype(jnp.int32)
    zero_blk = jnp.maximum(pends // T_EXPERT - 1, 0).astype(jnp.int32)
    zero_flag = (pcounts > 0).astype(jnp.int32)

    xs_rows = _dispatch(dest_tiles, zero_blk, zero_flag, h2_rows, n_slots)
    ys_rows = _expert_ffn(blk_e, n_used, xs_rows, w_gate_up, b_gate_up, w_down, b_down)
    out = _combine(dest_tiles, ys_rows, x1, rg)
    return out.reshape(batch, seq, D_MODEL)


def kernel(x, ln1_g, w_in, b_forget, b_gate, q_norm_a, k_norm_a, q_norm_b, k_norm_b, rel_bias, w_branch_a, w_branch_b, w_out, ln2_g, w_router, b_router, w_gate_up, b_gate_up, w_down, b_down):
    for layer in range(ln1_g.shape[0]):
        x = _layer(x, ln1_g[layer], w_in[layer], b_forget[layer], b_gate[layer], q_norm_a[layer],
                   k_norm_a[layer], q_norm_b[layer], k_norm_b[layer], rel_bias, w_branch_a[layer],
                   w_branch_b[layer], w_out[layer], ln2_g[layer], w_router[layer], b_router[layer],
                   w_gate_up[layer], b_gate_up[layer], w_down[layer], b_down[layer])
    return x
```

```python
import functools
import math

import numpy as np
import jax
import jax.numpy as jnp
from jax import lax
from jax.experimental import pallas as pl
from jax.experimental.pallas import tpu as pltpu

F32 = jnp.float32
BF16 = jnp.bfloat16

D_MODEL = 1024
HEAD_DIM = 64
DSWA_PATTERNS = ((128, 1), (512, 4), (2048, 16))
DSWA_HG = 4
DSWA_GROUPS = len(DSWA_PATTERNS)
DSWA_WIDTH = DSWA_GROUPS * DSWA_HG * HEAD_DIM
DSWA_OUT = DSWA_HG * HEAD_DIM
FOX_HEADS = 8
FOX_WIDTH = FOX_HEADS * HEAD_DIM
ATTN_BLOCK = 128
NUM_BUCKETS = 32
MAX_DISTANCE = 2048
N_EXPERTS = 32
TOP_K = 4
SWIGLU_LIMIT = 7.0
SWIGLU_ALPHA = 1.702
RMS_EPS = 1e-6
NEG_INF = -1e30
LOG2E = 1.4426950408889634

LANES = 128
SUBLANES = 8
ROW_CHUNKS = D_MODEL // LANES
QKV_WIDTH = 3 * DSWA_WIDTH + 3 * FOX_WIDTH
FOX_Q_ONES = (64, 65, 66)
FOX_K_C = (64, 65, 66)
FOX_Q_C = (67, 68, 69)
FOX_K_ONES = (67, 68, 69)
VMEM_LIMIT = 56 * 1024 * 1024

TM_PROJ = 512
PROJ_LOOKAHEAD = 1
TQ_FOX = 512
FOX_HEADS_PER_LOOP = 8
DSWA_BLOCKS_PER_STEP = 4
TM_MERGE = 512
MERGE_PARTS = 2
T_EXPERT = 512
TC_MOVE = 1024
TC_COMB = 512
DMA_UNROLL = 16


def _dot(a, b):
    return jnp.dot(a, b, preferred_element_type=F32)


def _dot_nt(a, b):
    return lax.dot_general(a, b, (((1,), (1,)), ((), ())), preferred_element_type=F32)


def _inproj_kernel(x_ref, g1_ref, wq_ref, wf_ref, wg_ref, bf_ref, bg_ref, nrm_ref, e_ref, place_ref, pat_ref,
                   qkv0_ref, qkv1_ref, qkv2_ref, qb_ref, kb_ref, vt_ref, lf_ref, gate_ref, stage_ref):
    group_refs = (qkv0_ref, qkv1_ref, qkv2_ref)
    x = x_ref[...]
    ms = jnp.mean(x * x, axis=-1, keepdims=True)
    h = (x * lax.rsqrt(ms + RMS_EPS) * g1_ref[...]).astype(BF16)
    ones_bd = e_ref[...]

    def head_norm(y, gn):
        ss = _dot((y * y).astype(BF16), ones_bd)
        return y * lax.rsqrt(ss * (1.0 / HEAD_DIM) + RMS_EPS) * gn

    n_a = 3 * DSWA_WIDTH // 256

    def qkv_epilogue(c, y):
        if c < 3:
            y = head_norm(y, nrm_ref[0:1, :])
        elif c < 6:
            y = head_norm(y, nrm_ref[1:2, :])
        elif 9 <= c < 11:
            y = head_norm(y, nrm_ref[2:3, :])
        elif 11 <= c < 13:
            y = head_norm(y, nrm_ref[3:4, :])
        if c < n_a:
            g, part = c % DSWA_GROUPS, c // DSWA_GROUPS
            dil = DSWA_PATTERNS[g][1]
            if dil == 1:
                group_refs[g][0, :, part * 256:(part + 1) * 256] = y.astype(BF16)
            else:
                for half in range(2):
                    stage_ref[half] = y[:, half * LANES:(half + 1) * LANES]
                for r in range(dil):
                    for half in range(2):
                        col = r * DSWA_WIDTH + part * 256 + half * LANES
                        rows = stage_ref[half, pl.ds(r, TM_PROJ // dil, stride=dil), :]
                        group_refs[g][0, :, col:col + LANES] = rows.astype(BF16)
        elif c < n_a + 4:
            is_k, half = divmod(c - n_a, 2)
            slots = _dot(y.astype(BF16), place_ref[...]) + pat_ref[is_k:is_k + 1, :]
            (kb_ref if is_k else qb_ref)[:, half * 512:(half + 1) * 512] = slots.astype(BF16)
        else:
            half = c - n_a - 4
            vt_ref[0, half * 256:(half + 1) * 256, :] = y.T.astype(BF16)

    def forget_epilogue(y):
        z = y + bf_ref[...]
        lf_ref[...] = (jnp.minimum(z, 0.0) - jnp.log1p(jnp.exp(-jnp.abs(z)))) * LOG2E

    def gate_epilogue(c, y):
        gate_ref[:, c * 256:(c + 1) * 256] = jax.nn.sigmoid(y + bg_ref[:, c * 256:(c + 1) * 256])

    jobs = [(functools.partial(lambda c: _dot(h, wq_ref[:, c * 256:(c + 1) * 256]), c),
             functools.partial(qkv_epilogue, c)) for c in range(QKV_WIDTH // 256)]
    jobs.append((lambda: _dot(h, wf_ref[...]), forget_epilogue))
    jobs += [(functools.partial(lambda c: _dot(h, wg_ref[:, c * 256:(c + 1) * 256]), c),
              functools.partial(gate_epilogue, c)) for c in range(2 * D_MODEL // 256)]
    pending = [matmul() for matmul, _ in jobs[:PROJ_LOOKAHEAD]]
    for j, (_, epilogue) in enumerate(jobs):
        if j + PROJ_LOOKAHEAD < len(jobs):
            pending.append(jobs[j + PROJ_LOOKAHEAD][0]())
        epilogue(pending.pop(0))


def _in_projection(x2, batch, seq, ln1_g, w_in, b_forget, b_gate, q_norm_a, k_norm_a, q_norm_b, k_norm_b):
    n = x2.shape[0]
    assert seq % TM_PROJ == 0 and all(TM_PROJ % (16 * d) == 0 for _, d in DSWA_PATTERNS)
    tiles_per_seq = seq // TM_PROJ
    wq = w_in[:, :QKV_WIDTH].astype(BF16)
    wf = jnp.pad(w_in[:, QKV_WIDTH:QKV_WIDTH + FOX_HEADS], ((0, 0), (0, LANES - FOX_HEADS))).astype(BF16)
    wg = w_in[:, QKV_WIDTH + FOX_HEADS:].astype(BF16)
    bf = jnp.pad(b_forget, (0, LANES - FOX_HEADS)).reshape(1, LANES)
    bg = b_gate.reshape(1, 2 * D_MODEL)
    scale = HEAD_DIM ** -0.5
    nrm = jnp.stack([jnp.tile(q_norm_a, 4) * scale, jnp.tile(k_norm_a, 4),
                     jnp.tile(q_norm_b, 4) * (scale * LOG2E), jnp.tile(k_norm_b, 4)])
    head_id = np.arange(256) // HEAD_DIM
    ones_bd = jnp.asarray(head_id[:, None] == head_id[None, :], BF16)
    src = np.arange(256)
    place = np.zeros((256, 4 * LANES), np.float32)
    place[src, (src // HEAD_DIM) * LANES + src % HEAD_DIM] = 1.0
    pat = np.zeros((2, 4 * LANES), np.float32)
    for slot in range(4):
        pat[0, [slot * LANES + l for l in FOX_Q_ONES]] = 1.0
        pat[1, [slot * LANES + l for l in FOX_K_ONES]] = 1.0
    const = lambda shape: pl.BlockSpec(shape, lambda i: (0,) * len(shape), pipeline_mode=pl.Buffered(1))
    row = lambda w: pl.BlockSpec((TM_PROJ, w), lambda i: (i, 0))
    seq_tile = lambda i: (i // tiles_per_seq, i % tiles_per_seq)
    group_spec = lambda d: pl.BlockSpec((1, TM_PROJ // d, d * DSWA_WIDTH), lambda i: (*seq_tile(i), 0))
    group_shape = lambda d: jax.ShapeDtypeStruct((batch, seq // d, d * DSWA_WIDTH), BF16)
    vt_spec = pl.BlockSpec((1, FOX_WIDTH, TM_PROJ), lambda i: (seq_tile(i)[0], 0, seq_tile(i)[1]))
    dils = [d for _, d in DSWA_PATTERNS]
    return pl.pallas_call(
        _inproj_kernel,
        grid=(n // TM_PROJ,),
        in_specs=[row(D_MODEL), const((1, D_MODEL)), const((D_MODEL, QKV_WIDTH)), const((D_MODEL, LANES)),
                  const((D_MODEL, 2 * D_MODEL)), const((1, LANES)), const((1, 2 * D_MODEL)),
                  const((4, 256)), const((256, 256)), const((256, 4 * LANES)), const((2, 4 * LANES))],
        out_specs=[group_spec(d) for d in dils] + [row(FOX_HEADS * LANES), row(FOX_HEADS * LANES), vt_spec,
                                                   row(LANES), row(2 * D_MODEL)],
        out_shape=[group_shape(d) for d in dils] + [jax.ShapeDtypeStruct((n, FOX_HEADS * LANES), BF16),
                                                    jax.ShapeDtypeStruct((n, FOX_HEADS * LANES), BF16),
                                                    jax.ShapeDtypeStruct((batch, FOX_WIDTH, seq), BF16),
                                                    jax.ShapeDtypeStruct((n, LANES), F32),
                                                    jax.ShapeDtypeStruct((n, 2 * D_MODEL), F32)],
        scratch_shapes=[pltpu.VMEM((2, TM_PROJ, LANES), F32)],
        compiler_params=pltpu.CompilerParams(dimension_semantics=("arbitrary",), vmem_limit_bytes=VMEM_LIMIT),
        name="in_projection",
    )(x2, ln1_g.reshape(1, D_MODEL), wq, wf, wg, bf, bg, nrm, ones_bd, jnp.asarray(place, BF16), jnp.asarray(pat))


def _t5_bucket(n):
    max_exact = NUM_BUCKETS // 2
    n_safe = np.maximum(n, 1).astype(np.float64)
    large = max_exact + (np.log(n_safe / max_exact) / np.log(MAX_DISTANCE / max_exact)
                         * (NUM_BUCKETS - max_exact)).astype(np.int64)
    large = np.minimum(large, NUM_BUCKETS - 1)
    return np.where(n < max_exact, n, large).astype(np.int32)


def _dswa_kernel(q_ref, kp_ref, kc_ref, vp_ref, vc_ref, bias_ref, o_ref, l_ref, *, blocks):
    first = pl.program_id(2) == 0
    items = [(j, h) for j in range(blocks) for h in range(DSWA_HG)]
    rows = lambda j: slice(j * ATTN_BLOCK, (j + 1) * ATTN_BLOCK)
    head = lambda h: slice(h * HEAD_DIM, (h + 1) * HEAD_DIM)

    sps, scs = [], []
    for j, h in items:
        q = q_ref[0, rows(j), head(h)]
        k_prev = kp_ref[0, :, head(h)] if j == 0 else kc_ref[0, rows(j - 1), head(h)]
        sp = _dot_nt(q, k_prev) + bias_ref[h, :, :ATTN_BLOCK]
        if j == 0:
            sp = jnp.where(first, NEG_INF, sp)
        sps.append(sp)
        scs.append(_dot_nt(q, kc_ref[0, rows(j), head(h)]) + bias_ref[h, :, ATTN_BLOCK:])
    ms = [jnp.maximum(jnp.max(sp, axis=-1, keepdims=True), jnp.max(sc, axis=-1, keepdims=True))
          for sp, sc in zip(sps, scs)]
    pps = [jnp.exp(sp - m) for sp, m in zip(sps, ms)]
    pcs = [jnp.exp(sc - m) for sc, m in zip(scs, ms)]
    dens = [jnp.sum(pp, axis=-1, keepdims=True) + jnp.sum(pc, axis=-1, keepdims=True) for pp, pc in zip(pps, pcs)]
    for (j, h), pp, pc, m, den in zip(items, pps, pcs, ms, dens):
        v_prev = vp_ref[0, :, head(h)] if j == 0 else vc_ref[0, rows(j - 1), head(h)]
        o = _dot(pp.astype(BF16), v_prev) + _dot(pc.astype(BF16), vc_ref[0, rows(j), head(h)])
        o_ref[0, rows(j), head(h)] = o / den
        l_ref[0, rows(j), head(h)] = jnp.broadcast_to(m + jnp.log(den), (ATTN_BLOCK, HEAD_DIM))


def _dilated_attention(qkv3, rel_bias, g, batch, seq):
    window, dil = DSWA_PATTERNS[g]
    steps = window // dil
    assert steps == ATTN_BLOCK
    length = seq // dil
    assert length % ATTN_BLOCK == 0
    nblk = length // ATTN_BLOCK
    span = 2 * ATTN_BLOCK
    back = np.arange(ATTN_BLOCK)[:, None] + steps - np.arange(span)[None, :]
    valid = (back >= 0) & (back <= steps)
    bucket = _t5_bucket(np.clip(back, 0, steps) * dil)
    cols = rel_bias[:, g * DSWA_HG:(g + 1) * DSWA_HG]
    onehot = np.asarray(bucket.reshape(-1)[:, None] == np.arange(NUM_BUCKETS)[None, :], np.float32)
    bias = jnp.dot(jnp.asarray(onehot), cols.astype(F32), precision=lax.Precision.HIGHEST)
    bias = jnp.transpose(bias.reshape(ATTN_BLOCK, span, DSWA_HG), (2, 0, 1))
    bias = jnp.where(valid[None], bias, NEG_INF)

    per_pos = DSWA_WIDTH // 256
    nb = math.gcd(nblk, DSWA_BLOCKS_PER_STEP)
    blk = (1, nb * ATTN_BLOCK, 256)
    one = (1, ATTN_BLOCK, 256)
    before = lambda n: jnp.maximum(n * nb - 1, 0)
    q_spec = pl.BlockSpec(blk, lambda b, r, n: (b, n, r * per_pos))
    kc_spec = pl.BlockSpec(blk, lambda b, r, n: (b, n, r * per_pos + 1))
    kp_spec = pl.BlockSpec(one, lambda b, r, n: (b, before(n), r * per_pos + 1))
    vc_spec = pl.BlockSpec(blk, lambda b, r, n: (b, n, r * per_pos + 2))
    vp_spec = pl.BlockSpec(one, lambda b, r, n: (b, before(n), r * per_pos + 2))
    bias_spec = pl.BlockSpec((DSWA_HG, ATTN_BLOCK, span), lambda b, r, n: (0, 0, 0))
    o_spec = pl.BlockSpec(blk, lambda b, r, n: (b, n, r))
    return pl.pallas_call(
        functools.partial(_dswa_kernel, blocks=nb),
        grid=(batch, dil, nblk // nb),
        in_specs=[q_spec, kp_spec, kc_spec, vp_spec, vc_spec, bias_spec],
        out_specs=[o_spec, o_spec],
        out_shape=[jax.ShapeDtypeStruct((batch, length, dil * DSWA_OUT), F32)] * 2,
        compiler_params=pltpu.CompilerParams(dimension_semantics=("arbitrary",) * 3),
        name=f"dilated_attention_{g}",
    )(qkv3, qkv3, qkv3, qkv3, qkv3, bias)


def _fox_kernel(q_ref, k_ref, lf_ref, vt_ref, put_ref, o_ref, kaug_s, carry_s):
    qi = pl.program_id(1)
    tq = TQ_FOX
    krow = lax.broadcasted_iota(jnp.int32, (tq, tq), 0)
    qcol = lax.broadcasted_iota(jnp.int32, (tq, tq), 1)
    causal = krow <= qcol

    @pl.when(qi == 0)
    def _():
        carry_s[...] = jnp.zeros_like(carry_s)

    tri = jnp.where(krow >= qcol, 1.0, 0.0).astype(BF16)
    lane = lax.broadcasted_iota(jnp.int32, (tq, LANES), 1)

    def pack3(v):
        hi, mid, lo = _split3(v)
        packed = jnp.where(lane < FOX_HEADS, hi,
                           jnp.where(lane < 2 * FOX_HEADS, pltpu.roll(mid, FOX_HEADS, axis=1),
                                     jnp.where(lane < 3 * FOX_HEADS, pltpu.roll(lo, 2 * FOX_HEADS, axis=1), 0.0)))
        return packed.astype(BF16)

    sums = _dot(tri, pack3(lf_ref[...]))
    c = (carry_s[...] + sums + pltpu.roll(sums, LANES - FOX_HEADS, axis=1)
         + pltpu.roll(sums, LANES - 2 * FOX_HEADS, axis=1))
    carry_s[...] = c[tq - 1:tq, :]
    bias = _dot(pack3(c), put_ref[...])
    q_all = q_ref[...] + bias[:, :FOX_HEADS * LANES].astype(BF16)
    kaug_s[pl.ds(pl.multiple_of(qi * tq, tq), tq), :] = k_ref[...] + bias[:, FOX_HEADS * LANES:].astype(BF16)

    for hg in range(FOX_HEADS // FOX_HEADS_PER_LOOP):
        heads = [hg * FOX_HEADS_PER_LOOP + i for i in range(FOX_HEADS_PER_LOOP)]
        qs = [q_all[:, h * LANES:(h + 1) * LANES] for h in heads]

        def step(kb, carry, masked):
            ks = pl.multiple_of(kb * tq, tq)
            idx = range(len(heads))
            sts = []
            for i, h in enumerate(heads):
                st = _dot_nt(kaug_s[pl.ds(ks, tq), h * LANES:(h + 1) * LANES], qs[i])
                sts.append(jnp.where(causal, st, NEG_INF) if masked else st)
            m_new = [jnp.maximum(carry[i][0], jnp.max(sts[i], axis=0, keepdims=True)) for i in idx]
            alpha = [jnp.exp2(carry[i][0] - m_new[i]) for i in idx]
            ps = [jnp.exp2(sts[i] - m_new[i]) for i in idx]
            ls = [alpha[i] * carry[i][1] + jnp.sum(ps[i], axis=0, keepdims=True) for i in idx]
            out = []
            for i, h in enumerate(heads):
                vt = vt_ref[0, h * HEAD_DIM:(h + 1) * HEAD_DIM, pl.ds(ks, tq)]
                out.append((m_new[i], ls[i], alpha[i] * carry[i][2] + _dot(vt, ps[i].astype(BF16))))
            return tuple(out)

        init = tuple((jnp.full((1, tq), NEG_INF, F32), jnp.zeros((1, tq), F32), jnp.zeros((HEAD_DIM, tq), F32))
                     for _ in heads)
        carry = lax.fori_loop(0, qi, lambda kb, c: step(kb, c, False), init)
        carry = step(qi, carry, True)
        for i, h in enumerate(heads):
            _, l, acc = carry[i]
            o_ref[0, h * HEAD_DIM:(h + 1) * HEAD_DIM, :] = (acc / l).astype(BF16)


def _top16(v):
    bits = lax.bitcast_convert_type(v, jnp.uint32) & jnp.uint32(0xFFFF0000)
    return lax.bitcast_convert_type(bits, F32)


def _split3(c):
    hi = _top16(c)
    r1 = c - hi
    mid = _top16(r1)
    return hi, mid, r1 - mid


def _forgetting_attention(qb, kb, lf, vt, batch, seq):
    nq = seq // TQ_FOX
    put = np.zeros((LANES, 2 * FOX_HEADS * LANES), np.float32)
    for h in range(FOX_HEADS):
        for j in range(3):
            put[j * FOX_HEADS + h, h * LANES + FOX_Q_C[j]] = 1.0
            put[j * FOX_HEADS + h, FOX_HEADS * LANES + h * LANES + FOX_K_C[j]] = -1.0
    tile = lambda w: pl.BlockSpec((TQ_FOX, w), lambda b, i: (b * nq + i, 0))
    return pl.pallas_call(
        _fox_kernel,
        grid=(batch, nq),
        in_specs=[tile(FOX_HEADS * LANES), tile(FOX_HEADS * LANES), tile(LANES),
                  pl.BlockSpec((1, FOX_WIDTH, seq), lambda b, i: (b, 0, 0)),
                  pl.BlockSpec(put.shape, lambda b, i: (0, 0), pipeline_mode=pl.Buffered(1))],
        out_specs=pl.BlockSpec((1, FOX_WIDTH, TQ_FOX), lambda b, i: (b, 0, i)),
        out_shape=jax.ShapeDtypeStruct((batch, FOX_WIDTH, seq), BF16),
        scratch_shapes=[pltpu.VMEM((seq, FOX_HEADS * LANES), BF16), pltpu.VMEM((1, LANES), F32)],
        compiler_params=pltpu.CompilerParams(dimension_semantics=("arbitrary", "arbitrary"),
                                             vmem_limit_bytes=VMEM_LIMIT),
        name="forgetting_attention",
    )(qb, kb, lf, vt, jnp.asarray(put, BF16))


def _merge_kernel(x_ref, oa0_ref, oa1_ref, oa2_ref, ls0_ref, ls1_ref, ls2_ref, obt_ref, gate_ref,
                  wa_ref, wb_ref, wo_ref, g2_ref, wrh_ref, wrl_ref, br_ref,
                  x1_ref, h2_ref, ri_ref, rg_ref, cnt_ref, run_s, stage_s):
    tm = TM_MERGE

    @pl.when(pl.program_id(0) == 0)
    def _():
        run_s[...] = jnp.zeros_like(run_s)

    tp = tm // MERGE_PARTS
    parts = range(MERGE_PARTS)
    rows = [slice(p * tp, (p + 1) * tp) for p in parts]

    def token_order(ref, g, stage):
        dil = DSWA_PATTERNS[g][1]
        if dil == 1:
            return [ref[0, rs, :] for rs in rows]
        for r in range(dil):
            for half in range(2):
                col = r * DSWA_OUT + half * LANES
                stage[half, pl.ds(r, tm // dil, stride=dil), :] = ref[0, :, col:col + LANES]
        return [jnp.concatenate([stage[0, rs, :], stage[1, rs, :]], axis=1) for rs in rows]

    yb = [lax.dot_general(obt_ref[0, :, rs], wb_ref[...], (((0,), (0,)), ((), ())), preferred_element_type=F32)
          for rs in rows]
    ls = [token_order(ref, g, stage_s.at[g]) for g, ref in enumerate((ls0_ref, ls1_ref, ls2_ref))]
    os_ = [token_order(ref, g, stage_s.at[DSWA_GROUPS + g]) for g, ref in enumerate((oa0_ref, oa1_ref, oa2_ref))]
    o_a = []
    for p in parts:
        lm = jnp.maximum(jnp.maximum(ls[0][p], ls[1][p]), ls[2][p])
        e0, e1, e2 = jnp.exp(ls[0][p] - lm), jnp.exp(ls[1][p] - lm), jnp.exp(ls[2][p] - lm)
        o_a.append((e0 * os_[0][p] + e1 * os_[1][p] + e2 * os_[2][p]) / (e0 + e1 + e2))
    ya = [_dot(o_a[p].astype(BF16), wa_ref[...]) for p in parts]
    merged = [gate_ref[rows[p], :D_MODEL] * ya[p] + gate_ref[rows[p], D_MODEL:] * yb[p] for p in parts]
    x1 = [x_ref[rows[p], :] + _dot(merged[p].astype(BF16), wo_ref[...]) for p in parts]
    for p in parts:
        x1_ref[rows[p], :] = x1[p]

    h2 = [x1[p] * lax.rsqrt(jnp.mean(x1[p] * x1[p], axis=-1, keepdims=True) + RMS_EPS) * g2_ref[...] for p in parts]
    for p in parts:
        for c in range(ROW_CHUNKS):
            h2_ref[pl.ds(p * tp * ROW_CHUNKS + c, tp, stride=ROW_CHUNKS), :] = h2[p][:, c * LANES:(c + 1) * LANES]

    hi = [h2[p].astype(BF16) for p in parts]
    lo = [(h2[p] - hi[p].astype(F32)).astype(BF16) for p in parts]
    work = [_dot(hi[p], wrh_ref[...]) + _dot(lo[p], wrh_ref[...]) + _dot(hi[p], wrl_ref[...]) + br_ref[...]
            for p in parts]

    lane = lax.broadcasted_iota(jnp.int32, (tp, LANES), 1).astype(F32)
    top_v, top_i = [[] for _ in parts], [[] for _ in parts]
    for _ in range(TOP_K):
        mk = [jnp.max(work[p], axis=-1, keepdims=True) for p in parts]
        ik = [jnp.min(jnp.where(work[p] == mk[p], lane, float(LANES)), axis=-1, keepdims=True) for p in parts]
        for p in parts:
            top_v[p].append(mk[p])
            top_i[p].append(ik[p])
        work = [jnp.where(lane == ik[p], -jnp.inf, work[p]) for p in parts]

    r_i = lax.broadcasted_iota(jnp.int32, (tp, tp), 0)
    c_i = lax.broadcasted_iota(jnp.int32, (tp, tp), 1)
    tri = jnp.where(c_i < r_i, 1.0, 0.0).astype(BF16)
    onehot = []
    for p in parts:
        oh = jnp.zeros((tp, LANES), F32)
        for ik in top_i[p]:
            oh = oh + jnp.where(lane == ik, 1.0, 0.0)
        onehot.append(oh)
    within = [_dot(tri, onehot[p].astype(BF16)) for p in parts]
    run = run_s[...]
    for p in parts:
        before = within[p] + run
        ex = [jnp.exp(v - top_v[p][0]) for v in top_v[p]]
        den = ex[0] + ex[1] + ex[2] + ex[3]
        ri = jnp.zeros((tp, LANES), jnp.int32)
        rg = jnp.zeros((tp, LANES), F32)
        for k in range(TOP_K):
            rank = jnp.sum(jnp.where(lane == top_i[p][k], before, 0.0), axis=-1, keepdims=True)
            ri = jnp.where(lane == k, top_i[p][k].astype(jnp.int32), ri)
            ri = jnp.where(lane == TOP_K + k, rank.astype(jnp.int32), ri)
            rg = jnp.where(lane == k, ex[k] / den, rg)
        ri_ref[rows[p], :] = ri
        rg_ref[rows[p], :] = rg
        run = run + jnp.sum(onehot[p], axis=0, keepdims=True)
    run_s[...] = run
    cnt_ref[...] = run


def _merge_and_route(x2, batch, seq, oas, lss, obt, gate, w_branch_a, w_branch_b, w_out, ln2_g, w_router,
                     b_router):
    n = x2.shape[0]
    assert seq % TM_MERGE == 0 and all(TM_MERGE % (SUBLANES * d) == 0 for _, d in DSWA_PATTERNS)
    tiles_per_seq = seq // TM_MERGE
    seq_tile = lambda i: (i // tiles_per_seq, i % tiles_per_seq)
    group_specs = [pl.BlockSpec((1, TM_MERGE // d, d * DSWA_OUT), lambda i: (*seq_tile(i), 0))
                   for _, d in DSWA_PATTERNS]
    obt_spec = pl.BlockSpec((1, FOX_WIDTH, TM_MERGE), lambda i: (seq_tile(i)[0], 0, seq_tile(i)[1]))
    wr = jnp.pad(w_router, ((0, 0), (0, LANES - N_EXPERTS)))
    wr_top = _top16(wr)
    wr_hi = wr_top.astype(BF16)
    wr_lo = (wr - wr_top).astype(BF16)
    br = jnp.concatenate([b_router, jnp.full((LANES - N_EXPERTS,), NEG_INF, F32)]).reshape(1, LANES)
    const = lambda shape: pl.BlockSpec(shape, lambda i: (0,) * len(shape), pipeline_mode=pl.Buffered(1))
    row = lambda w: pl.BlockSpec((TM_MERGE, w), lambda i: (i, 0))
    return pl.pallas_call(
        _merge_kernel,
        grid=(n // TM_MERGE,),
        in_specs=[row(D_MODEL)] + group_specs + group_specs + [obt_spec, row(2 * D_MODEL),
                  const((DSWA_OUT, D_MODEL)), const((FOX_WIDTH, D_MODEL)), const((D_MODEL, D_MODEL)),
                  const((1, D_MODEL)), const((D_MODEL, LANES)), const((D_MODEL, LANES)), const((1, LANES))],
        out_specs=[row(D_MODEL), pl.BlockSpec((TM_MERGE * ROW_CHUNKS, LANES), lambda i: (i, 0)),
                   row(LANES), row(LANES), pl.BlockSpec((1, LANES), lambda i: (0, 0))],
        out_shape=[jax.ShapeDtypeStruct((n, D_MODEL), F32),
                   jax.ShapeDtypeStruct((n * ROW_CHUNKS, LANES), F32),
                   jax.ShapeDtypeStruct((n, LANES), jnp.int32),
                   jax.ShapeDtypeStruct((n, LANES), F32),
                   jax.ShapeDtypeStruct((1, LANES), F32)],
        scratch_shapes=[pltpu.VMEM((1, LANES), F32), pltpu.VMEM((2 * DSWA_GROUPS, 2, TM_MERGE, LANES), F32)],
        compiler_params=pltpu.CompilerParams(dimension_semantics=("arbitrary",), vmem_limit_bytes=VMEM_LIMIT),
        name="merge_and_route",
    )(x2, *oas, *lss, obt, gate, w_branch_a.astype(BF16), w_branch_b.astype(BF16), w_out.astype(BF16),
      ln2_g.reshape(1, D_MODEL), wr_hi, wr_lo, br)


def _row_copy(src, src_row, dst, dst_row, sem):
    return pltpu.make_async_copy(src.at[pl.ds(pl.multiple_of(src_row * ROW_CHUNKS, ROW_CHUNKS), ROW_CHUNKS)],
                                 dst.at[pl.ds(pl.multiple_of(dst_row * ROW_CHUNKS, ROW_CHUNKS), ROW_CHUNKS)], sem)


def _dispatch_kernel(zb_ref, zf_ref, dest_hbm, h2_ref, xs_hbm, idx_s, zeros_v, sem_i, sem_z, sem_r):
    i = pl.program_id(0)
    n_move = TOP_K * TC_MOVE
    idx_copy = pltpu.make_async_copy(dest_hbm.at[i], idx_s, sem_i)
    idx_copy.start()

    def zero_copy(e):
        blk = pl.multiple_of(zb_ref[e] * (T_EXPERT * ROW_CHUNKS), T_EXPERT * ROW_CHUNKS)
        return pltpu.make_async_copy(zeros_v, xs_hbm.at[pl.ds(blk, T_EXPERT * ROW_CHUNKS)], sem_z)

    @pl.when(i == 0)
    def _():
        zeros_v[...] = jnp.zeros_like(zeros_v)
        for e in range(N_EXPERTS):
            @pl.when(zf_ref[e] > 0)
            def _():
                zero_copy(e).start()
        for e in range(N_EXPERTS):
            @pl.when(zf_ref[e] > 0)
            def _():
                zero_copy(e).wait()

    idx_copy.wait()

    for k in range(TOP_K):
        def issue(to, carry, k=k):
            t0 = to * DMA_UNROLL
            for u in range(DMA_UNROLL):
                _row_copy(h2_ref, t0 + u, xs_hbm, idx_s[k * TC_MOVE + t0 + u], sem_r).start(priority=u % 2)
            return carry

        lax.fori_loop(0, TC_MOVE // DMA_UNROLL, issue, 0)

    def drain(jo, carry):
        for u in range(DMA_UNROLL):
            _row_copy(h2_ref, 0, xs_hbm, 0, sem_r).wait()
        return carry

    lax.fori_loop(0, n_move // DMA_UNROLL, drain, 0)


def _dispatch(dest_tiles, zero_blk, zero_flag, h2_rows, n_slots):
    nt = dest_tiles.shape[0]
    return pl.pallas_call(
        _dispatch_kernel,
        grid_spec=pltpu.PrefetchScalarGridSpec(
            num_scalar_prefetch=2, grid=(nt,),
            in_specs=[pl.BlockSpec(memory_space=pl.ANY),
                      pl.BlockSpec((TC_MOVE * ROW_CHUNKS, LANES), lambda i, zb, zf: (i, 0))],
            out_specs=pl.BlockSpec(memory_space=pl.ANY),
            scratch_shapes=[pltpu.SMEM((TOP_K * TC_MOVE,), jnp.int32),
                            pltpu.VMEM((T_EXPERT * ROW_CHUNKS, LANES), F32),
                            pltpu.SemaphoreType.DMA, pltpu.SemaphoreType.DMA, pltpu.SemaphoreType.DMA]),
        out_shape=jax.ShapeDtypeStruct((n_slots * ROW_CHUNKS, LANES), F32),
        compiler_params=pltpu.CompilerParams(dimension_semantics=("arbitrary",)),
        name="dispatch_rows",
    )(zero_blk, zero_flag, dest_tiles, h2_rows)


def _expert_kernel(be_ref, nu_ref, xs_ref, wgu_ref, bgu_ref, wd_ref, bd_ref, ys_ref, wgu_s, wd_s):
    j = pl.program_id(0)
    t = T_EXPERT

    @pl.when(j < nu_ref[0])
    def _():
        prev = be_ref[jnp.maximum(j - 1, 0)]

        @pl.when((j == 0) | (be_ref[j] != prev))
        def _():
            wgu_s[...] = wgu_ref[0].astype(BF16)
            wd_s[...] = wd_ref[0].astype(BF16)

        x = jnp.concatenate([xs_ref[pl.ds(c, t, stride=ROW_CHUNKS), :] for c in range(ROW_CHUNKS)], axis=1)
        gu = _dot(x.astype(BF16), wgu_s[...]) + bgu_ref[0]
        g = jnp.minimum(gu[:, :D_MODEL], SWIGLU_LIMIT)
        u = jnp.clip(gu[:, D_MODEL:], -SWIGLU_LIMIT, SWIGLU_LIMIT)
        mid = (u + 1.0) * (g * jax.nn.sigmoid(SWIGLU_ALPHA * g))
        y = _dot(mid.astype(BF16), wd_s[...]) + bd_ref[0]
        for c in range(ROW_CHUNKS):
            ys_ref[pl.ds(c, t, stride=ROW_CHUNKS), :] = y[:, c * LANES:(c + 1) * LANES]


def _expert_ffn(blk_e, n_used, xs_rows, w_gate_up, b_gate_up, w_down, b_down):
    n_blk = blk_e.shape[0]
    rows = T_EXPERT * ROW_CHUNKS
    live = lambda j, be, nu: jnp.minimum(j, nu[0] - 1)
    sink = lambda j, be, nu: jnp.where(j < nu[0], j, n_blk - 1)
    return pl.pallas_call(
        _expert_kernel,
        grid_spec=pltpu.PrefetchScalarGridSpec(
            num_scalar_prefetch=2, grid=(n_blk,),
            in_specs=[pl.BlockSpec((rows, LANES), lambda j, be, nu: (live(j, be, nu), 0)),
                      pl.BlockSpec((1, D_MODEL, 2 * D_MODEL), lambda j, be, nu: (be[live(j, be, nu)], 0, 0)),
                      pl.BlockSpec((1, 1, 2 * D_MODEL), lambda j, be, nu: (be[live(j, be, nu)], 0, 0)),
                      pl.BlockSpec((1, D_MODEL, D_MODEL), lambda j, be, nu: (be[live(j, be, nu)], 0, 0)),
                      pl.BlockSpec((1, 1, D_MODEL), lambda j, be, nu: (be[live(j, be, nu)], 0, 0))],
            out_specs=pl.BlockSpec((rows, LANES), lambda j, be, nu: (sink(j, be, nu), 0)),
            scratch_shapes=[pltpu.VMEM((D_MODEL, 2 * D_MODEL), BF16), pltpu.VMEM((D_MODEL, D_MODEL), BF16)]),
        out_shape=jax.ShapeDtypeStruct(xs_rows.shape, F32),
        compiler_params=pltpu.CompilerParams(dimension_semantics=("arbitrary",), vmem_limit_bytes=VMEM_LIMIT),
        name="expert_ffn",
    )(blk_e, n_used, xs_rows, w_gate_up, b_gate_up.reshape(N_EXPERTS, 1, 2 * D_MODEL),
      w_down, b_down.reshape(N_EXPERTS, 1, D_MODEL))


def _combine_kernel(dest_hbm, ys_hbm, x1_ref, rg_ref, o_ref, idx_a, idx_b, buf, gate_s, sem_i, sem_r):
    i = pl.program_id(0)
    n_tiles = pl.num_programs(0)
    n_move = TOP_K * TC_COMB
    idx = (idx_a, idx_b)

    def gather_tile(tile, s):
        slots = pltpu.make_async_copy(dest_hbm.at[tile], idx[s], sem_i)
        slots.start()
        slots.wait()

        def issue(jo, carry):
            for u in range(DMA_UNROLL):
                j = jo * DMA_UNROLL + u
                _row_copy(ys_hbm, idx[s][j], buf.at[s], j, sem_r.at[s]).start(priority=u % 2)
            return carry

        lax.fori_loop(0, n_move // DMA_UNROLL, issue, 0)

    def wait_tile(s):
        def drain(jo, carry):
            for u in range(DMA_UNROLL):
                _row_copy(ys_hbm, 0, buf.at[s], 0, sem_r.at[s]).wait()
            return carry

        lax.fori_loop(0, n_move // DMA_UNROLL, drain, 0)

    def sum_tile(s):
        for k in range(TOP_K):
            gate_s[k] = jnp.broadcast_to(rg_ref[:, k:k + 1], (TC_COMB, LANES))
        for c in range(ROW_CHUNKS):
            acc = x1_ref[:, c * LANES:(c + 1) * LANES]
            for k in range(TOP_K):
                rows = buf[s, pl.ds(k * TC_COMB * ROW_CHUNKS + c, TC_COMB, stride=ROW_CHUNKS), :]
                acc = acc + gate_s[k] * rows
            o_ref[:, c * LANES:(c + 1) * LANES] = acc

    @pl.when(i == 0)
    def _():
        gather_tile(0, 0)

    for s in range(2):
        @pl.when(i % 2 == s)
        def _(s=s):
            @pl.when(i + 1 < n_tiles)
            def _():
                gather_tile(i + 1, 1 - s)

            wait_tile(s)
            sum_tile(s)


def _combine(dest_tiles, ys_rows, x1, rg):
    n = x1.shape[0]
    nt = dest_tiles.shape[0]
    return pl.pallas_call(
        _combine_kernel,
        grid=(nt,),
        in_specs=[pl.BlockSpec(memory_space=pl.ANY), pl.BlockSpec(memory_space=pl.ANY),
                  pl.BlockSpec((TC_COMB, D_MODEL), lambda i: (i, 0)),
                  pl.BlockSpec((TC_COMB, LANES), lambda i: (i, 0))],
        out_specs=pl.BlockSpec((TC_COMB, D_MODEL), lambda i: (i, 0)),
        out_shape=jax.ShapeDtypeStruct((n, D_MODEL), F32),
        scratch_shapes=[pltpu.SMEM((TOP_K * TC_COMB,), jnp.int32), pltpu.SMEM((TOP_K * TC_COMB,), jnp.int32),
                        pltpu.VMEM((2, TOP_K * TC_COMB * ROW_CHUNKS, LANES), F32),
                        pltpu.VMEM((TOP_K, TC_COMB, LANES), F32),
                        pltpu.SemaphoreType.DMA, pltpu.SemaphoreType.DMA((2,))],
        compiler_params=pltpu.CompilerParams(dimension_semantics=("arbitrary",), vmem_limit_bytes=VMEM_LIMIT),
        name="combine_rows",
    )(dest_tiles, ys_rows, x1, rg)


def _layer(x, ln1_g, w_in, b_forget, b_gate, q_norm_a, k_norm_a, q_norm_b, k_norm_b, rel_bias,
           w_branch_a, w_branch_b, w_out, ln2_g, w_router, b_router, w_gate_up, b_gate_up, w_down, b_down):
    batch, seq, _ = x.shape
    n = batch * seq
    x2 = x.reshape(n, D_MODEL)
    *qkv_groups, qb, kb, vt, lf, gate = _in_projection(x2, batch, seq, ln1_g, w_in, b_forget, b_gate,
                                                       q_norm_a, k_norm_a, q_norm_b, k_norm_b)

    oas, lss = [], []
    for g in range(DSWA_GROUPS):
        o, lse = _dilated_attention(qkv_groups[g], rel_bias, g, batch, seq)
        oas.append(o)
        lss.append(lse)

    obt = _forgetting_attention(qb, kb, lf, vt, batch, seq)

    x1, h2_rows, ri, rg, cnt = _merge_and_route(x2, batch, seq, oas, lss, obt, gate, w_branch_a, w_branch_b,
                                                w_out, ln2_g, w_router, b_router)

    counts = cnt[0, :N_EXPERTS].astype(jnp.int32)
    pcounts = ((counts + T_EXPERT - 1) // T_EXPERT) * T_EXPERT
    pends = jnp.cumsum(pcounts)
    pstarts = pends - pcounts
    idx, rank = ri[:, :TOP_K], ri[:, TOP_K:2 * TOP_K]
    start_of = jnp.sum(jnp.where(idx[..., None] == jnp.arange(N_EXPERTS), pstarts, 0), axis=-1)
    dest = start_of + rank
    by_tile = lambda t: jnp.transpose(dest.reshape(n // t, t, TOP_K), (0, 2, 1)).reshape(n // t, TOP_K * t)
    n_slots = n * TOP_K + N_EXPERTS * T_EXPERT
    n_blk = n_slots // T_EXPERT
    blk_first = jnp.arange(n_blk, dtype=jnp.int32) * T_EXPERT
    blk_e = jnp.minimum(jnp.sum((pends[None, :] <= blk_first[:, None]).astype(jnp.int32), axis=1), N_EXPERTS - 1)
    n_used = (pends[-1:] // T_EXPERT).astype(jnp.int32)
    zero_blk = jnp.maximum(pends // T_EXPERT - 1, 0).astype(jnp.int32)
    zero_flag = (pcounts > 0).astype(jnp.int32)

    xs_rows = _dispatch(by_tile(TC_MOVE), zero_blk, zero_flag, h2_rows, n_slots)
    ys_rows = _expert_ffn(blk_e, n_used, xs_rows, w_gate_up, b_gate_up, w_down, b_down)
    out = _combine(by_tile(TC_COMB), ys_rows, x1, rg)
    return out.reshape(batch, seq, D_MODEL)


def kernel(x, ln1_g, w_in, b_forget, b_gate, q_norm_a, k_norm_a, q_norm_b, k_norm_b, rel_bias, w_branch_a, w_branch_b, w_out, ln2_g, w_router, b_router, w_gate_up, b_gate_up, w_down, b_down):
    for layer in range(ln1_g.shape[0]):
        x = _layer(x, ln1_g[layer], w_in[layer], b_forget[layer], b_gate[layer], q_norm_a[layer],
                   k_norm_a[layer], q_norm_b[layer], k_norm_b[layer], rel_bias, w_branch_a[layer],
                   w_branch_b[layer], w_out[layer], ln2_g[layer], w_router[layer], b_router[layer],
                   w_gate_up[layer], b_gate_up[layer], w_down[layer], b_down[layer])
    return x
```

```python
import functools
import math

import numpy as np
import jax
import jax.numpy as jnp
from jax import lax
from jax.experimental import pallas as pl
from jax.experimental.pallas import tpu as pltpu

F32 = jnp.float32
BF16 = jnp.bfloat16

D_MODEL = 1024
HEAD_DIM = 64
DSWA_PATTERNS = ((128, 1), (512, 4), (2048, 16))
DSWA_HG = 4
DSWA_GROUPS = len(DSWA_PATTERNS)
DSWA_WIDTH = DSWA_GROUPS * DSWA_HG * HEAD_DIM
DSWA_OUT = DSWA_HG * HEAD_DIM
FOX_HEADS = 8
FOX_WIDTH = FOX_HEADS * HEAD_DIM
ATTN_BLOCK = 128
NUM_BUCKETS = 32
MAX_DISTANCE = 2048
N_EXPERTS = 32
TOP_K = 4
SWIGLU_LIMIT = 7.0
SWIGLU_ALPHA = 1.702
RMS_EPS = 1e-6
NEG_INF = -1e30
LOG2E = 1.4426950408889634

LANES = 128
SUBLANES = 8
ROW_CHUNKS = D_MODEL // LANES
QKV_WIDTH = 3 * DSWA_WIDTH + 3 * FOX_WIDTH
FOX_Q_ONES = (64, 65, 66)
FOX_K_C = (64, 65, 66)
FOX_Q_C = (67, 68, 69)
FOX_K_ONES = (67, 68, 69)
VMEM_LIMIT = 56 * 1024 * 1024

TM_PROJ = 512
PROJ_LOOKAHEAD = 1
TQ_FOX = 512
FOX_HEADS_PER_LOOP = 8
DSWA_BLOCKS_PER_STEP = 4
TM_MERGE = 512
MERGE_PARTS = 2
T_EXPERT = 512
TC_MOVE = 1024
DMA_UNROLL = 16


def _dot(a, b):
    return jnp.dot(a, b, preferred_element_type=F32)


def _dot_nt(a, b):
    return lax.dot_general(a, b, (((1,), (1,)), ((), ())), preferred_element_type=F32)


def _inproj_kernel(x_ref, g1_ref, wq_ref, wf_ref, wg_ref, bf_ref, bg_ref, nrm_ref, e_ref, place_ref, pat_ref,
                   qkv0_ref, qkv1_ref, qkv2_ref, qb_ref, kb_ref, vt_ref, lf_ref, gate_ref, stage_ref):
    group_refs = (qkv0_ref, qkv1_ref, qkv2_ref)
    x = x_ref[...]
    ms = jnp.mean(x * x, axis=-1, keepdims=True)
    h = (x * lax.rsqrt(ms + RMS_EPS) * g1_ref[...]).astype(BF16)
    ones_bd = e_ref[...]

    def head_norm(y, gn):
        ss = _dot((y * y).astype(BF16), ones_bd)
        return y * lax.rsqrt(ss * (1.0 / HEAD_DIM) + RMS_EPS) * gn

    n_a = 3 * DSWA_WIDTH // 256

    def qkv_epilogue(c, y):
        if c < 3:
            y = head_norm(y, nrm_ref[0:1, :])
        elif c < 6:
            y = head_norm(y, nrm_ref[1:2, :])
        elif 9 <= c < 11:
            y = head_norm(y, nrm_ref[2:3, :])
        elif 11 <= c < 13:
            y = head_norm(y, nrm_ref[3:4, :])
        if c < n_a:
            g, part = c % DSWA_GROUPS, c // DSWA_GROUPS
            dil = DSWA_PATTERNS[g][1]
            if dil == 1:
                group_refs[g][0, :, part * 256:(part + 1) * 256] = y.astype(BF16)
            else:
                for half in range(2):
                    stage_ref[half] = y[:, half * LANES:(half + 1) * LANES]
                for r in range(dil):
                    for half in range(2):
                        col = r * DSWA_WIDTH + part * 256 + half * LANES
                        rows = stage_ref[half, pl.ds(r, TM_PROJ // dil, stride=dil), :]
                        group_refs[g][0, :, col:col + LANES] = rows.astype(BF16)
        elif c < n_a + 4:
            is_k, half = divmod(c - n_a, 2)
            slots = _dot(y.astype(BF16), place_ref[...]) + pat_ref[is_k:is_k + 1, :]
            (kb_ref if is_k else qb_ref)[:, half * 512:(half + 1) * 512] = slots.astype(BF16)
        else:
            half = c - n_a - 4
            vt_ref[0, half * 256:(half + 1) * 256, :] = y.T.astype(BF16)

    def forget_epilogue(y):
        z = y + bf_ref[...]
        lf_ref[...] = (jnp.minimum(z, 0.0) - jnp.log1p(jnp.exp(-jnp.abs(z)))) * LOG2E

    def gate_epilogue(c, y):
        gate_ref[:, c * 256:(c + 1) * 256] = jax.nn.sigmoid(y + bg_ref[:, c * 256:(c + 1) * 256])

    jobs = [(functools.partial(lambda c: _dot(h, wq_ref[:, c * 256:(c + 1) * 256]), c),
             functools.partial(qkv_epilogue, c)) for c in range(QKV_WIDTH // 256)]
    jobs.append((lambda: _dot(h, wf_ref[...]), forget_epilogue))
    jobs += [(functools.partial(lambda c: _dot(h, wg_ref[:, c * 256:(c + 1) * 256]), c),
              functools.partial(gate_epilogue, c)) for c in range(2 * D_MODEL // 256)]
    pending = [matmul() for matmul, _ in jobs[:PROJ_LOOKAHEAD]]
    for j, (_, epilogue) in enumerate(jobs):
        if j + PROJ_LOOKAHEAD < len(jobs):
            pending.append(jobs[j + PROJ_LOOKAHEAD][0]())
        epilogue(pending.pop(0))


def _in_projection(x2, batch, seq, ln1_g, w_in, b_forget, b_gate, q_norm_a, k_norm_a, q_norm_b, k_norm_b):
    n = x2.shape[0]
    assert seq % TM_PROJ == 0 and all(TM_PROJ % (16 * d) == 0 for _, d in DSWA_PATTERNS)
    tiles_per_seq = seq // TM_PROJ
    wq = w_in[:, :QKV_WIDTH].astype(BF16)
    wf = jnp.pad(w_in[:, QKV_WIDTH:QKV_WIDTH + FOX_HEADS], ((0, 0), (0, LANES - FOX_HEADS))).astype(BF16)
    wg = w_in[:, QKV_WIDTH + FOX_HEADS:].astype(BF16)
    bf = jnp.pad(b_forget, (0, LANES - FOX_HEADS)).reshape(1, LANES)
    bg = b_gate.reshape(1, 2 * D_MODEL)
    scale = HEAD_DIM ** -0.5
    nrm = jnp.stack([jnp.tile(q_norm_a, 4) * scale, jnp.tile(k_norm_a, 4),
                     jnp.tile(q_norm_b, 4) * (scale * LOG2E), jnp.tile(k_norm_b, 4)])
    head_id = np.arange(256) // HEAD_DIM
    ones_bd = jnp.asarray(head_id[:, None] == head_id[None, :], BF16)
    src = np.arange(256)
    place = np.zeros((256, 4 * LANES), np.float32)
    place[src, (src // HEAD_DIM) * LANES + src % HEAD_DIM] = 1.0
    pat = np.zeros((2, 4 * LANES), np.float32)
    for slot in range(4):
        pat[0, [slot * LANES + l for l in FOX_Q_ONES]] = 1.0
        pat[1, [slot * LANES + l for l in FOX_K_ONES]] = 1.0
    const = lambda shape: pl.BlockSpec(shape, lambda i: (0,) * len(shape), pipeline_mode=pl.Buffered(1))
    row = lambda w: pl.BlockSpec((TM_PROJ, w), lambda i: (i, 0))
    seq_tile = lambda i: (i // tiles_per_seq, i % tiles_per_seq)
    group_spec = lambda d: pl.BlockSpec((1, TM_PROJ // d, d * DSWA_WIDTH), lambda i: (*seq_tile(i), 0))
    group_shape = lambda d: jax.ShapeDtypeStruct((batch, seq // d, d * DSWA_WIDTH), BF16)
    vt_spec = pl.BlockSpec((1, FOX_WIDTH, TM_PROJ), lambda i: (seq_tile(i)[0], 0, seq_tile(i)[1]))
    dils = [d for _, d in DSWA_PATTERNS]
    return pl.pallas_call(
        _inproj_kernel,
        grid=(n // TM_PROJ,),
        in_specs=[row(D_MODEL), const((1, D_MODEL)), const((D_MODEL, QKV_WIDTH)), const((D_MODEL, LANES)),
                  const((D_MODEL, 2 * D_MODEL)), const((1, LANES)), const((1, 2 * D_MODEL)),
                  const((4, 256)), const((256, 256)), const((256, 4 * LANES)), const((2, 4 * LANES))],
        out_specs=[group_spec(d) for d in dils] + [row(FOX_HEADS * LANES), row(FOX_HEADS * LANES), vt_spec,
                                                   row(LANES), row(2 * D_MODEL)],
        out_shape=[group_shape(d) for d in dils] + [jax.ShapeDtypeStruct((n, FOX_HEADS * LANES), BF16),
                                                    jax.ShapeDtypeStruct((n, FOX_HEADS * LANES), BF16),
                                                    jax.ShapeDtypeStruct((batch, FOX_WIDTH, seq), BF16),
                                                    jax.ShapeDtypeStruct((n, LANES), F32),
                                                    jax.ShapeDtypeStruct((n, 2 * D_MODEL), F32)],
        scratch_shapes=[pltpu.VMEM((2, TM_PROJ, LANES), F32)],
        compiler_params=pltpu.CompilerParams(dimension_semantics=("arbitrary",), vmem_limit_bytes=VMEM_LIMIT),
        name="in_projection",
    )(x2, ln1_g.reshape(1, D_MODEL), wq, wf, wg, bf, bg, nrm, ones_bd, jnp.asarray(place, BF16), jnp.asarray(pat))


def _t5_bucket(n):
    max_exact = NUM_BUCKETS // 2
    n_safe = np.maximum(n, 1).astype(np.float64)
    large = max_exact + (np.log(n_safe / max_exact) / np.log(MAX_DISTANCE / max_exact)
                         * (NUM_BUCKETS - max_exact)).astype(np.int64)
    large = np.minimum(large, NUM_BUCKETS - 1)
    return np.where(n < max_exact, n, large).astype(np.int32)


def _dswa_kernel(q_ref, kp_ref, kc_ref, vp_ref, vc_ref, bias_ref, o_ref, l_ref, *, blocks):
    first = pl.program_id(2) == 0
    items = [(j, h) for j in range(blocks) for h in range(DSWA_HG)]
    rows = lambda j: slice(j * ATTN_BLOCK, (j + 1) * ATTN_BLOCK)
    head = lambda h: slice(h * HEAD_DIM, (h + 1) * HEAD_DIM)

    sps, scs = [], []
    for j, h in items:
        q = q_ref[0, rows(j), head(h)]
        k_prev = kp_ref[0, :, head(h)] if j == 0 else kc_ref[0, rows(j - 1), head(h)]
        sp = _dot_nt(q, k_prev) + bias_ref[h, :, :ATTN_BLOCK]
        if j == 0:
            sp = jnp.where(first, NEG_INF, sp)
        sps.append(sp)
        scs.append(_dot_nt(q, kc_ref[0, rows(j), head(h)]) + bias_ref[h, :, ATTN_BLOCK:])
    ms = [jnp.maximum(jnp.max(sp, axis=-1, keepdims=True), jnp.max(sc, axis=-1, keepdims=True))
          for sp, sc in zip(sps, scs)]
    pps = [jnp.exp(sp - m) for sp, m in zip(sps, ms)]
    pcs = [jnp.exp(sc - m) for sc, m in zip(scs, ms)]
    dens = [jnp.sum(pp, axis=-1, keepdims=True) + jnp.sum(pc, axis=-1, keepdims=True) for pp, pc in zip(pps, pcs)]
    for (j, h), pp, pc, m, den in zip(items, pps, pcs, ms, dens):
        v_prev = vp_ref[0, :, head(h)] if j == 0 else vc_ref[0, rows(j - 1), head(h)]
        o = _dot(pp.astype(BF16), v_prev) + _dot(pc.astype(BF16), vc_ref[0, rows(j), head(h)])
        o_ref[0, rows(j), head(h)] = o / den
        l_ref[0, rows(j), head(h)] = jnp.broadcast_to(m + jnp.log(den), (ATTN_BLOCK, HEAD_DIM))


def _dilated_attention(qkv3, rel_bias, g, batch, seq):
    window, dil = DSWA_PATTERNS[g]
    steps = window // dil
    assert steps == ATTN_BLOCK
    length = seq // dil
    assert length % ATTN_BLOCK == 0
    nblk = length // ATTN_BLOCK
    span = 2 * ATTN_BLOCK
    back = np.arange(ATTN_BLOCK)[:, None] + steps - np.arange(span)[None, :]
    valid = (back >= 0) & (back <= steps)
    bucket = _t5_bucket(np.clip(back, 0, steps) * dil)
    cols = rel_bias[:, g * DSWA_HG:(g + 1) * DSWA_HG]
    onehot = np.asarray(bucket.reshape(-1)[:, None] == np.arange(NUM_BUCKETS)[None, :], np.float32)
    bias = jnp.dot(jnp.asarray(onehot), cols.astype(F32), precision=lax.Precision.HIGHEST)
    bias = jnp.transpose(bias.reshape(ATTN_BLOCK, span, DSWA_HG), (2, 0, 1))
    bias = jnp.where(valid[None], bias, NEG_INF)

    per_pos = DSWA_WIDTH // 256
    nb = math.gcd(nblk, DSWA_BLOCKS_PER_STEP)
    blk = (1, nb * ATTN_BLOCK, 256)
    one = (1, ATTN_BLOCK, 256)
    before = lambda n: jnp.maximum(n * nb - 1, 0)
    q_spec = pl.BlockSpec(blk, lambda b, r, n: (b, n, r * per_pos))
    kc_spec = pl.BlockSpec(blk, lambda b, r, n: (b, n, r * per_pos + 1))
    kp_spec = pl.BlockSpec(one, lambda b, r, n: (b, before(n), r * per_pos + 1))
    vc_spec = pl.BlockSpec(blk, lambda b, r, n: (b, n, r * per_pos + 2))
    vp_spec = pl.BlockSpec(one, lambda b, r, n: (b, before(n), r * per_pos + 2))
    bias_spec = pl.BlockSpec((DSWA_HG, ATTN_BLOCK, span), lambda b, r, n: (0, 0, 0))
    o_spec = pl.BlockSpec(blk, lambda b, r, n: (b, n, r))
    return pl.pallas_call(
        functools.partial(_dswa_kernel, blocks=nb),
        grid=(batch, dil, nblk // nb),
        in_specs=[q_spec, kp_spec, kc_spec, vp_spec, vc_spec, bias_spec],
        out_specs=[o_spec, o_spec],
        out_shape=[jax.ShapeDtypeStruct((batch, length, dil * DSWA_OUT), F32)] * 2,
        compiler_params=pltpu.CompilerParams(dimension_semantics=("arbitrary",) * 3),
        name=f"dilated_attention_{g}",
    )(qkv3, qkv3, qkv3, qkv3, qkv3, bias)


def _fox_kernel(q_ref, k_ref, lf_ref, vt_ref, put_ref, o_ref, kaug_s, carry_s):
    qi = pl.program_id(1)
    tq = TQ_FOX
    krow = lax.broadcasted_iota(jnp.int32, (tq, tq), 0)
    qcol = lax.broadcasted_iota(jnp.int32, (tq, tq), 1)
    causal = krow <= qcol

    @pl.when(qi == 0)
    def _():
        carry_s[...] = jnp.zeros_like(carry_s)

    tri = jnp.where(krow >= qcol, 1.0, 0.0).astype(BF16)
    lane = lax.broadcasted_iota(jnp.int32, (tq, LANES), 1)

    def pack3(v):
        hi, mid, lo = _split3(v)
        packed = jnp.where(lane < FOX_HEADS, hi,
                           jnp.where(lane < 2 * FOX_HEADS, pltpu.roll(mid, FOX_HEADS, axis=1),
                                     jnp.where(lane < 3 * FOX_HEADS, pltpu.roll(lo, 2 * FOX_HEADS, axis=1), 0.0)))
        return packed.astype(BF16)

    sums = _dot(tri, pack3(lf_ref[...]))
    c = (carry_s[...] + sums + pltpu.roll(sums, LANES - FOX_HEADS, axis=1)
         + pltpu.roll(sums, LANES - 2 * FOX_HEADS, axis=1))
    carry_s[...] = c[tq - 1:tq, :]
    bias = _dot(pack3(c), put_ref[...])
    q_all = q_ref[...] + bias[:, :FOX_HEADS * LANES].astype(BF16)
    kaug_s[pl.ds(pl.multiple_of(qi * tq, tq), tq), :] = k_ref[...] + bias[:, FOX_HEADS * LANES:].astype(BF16)

    for hg in range(FOX_HEADS // FOX_HEADS_PER_LOOP):
        heads = [hg * FOX_HEADS_PER_LOOP + i for i in range(FOX_HEADS_PER_LOOP)]
        qs = [q_all[:, h * LANES:(h + 1) * LANES] for h in heads]

        def step(kb, carry, masked):
            ks = pl.multiple_of(kb * tq, tq)
            idx = range(len(heads))
            sts = []
            for i, h in enumerate(heads):
                st = _dot_nt(kaug_s[pl.ds(ks, tq), h * LANES:(h + 1) * LANES], qs[i])
                sts.append(jnp.where(causal, st, NEG_INF) if masked else st)
            m_new = [jnp.maximum(carry[i][0], jnp.max(sts[i], axis=0, keepdims=True)) for i in idx]
            alpha = [jnp.exp2(carry[i][0] - m_new[i]) for i in idx]
            ps = [jnp.exp2(sts[i] - m_new[i]) for i in idx]
            ls = [alpha[i] * carry[i][1] + jnp.sum(ps[i], axis=0, keepdims=True) for i in idx]
            out = []
            for i, h in enumerate(heads):
                vt = vt_ref[0, h * HEAD_DIM:(h + 1) * HEAD_DIM, pl.ds(ks, tq)]
                out.append((m_new[i], ls[i], alpha[i] * carry[i][2] + _dot(vt, ps[i].astype(BF16))))
            return tuple(out)

        init = tuple((jnp.full((1, tq), NEG_INF, F32), jnp.zeros((1, tq), F32), jnp.zeros((HEAD_DIM, tq), F32))
                     for _ in heads)
        carry = lax.fori_loop(0, qi, lambda kb, c: step(kb, c, False), init)
        carry = step(qi, carry, True)
        for i, h in enumerate(heads):
            _, l, acc = carry[i]
            o_ref[0, h * HEAD_DIM:(h + 1) * HEAD_DIM, :] = (acc / l).astype(BF16)


def _top16(v):
    bits = lax.bitcast_convert_type(v, jnp.uint32) & jnp.uint32(0xFFFF0000)
    return lax.bitcast_convert_type(bits, F32)


def _split3(c):
    hi = _top16(c)
    r1 = c - hi
    mid = _top16(r1)
    return hi, mid, r1 - mid


def _forgetting_attention(qb, kb, lf, vt, batch, seq):
    nq = seq // TQ_FOX
    put = np.zeros((LANES, 2 * FOX_HEADS * LANES), np.float32)
    for h in range(FOX_HEADS):
        for j in range(3):
            put[j * FOX_HEADS + h, h * LANES + FOX_Q_C[j]] = 1.0
            put[j * FOX_HEADS + h, FOX_HEADS * LANES + h * LANES + FOX_K_C[j]] = -1.0
    tile = lambda w: pl.BlockSpec((TQ_FOX, w), lambda b, i: (b * nq + i, 0))
    return pl.pallas_call(
        _fox_kernel,
        grid=(batch, nq),
        in_specs=[tile(FOX_HEADS * LANES), tile(FOX_HEADS * LANES), tile(LANES),
                  pl.BlockSpec((1, FOX_WIDTH, seq), lambda b, i: (b, 0, 0)),
                  pl.BlockSpec(put.shape, lambda b, i: (0, 0), pipeline_mode=pl.Buffered(1))],
        out_specs=pl.BlockSpec((1, FOX_WIDTH, TQ_FOX), lambda b, i: (b, 0, i)),
        out_shape=jax.ShapeDtypeStruct((batch, FOX_WIDTH, seq), BF16),
        scratch_shapes=[pltpu.VMEM((seq, FOX_HEADS * LANES), BF16), pltpu.VMEM((1, LANES), F32)],
        compiler_params=pltpu.CompilerParams(dimension_semantics=("arbitrary", "arbitrary"),
                                             vmem_limit_bytes=VMEM_LIMIT),
        name="forgetting_attention",
    )(qb, kb, lf, vt, jnp.asarray(put, BF16))


def _merge_kernel(x_ref, oa0_ref, oa1_ref, oa2_ref, ls0_ref, ls1_ref, ls2_ref, obt_ref, gate_ref,
                  wa_ref, wb_ref, wo_ref, g2_ref, wrh_ref, wrl_ref, br_ref,
                  x1_ref, h2_ref, ri_ref, rg_ref, cnt_ref, run_s, stage_s):
    tm = TM_MERGE

    @pl.when(pl.program_id(0) == 0)
    def _():
        run_s[...] = jnp.zeros_like(run_s)

    tp = tm // MERGE_PARTS
    parts = range(MERGE_PARTS)
    rows = [slice(p * tp, (p + 1) * tp) for p in parts]

    def token_order(ref, g, stage):
        dil = DSWA_PATTERNS[g][1]
        if dil == 1:
            return [ref[0, rs, :] for rs in rows]
        for r in range(dil):
            for half in range(2):
                col = r * DSWA_OUT + half * LANES
                stage[half, pl.ds(r, tm // dil, stride=dil), :] = ref[0, :, col:col + LANES]
        return [jnp.concatenate([stage[0, rs, :], stage[1, rs, :]], axis=1) for rs in rows]

    yb = [lax.dot_general(obt_ref[0, :, rs], wb_ref[...], (((0,), (0,)), ((), ())), preferred_element_type=F32)
          for rs in rows]
    ls = [token_order(ref, g, stage_s.at[g]) for g, ref in enumerate((ls0_ref, ls1_ref, ls2_ref))]
    os_ = [token_order(ref, g, stage_s.at[DSWA_GROUPS + g]) for g, ref in enumerate((oa0_ref, oa1_ref, oa2_ref))]
    o_a = []
    for p in parts:
        lm = jnp.maximum(jnp.maximum(ls[0][p], ls[1][p]), ls[2][p])
        e0, e1, e2 = jnp.exp(ls[0][p] - lm), jnp.exp(ls[1][p] - lm), jnp.exp(ls[2][p] - lm)
        o_a.append((e0 * os_[0][p] + e1 * os_[1][p] + e2 * os_[2][p]) / (e0 + e1 + e2))
    ya = [_dot(o_a[p].astype(BF16), wa_ref[...]) for p in parts]
    merged = [gate_ref[rows[p], :D_MODEL] * ya[p] + gate_ref[rows[p], D_MODEL:] * yb[p] for p in parts]
    x1 = [x_ref[rows[p], :] + _dot(merged[p].astype(BF16), wo_ref[...]) for p in parts]
    for p in parts:
        x1_ref[rows[p], :] = x1[p]

    h2 = [x1[p] * lax.rsqrt(jnp.mean(x1[p] * x1[p], axis=-1, keepdims=True) + RMS_EPS) * g2_ref[...] for p in parts]
    for p in parts:
        for c in range(ROW_CHUNKS):
            h2_ref[pl.ds(p * tp * ROW_CHUNKS + c, tp, stride=ROW_CHUNKS), :] = h2[p][:, c * LANES:(c + 1) * LANES]

    hi = [h2[p].astype(BF16) for p in parts]
    lo = [(h2[p] - hi[p].astype(F32)).astype(BF16) for p in parts]
    work = [_dot(hi[p], wrh_ref[...]) + _dot(lo[p], wrh_ref[...]) + _dot(hi[p], wrl_ref[...]) + br_ref[...]
            for p in parts]

    lane = lax.broadcasted_iota(jnp.int32, (tp, LANES), 1).astype(F32)
    top_v, top_i = [[] for _ in parts], [[] for _ in parts]
    for _ in range(TOP_K):
        mk = [jnp.max(work[p], axis=-1, keepdims=True) for p in parts]
        ik = [jnp.min(jnp.where(work[p] == mk[p], lane, float(LANES)), axis=-1, keepdims=True) for p in parts]
        for p in parts:
            top_v[p].append(mk[p])
            top_i[p].append(ik[p])
        work = [jnp.where(lane == ik[p], -jnp.inf, work[p]) for p in parts]

    r_i = lax.broadcasted_iota(jnp.int32, (tp, tp), 0)
    c_i = lax.broadcasted_iota(jnp.int32, (tp, tp), 1)
    tri = jnp.where(c_i < r_i, 1.0, 0.0).astype(BF16)
    onehot = []
    for p in parts:
        oh = jnp.zeros((tp, LANES), F32)
        for ik in top_i[p]:
            oh = oh + jnp.where(lane == ik, 1.0, 0.0)
        onehot.append(oh)
    within = [_dot(tri, onehot[p].astype(BF16)) for p in parts]
    run = run_s[...]
    for p in parts:
        before = within[p] + run
        ex = [jnp.exp(v - top_v[p][0]) for v in top_v[p]]
        den = ex[0] + ex[1] + ex[2] + ex[3]
        ri = jnp.zeros((tp, LANES), jnp.int32)
        rg = jnp.zeros((tp, LANES), F32)
        for k in range(TOP_K):
            rank = jnp.sum(jnp.where(lane == top_i[p][k], before, 0.0), axis=-1, keepdims=True)
            ri = jnp.where(lane == k, top_i[p][k].astype(jnp.int32), ri)
            ri = jnp.where(lane == TOP_K + k, rank.astype(jnp.int32), ri)
            rg = jnp.where(lane == k, ex[k] / den, rg)
        ri_ref[rows[p], :] = ri
        rg_ref[rows[p], :] = rg
        run = run + jnp.sum(onehot[p], axis=0, keepdims=True)
    run_s[...] = run
    cnt_ref[...] = run


def _merge_and_route(x2, batch, seq, oas, lss, obt, gate, w_branch_a, w_branch_b, w_out, ln2_g, w_router,
                     b_router):
    n = x2.shape[0]
    assert seq % TM_MERGE == 0 and all(TM_MERGE % (SUBLANES * d) == 0 for _, d in DSWA_PATTERNS)
    tiles_per_seq = seq // TM_MERGE
    seq_tile = lambda i: (i // tiles_per_seq, i % tiles_per_seq)
    group_specs = [pl.BlockSpec((1, TM_MERGE // d, d * DSWA_OUT), lambda i: (*seq_tile(i), 0))
                   for _, d in DSWA_PATTERNS]
    obt_spec = pl.BlockSpec((1, FOX_WIDTH, TM_MERGE), lambda i: (seq_tile(i)[0], 0, seq_tile(i)[1]))
    wr = jnp.pad(w_router, ((0, 0), (0, LANES - N_EXPERTS)))
    wr_top = _top16(wr)
    wr_hi = wr_top.astype(BF16)
    wr_lo = (wr - wr_top).astype(BF16)
    br = jnp.concatenate([b_router, jnp.full((LANES - N_EXPERTS,), NEG_INF, F32)]).reshape(1, LANES)
    const = lambda shape: pl.BlockSpec(shape, lambda i: (0,) * len(shape), pipeline_mode=pl.Buffered(1))
    row = lambda w: pl.BlockSpec((TM_MERGE, w), lambda i: (i, 0))
    return pl.pallas_call(
        _merge_kernel,
        grid=(n // TM_MERGE,),
        in_specs=[row(D_MODEL)] + group_specs + group_specs + [obt_spec, row(2 * D_MODEL),
                  const((DSWA_OUT, D_MODEL)), const((FOX_WIDTH, D_MODEL)), const((D_MODEL, D_MODEL)),
                  const((1, D_MODEL)), const((D_MODEL, LANES)), const((D_MODEL, LANES)), const((1, LANES))],
        out_specs=[row(D_MODEL), pl.BlockSpec((TM_MERGE * ROW_CHUNKS, LANES), lambda i: (i, 0)),
                   row(LANES), row(LANES), pl.BlockSpec((1, LANES), lambda i: (0, 0))],
        out_shape=[jax.ShapeDtypeStruct((n, D_MODEL), F32),
                   jax.ShapeDtypeStruct((n * ROW_CHUNKS, LANES), F32),
                   jax.ShapeDtypeStruct((n, LANES), jnp.int32),
                   jax.ShapeDtypeStruct((n, LANES), F32),
                   jax.ShapeDtypeStruct((1, LANES), F32)],
        scratch_shapes=[pltpu.VMEM((1, LANES), F32), pltpu.VMEM((2 * DSWA_GROUPS, 2, TM_MERGE, LANES), F32)],
        compiler_params=pltpu.CompilerParams(dimension_semantics=("arbitrary",), vmem_limit_bytes=VMEM_LIMIT),
        name="merge_and_route",
    )(x2, *oas, *lss, obt, gate, w_branch_a.astype(BF16), w_branch_b.astype(BF16), w_out.astype(BF16),
      ln2_g.reshape(1, D_MODEL), wr_hi, wr_lo, br)


def _row_copy(src, src_row, dst, dst_row, sem):
    return pltpu.make_async_copy(src.at[pl.ds(pl.multiple_of(src_row * ROW_CHUNKS, ROW_CHUNKS), ROW_CHUNKS)],
                                 dst.at[pl.ds(pl.multiple_of(dst_row * ROW_CHUNKS, ROW_CHUNKS), ROW_CHUNKS)], sem)


def _dispatch_kernel(zb_ref, zf_ref, dest_hbm, h2_ref, xs_hbm, idx_s, zeros_v, sem_i, sem_z, sem_r):
    i = pl.program_id(0)
    n_move = TOP_K * TC_MOVE
    idx_copy = pltpu.make_async_copy(dest_hbm.at[i], idx_s, sem_i)
    idx_copy.start()

    def zero_copy(e):
        blk = pl.multiple_of(zb_ref[e] * (T_EXPERT * ROW_CHUNKS), T_EXPERT * ROW_CHUNKS)
        return pltpu.make_async_copy(zeros_v, xs_hbm.at[pl.ds(blk, T_EXPERT * ROW_CHUNKS)], sem_z)

    @pl.when(i == 0)
    def _():
        zeros_v[...] = jnp.zeros_like(zeros_v)
        for e in range(N_EXPERTS):
            @pl.when(zf_ref[e] > 0)
            def _():
                zero_copy(e).start()
        for e in range(N_EXPERTS):
            @pl.when(zf_ref[e] > 0)
            def _():
                zero_copy(e).wait()

    idx_copy.wait()

    for k in range(TOP_K):
        def issue(to, carry, k=k):
            t0 = to * DMA_UNROLL
            for u in range(DMA_UNROLL):
                _row_copy(h2_ref, t0 + u, xs_hbm, idx_s[k * TC_MOVE + t0 + u], sem_r).start(priority=u % 2)
            return carry

        lax.fori_loop(0, TC_MOVE // DMA_UNROLL, issue, 0)

    def drain(jo, carry):
        for u in range(DMA_UNROLL):
            _row_copy(h2_ref, 0, xs_hbm, 0, sem_r).wait()
        return carry

    lax.fori_loop(0, n_move // DMA_UNROLL, drain, 0)


def _dispatch(dest_tiles, zero_blk, zero_flag, h2_rows, n_slots):
    nt = dest_tiles.shape[0]
    return pl.pallas_call(
        _dispatch_kernel,
        grid_spec=pltpu.PrefetchScalarGridSpec(
            num_scalar_prefetch=2, grid=(nt,),
            in_specs=[pl.BlockSpec(memory_space=pl.ANY),
                      pl.BlockSpec((TC_MOVE * ROW_CHUNKS, LANES), lambda i, zb, zf: (i, 0))],
            out_specs=pl.BlockSpec(memory_space=pl.ANY),
            scratch_shapes=[pltpu.SMEM((TOP_K * TC_MOVE,), jnp.int32),
                            pltpu.VMEM((T_EXPERT * ROW_CHUNKS, LANES), F32),
                            pltpu.SemaphoreType.DMA, pltpu.SemaphoreType.DMA, pltpu.SemaphoreType.DMA]),
        out_shape=jax.ShapeDtypeStruct((n_slots * ROW_CHUNKS, LANES), F32),
        compiler_params=pltpu.CompilerParams(dimension_semantics=("arbitrary",)),
        name="dispatch_rows",
    )(zero_blk, zero_flag, dest_tiles, h2_rows)


def _expert_kernel(be_ref, nu_ref, xs_ref, wgu_ref, bgu_ref, wd_ref, bd_ref, ys_ref, wgu_s, wd_s):
    j = pl.program_id(0)
    t = T_EXPERT

    @pl.when(j < nu_ref[0])
    def _():
        prev = be_ref[jnp.maximum(j - 1, 0)]

        @pl.when((j == 0) | (be_ref[j] != prev))
        def _():
            wgu_s[...] = wgu_ref[0].astype(BF16)
            wd_s[...] = wd_ref[0].astype(BF16)

        x = jnp.concatenate([xs_ref[pl.ds(c, t, stride=ROW_CHUNKS), :] for c in range(ROW_CHUNKS)], axis=1)
        gu = _dot(x.astype(BF16), wgu_s[...]) + bgu_ref[0]
        g = jnp.minimum(gu[:, :D_MODEL], SWIGLU_LIMIT)
        u = jnp.clip(gu[:, D_MODEL:], -SWIGLU_LIMIT, SWIGLU_LIMIT)
        mid = (u + 1.0) * (g * jax.nn.sigmoid(SWIGLU_ALPHA * g))
        y = _dot(mid.astype(BF16), wd_s[...]) + bd_ref[0]
        for c in range(ROW_CHUNKS):
            ys_ref[pl.ds(c, t, stride=ROW_CHUNKS), :] = y[:, c * LANES:(c + 1) * LANES]


def _expert_ffn(blk_e, n_used, xs_rows, w_gate_up, b_gate_up, w_down, b_down):
    n_blk = blk_e.shape[0]
    rows = T_EXPERT * ROW_CHUNKS
    live = lambda j, be, nu: jnp.minimum(j, nu[0] - 1)
    sink = lambda j, be, nu: jnp.where(j < nu[0], j, n_blk - 1)
    return pl.pallas_call(
        _expert_kernel,
        grid_spec=pltpu.PrefetchScalarGridSpec(
            num_scalar_prefetch=2, grid=(n_blk,),
            in_specs=[pl.BlockSpec((rows, LANES), lambda j, be, nu: (live(j, be, nu), 0)),
                      pl.BlockSpec((1, D_MODEL, 2 * D_MODEL), lambda j, be, nu: (be[live(j, be, nu)], 0, 0)),
                      pl.BlockSpec((1, 1, 2 * D_MODEL), lambda j, be, nu: (be[live(j, be, nu)], 0, 0)),
                      pl.BlockSpec((1, D_MODEL, D_MODEL), lambda j, be, nu: (be[live(j, be, nu)], 0, 0)),
                      pl.BlockSpec((1, 1, D_MODEL), lambda j, be, nu: (be[live(j, be, nu)], 0, 0))],
            out_specs=pl.BlockSpec((rows, LANES), lambda j, be, nu: (sink(j, be, nu), 0)),
            scratch_shapes=[pltpu.VMEM((D_MODEL, 2 * D_MODEL), BF16), pltpu.VMEM((D_MODEL, D_MODEL), BF16)]),
        out_shape=jax.ShapeDtypeStruct(xs_rows.shape, F32),
        compiler_params=pltpu.CompilerParams(dimension_semantics=("arbitrary",), vmem_limit_bytes=VMEM_LIMIT),
        name="expert_ffn",
    )(blk_e, n_used, xs_rows, w_gate_up, b_gate_up.reshape(N_EXPERTS, 1, 2 * D_MODEL),
      w_down, b_down.reshape(N_EXPERTS, 1, D_MODEL))


def _combine_kernel(dest_hbm, ys_hbm, x1_ref, rg_ref, o_ref, idx_s, buf, gate_s, sem_i, sem_r):
    i = pl.program_id(0)
    n_move = TOP_K * TC_MOVE
    idx_copy = pltpu.make_async_copy(dest_hbm.at[i], idx_s, sem_i)
    idx_copy.start()
    idx_copy.wait()

    def issue(jo, carry):
        for u in range(DMA_UNROLL):
            j = jo * DMA_UNROLL + u
            _row_copy(ys_hbm, idx_s[j], buf, j, sem_r).start(priority=u % 2)
        return carry

    lax.fori_loop(0, n_move // DMA_UNROLL, issue, 0)

    def drain(jo, carry):
        for u in range(DMA_UNROLL):
            _row_copy(ys_hbm, 0, buf, 0, sem_r).wait()
        return carry

    lax.fori_loop(0, n_move // DMA_UNROLL, drain, 0)

    for k in range(TOP_K):
        gate_s[k] = jnp.broadcast_to(rg_ref[:, k:k + 1], (TC_MOVE, LANES))
    for c in range(ROW_CHUNKS):
        acc = x1_ref[:, c * LANES:(c + 1) * LANES]
        for k in range(TOP_K):
            rows = buf[pl.ds(k * TC_MOVE * ROW_CHUNKS + c, TC_MOVE, stride=ROW_CHUNKS), :]
            acc = acc + gate_s[k] * rows
        o_ref[:, c * LANES:(c + 1) * LANES] = acc


def _combine(dest_tiles, ys_rows, x1, rg):
    n = x1.shape[0]
    nt = dest_tiles.shape[0]
    return pl.pallas_call(
        _combine_kernel,
        grid=(nt,),
        in_specs=[pl.BlockSpec(memory_space=pl.ANY), pl.BlockSpec(memory_space=pl.ANY),
                  pl.BlockSpec((TC_MOVE, D_MODEL), lambda i: (i, 0)),
                  pl.BlockSpec((TC_MOVE, LANES), lambda i: (i, 0))],
        out_specs=pl.BlockSpec((TC_MOVE, D_MODEL), lambda i: (i, 0)),
        out_shape=jax.ShapeDtypeStruct((n, D_MODEL), F32),
        scratch_shapes=[pltpu.SMEM((TOP_K * TC_MOVE,), jnp.int32),
                        pltpu.VMEM((TOP_K * TC_MOVE * ROW_CHUNKS, LANES), F32),
                        pltpu.VMEM((TOP_K, TC_MOVE, LANES), F32),
                        pltpu.SemaphoreType.DMA, pltpu.SemaphoreType.DMA],
        compiler_params=pltpu.CompilerParams(dimension_semantics=("arbitrary",), vmem_limit_bytes=VMEM_LIMIT),
        name="combine_rows",
    )(dest_tiles, ys_rows, x1, rg)


def _layer(x, ln1_g, w_in, b_forget, b_gate, q_norm_a, k_norm_a, q_norm_b, k_norm_b, rel_bias,
           w_branch_a, w_branch_b, w_out, ln2_g, w_router, b_router, w_gate_up, b_gate_up, w_down, b_down):
    batch, seq, _ = x.shape
    n = batch * seq
    x2 = x.reshape(n, D_MODEL)
    *qkv_groups, qb, kb, vt, lf, gate = _in_projection(x2, batch, seq, ln1_g, w_in, b_forget, b_gate,
                                                       q_norm_a, k_norm_a, q_norm_b, k_norm_b)

    oas, lss = [], []
    for g in range(DSWA_GROUPS):
        o, lse = _dilated_attention(qkv_groups[g], rel_bias, g, batch, seq)
        oas.append(o)
        lss.append(lse)

    obt = _forgetting_attention(qb, kb, lf, vt, batch, seq)

    x1, h2_rows, ri, rg, cnt = _merge_and_route(x2, batch, seq, oas, lss, obt, gate, w_branch_a, w_branch_b,
                                                w_out, ln2_g, w_router, b_router)

    counts = cnt[0, :N_EXPERTS].astype(jnp.int32)
    pcounts = ((counts + T_EXPERT - 1) // T_EXPERT) * T_EXPERT
    pends = jnp.cumsum(pcounts)
    pstarts = pends - pcounts
    idx, rank = ri[:, :TOP_K], ri[:, TOP_K:2 * TOP_K]
    start_of = jnp.sum(jnp.where(idx[..., None] == jnp.arange(N_EXPERTS), pstarts, 0), axis=-1)
    dest = start_of + rank
    nt = n // TC_MOVE
    dest_tiles = jnp.transpose(dest.reshape(nt, TC_MOVE, TOP_K), (0, 2, 1)).reshape(nt, TOP_K * TC_MOVE)
    n_slots = n * TOP_K + N_EXPERTS * T_EXPERT
    n_blk = n_slots // T_EXPERT
    blk_first = jnp.arange(n_blk, dtype=jnp.int32) * T_EXPERT
    blk_e = jnp.minimum(jnp.sum((pends[None, :] <= blk_first[:, None]).astype(jnp.int32), axis=1), N_EXPERTS - 1)
    n_used = (pends[-1:] // T_EXPERT).astype(jnp.int32)
    zero_blk = jnp.maximum(pends // T_EXPERT - 1, 0).astype(jnp.int32)
    zero_flag = (pcounts > 0).astype(jnp.int32)

    xs_rows = _dispatch(dest_tiles, zero_blk, zero_flag, h2_rows, n_slots)
    ys_rows = _expert_ffn(blk_e, n_used, xs_rows, w_gate_up, b_gate_up, w_down, b_down)
    out = _combine(dest_tiles, ys_rows, x1, rg)
    return out.reshape(batch, seq, D_MODEL)


def kernel(x, ln1_g, w_in, b_forget, b_gate, q_norm_a, k_norm_a, q_norm_b, k_norm_b, rel_bias, w_branch_a, w_branch_b, w_out, ln2_g, w_router, b_router, w_gate_up, b_gate_up, w_down, b_down):
    for layer in range(ln1_g.shape[0]):
        x = _layer(x, ln1_g[layer], w_in[layer], b_forget[layer], b_gate[layer], q_norm_a[layer],
                   k_norm_a[layer], q_norm_b[layer], k_norm_b[layer], rel_bias, w_branch_a[layer],
                   w_branch_b[layer], w_out[layer], ln2_g[layer], w_router[layer], b_router[layer],
                   w_gate_up[layer], b_gate_up[layer], w_down[layer], b_down[layer])
    return x
```

```python
import functools
import math

import numpy as np
import jax
import jax.numpy as jnp
from jax import lax
from jax.experimental import pallas as pl
from jax.experimental.pallas import tpu as pltpu

F32 = jnp.float32
BF16 = jnp.bfloat16

D_MODEL = 1024
HEAD_DIM = 64
DSWA_PATTERNS = ((128, 1), (512, 4), (2048, 16))
DSWA_HG = 4
DSWA_GROUPS = len(DSWA_PATTERNS)
DSWA_WIDTH = DSWA_GROUPS * DSWA_HG * HEAD_DIM
DSWA_OUT = DSWA_HG * HEAD_DIM
FOX_HEADS = 8
FOX_WIDTH = FOX_HEADS * HEAD_DIM
ATTN_BLOCK = 128
NUM_BUCKETS = 32
MAX_DISTANCE = 2048
N_EXPERTS = 32
TOP_K = 4
SWIGLU_LIMIT = 7.0
SWIGLU_ALPHA = 1.702
RMS_EPS = 1e-6
NEG_INF = -1e30
LOG2E = 1.4426950408889634

LANES = 128
SUBLANES = 8
ROW_CHUNKS = D_MODEL // LANES
QKV_WIDTH = 3 * DSWA_WIDTH + 3 * FOX_WIDTH
FOX_Q_ONES = (64, 65, 66)
FOX_K_C = (64, 65, 66)
FOX_Q_C = (67, 68, 69)
FOX_K_ONES = (67, 68, 69)
VMEM_LIMIT = 56 * 1024 * 1024

TM_PROJ = 512
PROJ_LOOKAHEAD = 1
TQ_FOX = 512
FOX_HEADS_PER_LOOP = 8
DSWA_CHAINS_PER_STEP = 32
TM_MERGE = 512
MERGE_PARTS = 2
T_EXPERT = 512
TC_MOVE = 1024
DMA_UNROLL = 16


def _dot(a, b):
    return jnp.dot(a, b, preferred_element_type=F32)


def _dot_nt(a, b):
    return lax.dot_general(a, b, (((1,), (1,)), ((), ())), preferred_element_type=F32)


def _inproj_kernel(x_ref, g1_ref, wq_ref, wf_ref, wg_ref, bf_ref, bg_ref, nrm_ref, e_ref, place_ref, pat_ref,
                   qkv0_ref, qkv1_ref, qkv2_ref, qb_ref, kb_ref, vt_ref, lf_ref, gate_ref, stage_ref):
    group_refs = (qkv0_ref, qkv1_ref, qkv2_ref)
    x = x_ref[...]
    ms = jnp.mean(x * x, axis=-1, keepdims=True)
    h = (x * lax.rsqrt(ms + RMS_EPS) * g1_ref[...]).astype(BF16)
    ones_bd = e_ref[...]

    def head_norm(y, gn):
        ss = _dot((y * y).astype(BF16), ones_bd)
        return y * lax.rsqrt(ss * (1.0 / HEAD_DIM) + RMS_EPS) * gn

    n_a = 3 * DSWA_WIDTH // 256

    def qkv_epilogue(c, y):
        if c < 3:
            y = head_norm(y, nrm_ref[0:1, :])
        elif c < 6:
            y = head_norm(y, nrm_ref[1:2, :])
        elif 9 <= c < 11:
            y = head_norm(y, nrm_ref[2:3, :])
        elif 11 <= c < 13:
            y = head_norm(y, nrm_ref[3:4, :])
        if c < n_a:
            g, part = c % DSWA_GROUPS, c // DSWA_GROUPS
            dil = DSWA_PATTERNS[g][1]
            if dil == 1:
                group_refs[g][0, :, part * 256:(part + 1) * 256] = y.astype(BF16)
            else:
                for half in range(2):
                    stage_ref[half] = y[:, half * LANES:(half + 1) * LANES]
                for r in range(dil):
                    for half in range(2):
                        col = r * DSWA_WIDTH + part * 256 + half * LANES
                        rows = stage_ref[half, pl.ds(r, TM_PROJ // dil, stride=dil), :]
                        group_refs[g][0, :, col:col + LANES] = rows.astype(BF16)
        elif c < n_a + 4:
            is_k, half = divmod(c - n_a, 2)
            slots = _dot(y.astype(BF16), place_ref[...]) + pat_ref[is_k:is_k + 1, :]
            (kb_ref if is_k else qb_ref)[:, half * 512:(half + 1) * 512] = slots.astype(BF16)
        else:
            half = c - n_a - 4
            vt_ref[0, half * 256:(half + 1) * 256, :] = y.T.astype(BF16)

    def forget_epilogue(y):
        z = y + bf_ref[...]
        lf_ref[...] = (jnp.minimum(z, 0.0) - jnp.log1p(jnp.exp(-jnp.abs(z)))) * LOG2E

    def gate_epilogue(c, y):
        gate_ref[:, c * 256:(c + 1) * 256] = jax.nn.sigmoid(y + bg_ref[:, c * 256:(c + 1) * 256])

    jobs = [(functools.partial(lambda c: _dot(h, wq_ref[:, c * 256:(c + 1) * 256]), c),
             functools.partial(qkv_epilogue, c)) for c in range(QKV_WIDTH // 256)]
    jobs.append((lambda: _dot(h, wf_ref[...]), forget_epilogue))
    jobs += [(functools.partial(lambda c: _dot(h, wg_ref[:, c * 256:(c + 1) * 256]), c),
              functools.partial(gate_epilogue, c)) for c in range(2 * D_MODEL // 256)]
    pending = [matmul() for matmul, _ in jobs[:PROJ_LOOKAHEAD]]
    for j, (_, epilogue) in enumerate(jobs):
        if j + PROJ_LOOKAHEAD < len(jobs):
            pending.append(jobs[j + PROJ_LOOKAHEAD][0]())
        epilogue(pending.pop(0))


def _in_projection(x2, batch, seq, ln1_g, w_in, b_forget, b_gate, q_norm_a, k_norm_a, q_norm_b, k_norm_b):
    n = x2.shape[0]
    assert seq % TM_PROJ == 0 and all(TM_PROJ % (16 * d) == 0 for _, d in DSWA_PATTERNS)
    tiles_per_seq = seq // TM_PROJ
    wq = w_in[:, :QKV_WIDTH].astype(BF16)
    wf = jnp.pad(w_in[:, QKV_WIDTH:QKV_WIDTH + FOX_HEADS], ((0, 0), (0, LANES - FOX_HEADS))).astype(BF16)
    wg = w_in[:, QKV_WIDTH + FOX_HEADS:].astype(BF16)
    bf = jnp.pad(b_forget, (0, LANES - FOX_HEADS)).reshape(1, LANES)
    bg = b_gate.reshape(1, 2 * D_MODEL)
    scale = HEAD_DIM ** -0.5
    nrm = jnp.stack([jnp.tile(q_norm_a, 4) * scale, jnp.tile(k_norm_a, 4),
                     jnp.tile(q_norm_b, 4) * (scale * LOG2E), jnp.tile(k_norm_b, 4)])
    head_id = np.arange(256) // HEAD_DIM
    ones_bd = jnp.asarray(head_id[:, None] == head_id[None, :], BF16)
    src = np.arange(256)
    place = np.zeros((256, 4 * LANES), np.float32)
    place[src, (src // HEAD_DIM) * LANES + src % HEAD_DIM] = 1.0
    pat = np.zeros((2, 4 * LANES), np.float32)
    for slot in range(4):
        pat[0, [slot * LANES + l for l in FOX_Q_ONES]] = 1.0
        pat[1, [slot * LANES + l for l in FOX_K_ONES]] = 1.0
    const = lambda shape: pl.BlockSpec(shape, lambda i: (0,) * len(shape), pipeline_mode=pl.Buffered(1))
    row = lambda w: pl.BlockSpec((TM_PROJ, w), lambda i: (i, 0))
    seq_tile = lambda i: (i // tiles_per_seq, i % tiles_per_seq)
    group_spec = lambda d: pl.BlockSpec((1, TM_PROJ // d, d * DSWA_WIDTH), lambda i: (*seq_tile(i), 0))
    group_shape = lambda d: jax.ShapeDtypeStruct((batch, seq // d, d * DSWA_WIDTH), BF16)
    vt_spec = pl.BlockSpec((1, FOX_WIDTH, TM_PROJ), lambda i: (seq_tile(i)[0], 0, seq_tile(i)[1]))
    dils = [d for _, d in DSWA_PATTERNS]
    return pl.pallas_call(
        _inproj_kernel,
        grid=(n // TM_PROJ,),
        in_specs=[row(D_MODEL), const((1, D_MODEL)), const((D_MODEL, QKV_WIDTH)), const((D_MODEL, LANES)),
                  const((D_MODEL, 2 * D_MODEL)), const((1, LANES)), const((1, 2 * D_MODEL)),
                  const((4, 256)), const((256, 256)), const((256, 4 * LANES)), const((2, 4 * LANES))],
        out_specs=[group_spec(d) for d in dils] + [row(FOX_HEADS * LANES), row(FOX_HEADS * LANES), vt_spec,
                                                   row(LANES), row(2 * D_MODEL)],
        out_shape=[group_shape(d) for d in dils] + [jax.ShapeDtypeStruct((n, FOX_HEADS * LANES), BF16),
                                                    jax.ShapeDtypeStruct((n, FOX_HEADS * LANES), BF16),
                                                    jax.ShapeDtypeStruct((batch, FOX_WIDTH, seq), BF16),
                                                    jax.ShapeDtypeStruct((n, LANES), F32),
                                                    jax.ShapeDtypeStruct((n, 2 * D_MODEL), F32)],
        scratch_shapes=[pltpu.VMEM((2, TM_PROJ, LANES), F32)],
        compiler_params=pltpu.CompilerParams(dimension_semantics=("arbitrary",), vmem_limit_bytes=VMEM_LIMIT),
        name="in_projection",
    )(x2, ln1_g.reshape(1, D_MODEL), wq, wf, wg, bf, bg, nrm, ones_bd, jnp.asarray(place, BF16), jnp.asarray(pat))


def _t5_bucket(n):
    max_exact = NUM_BUCKETS // 2
    n_safe = np.maximum(n, 1).astype(np.float64)
    large = max_exact + (np.log(n_safe / max_exact) / np.log(MAX_DISTANCE / max_exact)
                         * (NUM_BUCKETS - max_exact)).astype(np.int64)
    large = np.minimum(large, NUM_BUCKETS - 1)
    return np.where(n < max_exact, n, large).astype(np.int32)


def _dswa_kernel(cur_ref, prev_ref, bias_ref, o_ref, l_ref, *, blocks, residues):
    first = pl.program_id(2) == 0
    items = [(r, j, h) for r in range(residues) for j in range(blocks) for h in range(DSWA_HG)]
    rows = lambda j: slice(j * ATTN_BLOCK, (j + 1) * ATTN_BLOCK)
    col = lambda r, part, h: slice(r * DSWA_WIDTH + part * 256 + h * HEAD_DIM,
                                   r * DSWA_WIDTH + part * 256 + (h + 1) * HEAD_DIM)

    def before(r, j, part, h):
        return prev_ref[0, :, col(r, part, h)] if j == 0 else cur_ref[0, rows(j - 1), col(r, part, h)]

    sps, scs = [], []
    for r, j, h in items:
        q = cur_ref[0, rows(j), col(r, 0, h)]
        sp = _dot_nt(q, before(r, j, 1, h)) + bias_ref[h, :, :ATTN_BLOCK]
        if j == 0:
            sp = jnp.where(first, NEG_INF, sp)
        sps.append(sp)
        scs.append(_dot_nt(q, cur_ref[0, rows(j), col(r, 1, h)]) + bias_ref[h, :, ATTN_BLOCK:])
    ms = [jnp.maximum(jnp.max(sp, axis=-1, keepdims=True), jnp.max(sc, axis=-1, keepdims=True))
          for sp, sc in zip(sps, scs)]
    pps = [jnp.exp(sp - m) for sp, m in zip(sps, ms)]
    pcs = [jnp.exp(sc - m) for sc, m in zip(scs, ms)]
    dens = [jnp.sum(pp, axis=-1, keepdims=True) + jnp.sum(pc, axis=-1, keepdims=True) for pp, pc in zip(pps, pcs)]
    for (r, j, h), pp, pc, m, den in zip(items, pps, pcs, ms, dens):
        o = _dot(pp.astype(BF16), before(r, j, 2, h)) + _dot(pc.astype(BF16), cur_ref[0, rows(j), col(r, 2, h)])
        out_col = slice(r * DSWA_OUT + h * HEAD_DIM, r * DSWA_OUT + (h + 1) * HEAD_DIM)
        o_ref[0, rows(j), out_col] = o / den
        l_ref[0, rows(j), out_col] = jnp.broadcast_to(m + jnp.log(den), (ATTN_BLOCK, HEAD_DIM))


def _dilated_attention(qkv3, rel_bias, g, batch, seq):
    window, dil = DSWA_PATTERNS[g]
    steps = window // dil
    assert steps == ATTN_BLOCK
    length = seq // dil
    assert length % ATTN_BLOCK == 0
    nblk = length // ATTN_BLOCK
    span = 2 * ATTN_BLOCK
    back = np.arange(ATTN_BLOCK)[:, None] + steps - np.arange(span)[None, :]
    valid = (back >= 0) & (back <= steps)
    bucket = _t5_bucket(np.clip(back, 0, steps) * dil)
    cols = rel_bias[:, g * DSWA_HG:(g + 1) * DSWA_HG]
    onehot = np.asarray(bucket.reshape(-1)[:, None] == np.arange(NUM_BUCKETS)[None, :], np.float32)
    bias = jnp.dot(jnp.asarray(onehot), cols.astype(F32), precision=lax.Precision.HIGHEST)
    bias = jnp.transpose(bias.reshape(ATTN_BLOCK, span, DSWA_HG), (2, 0, 1))
    bias = jnp.where(valid[None], bias, NEG_INF)

    nb = math.gcd(nblk, DSWA_CHAINS_PER_STEP // DSWA_HG)
    nr = math.gcd(dil, DSWA_CHAINS_PER_STEP // DSWA_HG // nb)
    before = lambda n: jnp.maximum(n * nb - 1, 0)
    cur_spec = pl.BlockSpec((1, nb * ATTN_BLOCK, nr * DSWA_WIDTH), lambda b, r, n: (b, n, r))
    prev_spec = pl.BlockSpec((1, ATTN_BLOCK, nr * DSWA_WIDTH), lambda b, r, n: (b, before(n), r))
    bias_spec = pl.BlockSpec((DSWA_HG, ATTN_BLOCK, span), lambda b, r, n: (0, 0, 0))
    o_spec = pl.BlockSpec((1, nb * ATTN_BLOCK, nr * DSWA_OUT), lambda b, r, n: (b, n, r))
    return pl.pallas_call(
        functools.partial(_dswa_kernel, blocks=nb, residues=nr),
        grid=(batch, dil // nr, nblk // nb),
        in_specs=[cur_spec, prev_spec, bias_spec],
        out_specs=[o_spec, o_spec],
        out_shape=[jax.ShapeDtypeStruct((batch, length, dil * DSWA_OUT), F32)] * 2,
        compiler_params=pltpu.CompilerParams(dimension_semantics=("arbitrary",) * 3),
        name=f"dilated_attention_{g}",
    )(qkv3, qkv3, bias)


def _fox_kernel(q_ref, k_ref, lf_ref, vt_ref, put_ref, o_ref, kaug_s, carry_s):
    qi = pl.program_id(1)
    tq = TQ_FOX
    krow = lax.broadcasted_iota(jnp.int32, (tq, tq), 0)
    qcol = lax.broadcasted_iota(jnp.int32, (tq, tq), 1)
    causal = krow <= qcol

    @pl.when(qi == 0)
    def _():
        carry_s[...] = jnp.zeros_like(carry_s)

    tri = jnp.where(krow >= qcol, 1.0, 0.0).astype(BF16)
    lane = lax.broadcasted_iota(jnp.int32, (tq, LANES), 1)

    def pack3(v):
        hi, mid, lo = _split3(v)
        packed = jnp.where(lane < FOX_HEADS, hi,
                           jnp.where(lane < 2 * FOX_HEADS, pltpu.roll(mid, FOX_HEADS, axis=1),
                                     jnp.where(lane < 3 * FOX_HEADS, pltpu.roll(lo, 2 * FOX_HEADS, axis=1), 0.0)))
        return packed.astype(BF16)

    sums = _dot(tri, pack3(lf_ref[...]))
    c = (carry_s[...] + sums + pltpu.roll(sums, LANES - FOX_HEADS, axis=1)
         + pltpu.roll(sums, LANES - 2 * FOX_HEADS, axis=1))
    carry_s[...] = c[tq - 1:tq, :]
    bias = _dot(pack3(c), put_ref[...])
    q_all = q_ref[...] + bias[:, :FOX_HEADS * LANES].astype(BF16)
    kaug_s[pl.ds(pl.multiple_of(qi * tq, tq), tq), :] = k_ref[...] + bias[:, FOX_HEADS * LANES:].astype(BF16)

    for hg in range(FOX_HEADS // FOX_HEADS_PER_LOOP):
        heads = [hg * FOX_HEADS_PER_LOOP + i for i in range(FOX_HEADS_PER_LOOP)]
        qs = [q_all[:, h * LANES:(h + 1) * LANES] for h in heads]

        def step(kb, carry, masked):
            ks = pl.multiple_of(kb * tq, tq)
            idx = range(len(heads))
            sts = []
            for i, h in enumerate(heads):
                st = _dot_nt(kaug_s[pl.ds(ks, tq), h * LANES:(h + 1) * LANES], qs[i])
                sts.append(jnp.where(causal, st, NEG_INF) if masked else st)
            m_new = [jnp.maximum(carry[i][0], jnp.max(sts[i], axis=0, keepdims=True)) for i in idx]
            alpha = [jnp.exp2(carry[i][0] - m_new[i]) for i in idx]
            ps = [jnp.exp2(sts[i] - m_new[i]) for i in idx]
            ls = [alpha[i] * carry[i][1] + jnp.sum(ps[i], axis=0, keepdims=True) for i in idx]
            out = []
            for i, h in enumerate(heads):
                vt = vt_ref[0, h * HEAD_DIM:(h + 1) * HEAD_DIM, pl.ds(ks, tq)]
                out.append((m_new[i], ls[i], alpha[i] * carry[i][2] + _dot(vt, ps[i].astype(BF16))))
            return tuple(out)

        init = tuple((jnp.full((1, tq), NEG_INF, F32), jnp.zeros((1, tq), F32), jnp.zeros((HEAD_DIM, tq), F32))
                     for _ in heads)
        carry = lax.fori_loop(0, qi, lambda kb, c: step(kb, c, False), init)
        carry = step(qi, carry, True)
        for i, h in enumerate(heads):
            _, l, acc = carry[i]
            o_ref[0, h * HEAD_DIM:(h + 1) * HEAD_DIM, :] = (acc / l).astype(BF16)


def _top16(v):
    bits = lax.bitcast_convert_type(v, jnp.uint32) & jnp.uint32(0xFFFF0000)
    return lax.bitcast_convert_type(bits, F32)


def _split3(c):
    hi = _top16(c)
    r1 = c - hi
    mid = _top16(r1)
    return hi, mid, r1 - mid


def _forgetting_attention(qb, kb, lf, vt, batch, seq):
    nq = seq // TQ_FOX
    put = np.zeros((LANES, 2 * FOX_HEADS * LANES), np.float32)
    for h in range(FOX_HEADS):
        for j in range(3):
            put[j * FOX_HEADS + h, h * LANES + FOX_Q_C[j]] = 1.0
            put[j * FOX_HEADS + h, FOX_HEADS * LANES + h * LANES + FOX_K_C[j]] = -1.0
    tile = lambda w: pl.BlockSpec((TQ_FOX, w), lambda b, i: (b * nq + i, 0))
    return pl.pallas_call(
        _fox_kernel,
        grid=(batch, nq),
        in_specs=[tile(FOX_HEADS * LANES), tile(FOX_HEADS * LANES), tile(LANES),
                  pl.BlockSpec((1, FOX_WIDTH, seq), lambda b, i: (b, 0, 0)),
                  pl.BlockSpec(put.shape, lambda b, i: (0, 0), pipeline_mode=pl.Buffered(1))],
        out_specs=pl.BlockSpec((1, FOX_WIDTH, TQ_FOX), lambda b, i: (b, 0, i)),
        out_shape=jax.ShapeDtypeStruct((batch, FOX_WIDTH, seq), BF16),
        scratch_shapes=[pltpu.VMEM((seq, FOX_HEADS * LANES), BF16), pltpu.VMEM((1, LANES), F32)],
        compiler_params=pltpu.CompilerParams(dimension_semantics=("arbitrary", "arbitrary"),
                                             vmem_limit_bytes=VMEM_LIMIT),
        name="forgetting_attention",
    )(qb, kb, lf, vt, jnp.asarray(put, BF16))


def _merge_kernel(x_ref, oa0_ref, oa1_ref, oa2_ref, ls0_ref, ls1_ref, ls2_ref, obt_ref, gate_ref,
                  wa_ref, wb_ref, wo_ref, g2_ref, wrh_ref, wrl_ref, br_ref,
                  x1_ref, h2_ref, ri_ref, rg_ref, cnt_ref, run_s, stage_s):
    tm = TM_MERGE

    @pl.when(pl.program_id(0) == 0)
    def _():
        run_s[...] = jnp.zeros_like(run_s)

    tp = tm // MERGE_PARTS
    parts = range(MERGE_PARTS)
    rows = [slice(p * tp, (p + 1) * tp) for p in parts]

    def token_order(ref, g, stage):
        dil = DSWA_PATTERNS[g][1]
        if dil == 1:
            return [ref[0, rs, :] for rs in rows]
        for r in range(dil):
            for half in range(2):
                col = r * DSWA_OUT + half * LANES
                stage[half, pl.ds(r, tm // dil, stride=dil), :] = ref[0, :, col:col + LANES]
        return [jnp.concatenate([stage[0, rs, :], stage[1, rs, :]], axis=1) for rs in rows]

    yb = [lax.dot_general(obt_ref[0, :, rs], wb_ref[...], (((0,), (0,)), ((), ())), preferred_element_type=F32)
          for rs in rows]
    ls = [token_order(ref, g, stage_s.at[g]) for g, ref in enumerate((ls0_ref, ls1_ref, ls2_ref))]
    os_ = [token_order(ref, g, stage_s.at[DSWA_GROUPS + g]) for g, ref in enumerate((oa0_ref, oa1_ref, oa2_ref))]
    o_a = []
    for p in parts:
        lm = jnp.maximum(jnp.maximum(ls[0][p], ls[1][p]), ls[2][p])
        e0, e1, e2 = jnp.exp(ls[0][p] - lm), jnp.exp(ls[1][p] - lm), jnp.exp(ls[2][p] - lm)
        o_a.append((e0 * os_[0][p] + e1 * os_[1][p] + e2 * os_[2][p]) / (e0 + e1 + e2))
    ya = [_dot(o_a[p].astype(BF16), wa_ref[...]) for p in parts]
    merged = [gate_ref[rows[p], :D_MODEL] * ya[p] + gate_ref[rows[p], D_MODEL:] * yb[p] for p in parts]
    x1 = [x_ref[rows[p], :] + _dot(merged[p].astype(BF16), wo_ref[...]) for p in parts]
    for p in parts:
        x1_ref[rows[p], :] = x1[p]

    h2 = [x1[p] * lax.rsqrt(jnp.mean(x1[p] * x1[p], axis=-1, keepdims=True) + RMS_EPS) * g2_ref[...] for p in parts]
    for p in parts:
        for c in range(ROW_CHUNKS):
            h2_ref[pl.ds(p * tp * ROW_CHUNKS + c, tp, stride=ROW_CHUNKS), :] = h2[p][:, c * LANES:(c + 1) * LANES]

    hi = [h2[p].astype(BF16) for p in parts]
    lo = [(h2[p] - hi[p].astype(F32)).astype(BF16) for p in parts]
    work = [_dot(hi[p], wrh_ref[...]) + _dot(lo[p], wrh_ref[...]) + _dot(hi[p], wrl_ref[...]) + br_ref[...]
            for p in parts]

    lane = lax.broadcasted_iota(jnp.int32, (tp, LANES), 1).astype(F32)
    top_v, top_i = [[] for _ in parts], [[] for _ in parts]
    for _ in range(TOP_K):
        mk = [jnp.max(work[p], axis=-1, keepdims=True) for p in parts]
        ik = [jnp.min(jnp.where(work[p] == mk[p], lane, float(LANES)), axis=-1, keepdims=True) for p in parts]
        for p in parts:
            top_v[p].append(mk[p])
            top_i[p].append(ik[p])
        work = [jnp.where(lane == ik[p], -jnp.inf, work[p]) for p in parts]

    r_i = lax.broadcasted_iota(jnp.int32, (tp, tp), 0)
    c_i = lax.broadcasted_iota(jnp.int32, (tp, tp), 1)
    tri = jnp.where(c_i < r_i, 1.0, 0.0).astype(BF16)
    onehot = []
    for p in parts:
        oh = jnp.zeros((tp, LANES), F32)
        for ik in top_i[p]:
            oh = oh + jnp.where(lane == ik, 1.0, 0.0)
        onehot.append(oh)
    within = [_dot(tri, onehot[p].astype(BF16)) for p in parts]
    run = run_s[...]
    for p in parts:
        before = within[p] + run
        ex = [jnp.exp(v - top_v[p][0]) for v in top_v[p]]
        den = ex[0] + ex[1] + ex[2] + ex[3]
        ri = jnp.zeros((tp, LANES), jnp.int32)
        rg = jnp.zeros((tp, LANES), F32)
        for k in range(TOP_K):
            rank = jnp.sum(jnp.where(lane == top_i[p][k], before, 0.0), axis=-1, keepdims=True)
            ri = jnp.where(lane == k, top_i[p][k].astype(jnp.int32), ri)
            ri = jnp.where(lane == TOP_K + k, rank.astype(jnp.int32), ri)
            rg = jnp.where(lane == k, ex[k] / den, rg)
        ri_ref[rows[p], :] = ri
        rg_ref[rows[p], :] = rg
        run = run + jnp.sum(onehot[p], axis=0, keepdims=True)
    run_s[...] = run
    cnt_ref[...] = run


def _merge_and_route(x2, batch, seq, oas, lss, obt, gate, w_branch_a, w_branch_b, w_out, ln2_g, w_router,
                     b_router):
    n = x2.shape[0]
    assert seq % TM_MERGE == 0 and all(TM_MERGE % (SUBLANES * d) == 0 for _, d in DSWA_PATTERNS)
    tiles_per_seq = seq // TM_MERGE
    seq_tile = lambda i: (i // tiles_per_seq, i % tiles_per_seq)
    group_specs = [pl.BlockSpec((1, TM_MERGE // d, d * DSWA_OUT), lambda i: (*seq_tile(i), 0))
                   for _, d in DSWA_PATTERNS]
    obt_spec = pl.BlockSpec((1, FOX_WIDTH, TM_MERGE), lambda i: (seq_tile(i)[0], 0, seq_tile(i)[1]))
    wr = jnp.pad(w_router, ((0, 0), (0, LANES - N_EXPERTS)))
    wr_top = _top16(wr)
    wr_hi = wr_top.astype(BF16)
    wr_lo = (wr - wr_top).astype(BF16)
    br = jnp.concatenate([b_router, jnp.full((LANES - N_EXPERTS,), NEG_INF, F32)]).reshape(1, LANES)
    const = lambda shape: pl.BlockSpec(shape, lambda i: (0,) * len(shape), pipeline_mode=pl.Buffered(1))
    row = lambda w: pl.BlockSpec((TM_MERGE, w), lambda i: (i, 0))
    return pl.pallas_call(
        _merge_kernel,
        grid=(n // TM_MERGE,),
        in_specs=[row(D_MODEL)] + group_specs + group_specs + [obt_spec, row(2 * D_MODEL),
                  const((DSWA_OUT, D_MODEL)), const((FOX_WIDTH, D_MODEL)), const((D_MODEL, D_MODEL)),
                  const((1, D_MODEL)), const((D_MODEL, LANES)), const((D_MODEL, LANES)), const((1, LANES))],
        out_specs=[row(D_MODEL), pl.BlockSpec((TM_MERGE * ROW_CHUNKS, LANES), lambda i: (i, 0)),
                   row(LANES), row(LANES), pl.BlockSpec((1, LANES), lambda i: (0, 0))],
        out_shape=[jax.ShapeDtypeStruct((n, D_MODEL), F32),
                   jax.ShapeDtypeStruct((n * ROW_CHUNKS, LANES), F32),
                   jax.ShapeDtypeStruct((n, LANES), jnp.int32),
                   jax.ShapeDtypeStruct((n, LANES), F32),
                   jax.ShapeDtypeStruct((1, LANES), F32)],
        scratch_shapes=[pltpu.VMEM((1, LANES), F32), pltpu.VMEM((2 * DSWA_GROUPS, 2, TM_MERGE, LANES), F32)],
        compiler_params=pltpu.CompilerParams(dimension_semantics=("arbitrary",), vmem_limit_bytes=VMEM_LIMIT),
        name="merge_and_route",
    )(x2, *oas, *lss, obt, gate, w_branch_a.astype(BF16), w_branch_b.astype(BF16), w_out.astype(BF16),
      ln2_g.reshape(1, D_MODEL), wr_hi, wr_lo, br)


def _row_copy(src, src_row, dst, dst_row, sem):
    return pltpu.make_async_copy(src.at[pl.ds(pl.multiple_of(src_row * ROW_CHUNKS, ROW_CHUNKS), ROW_CHUNKS)],
                                 dst.at[pl.ds(pl.multiple_of(dst_row * ROW_CHUNKS, ROW_CHUNKS), ROW_CHUNKS)], sem)


def _dispatch_kernel(zb_ref, zf_ref, dest_hbm, h2_ref, xs_hbm, idx_s, zeros_v, sem_i, sem_z, sem_r):
    i = pl.program_id(0)
    n_move = TOP_K * TC_MOVE
    idx_copy = pltpu.make_async_copy(dest_hbm.at[i], idx_s, sem_i)
    idx_copy.start()

    def zero_copy(e):
        blk = pl.multiple_of(zb_ref[e] * (T_EXPERT * ROW_CHUNKS), T_EXPERT * ROW_CHUNKS)
        return pltpu.make_async_copy(zeros_v, xs_hbm.at[pl.ds(blk, T_EXPERT * ROW_CHUNKS)], sem_z)

    @pl.when(i == 0)
    def _():
        zeros_v[...] = jnp.zeros_like(zeros_v)
        for e in range(N_EXPERTS):
            @pl.when(zf_ref[e] > 0)
            def _():
                zero_copy(e).start()
        for e in range(N_EXPERTS):
            @pl.when(zf_ref[e] > 0)
            def _():
                zero_copy(e).wait()

    idx_copy.wait()

    for k in range(TOP_K):
        def issue(to, carry, k=k):
            t0 = to * DMA_UNROLL
            for u in range(DMA_UNROLL):
                _row_copy(h2_ref, t0 + u, xs_hbm, idx_s[k * TC_MOVE + t0 + u], sem_r).start(priority=u % 2)
            return carry

        lax.fori_loop(0, TC_MOVE // DMA_UNROLL, issue, 0)

    def drain(jo, carry):
        for u in range(DMA_UNROLL):
            _row_copy(h2_ref, 0, xs_hbm, 0, sem_r).wait()
        return carry

    lax.fori_loop(0, n_move // DMA_UNROLL, drain, 0)


def _dispatch(dest_tiles, zero_blk, zero_flag, h2_rows, n_slots):
    nt = dest_tiles.shape[0]
    return pl.pallas_call(
        _dispatch_kernel,
        grid_spec=pltpu.PrefetchScalarGridSpec(
            num_scalar_prefetch=2, grid=(nt,),
            in_specs=[pl.BlockSpec(memory_space=pl.ANY),
                      pl.BlockSpec((TC_MOVE * ROW_CHUNKS, LANES), lambda i, zb, zf: (i, 0))],
            out_specs=pl.BlockSpec(memory_space=pl.ANY),
            scratch_shapes=[pltpu.SMEM((TOP_K * TC_MOVE,), jnp.int32),
                            pltpu.VMEM((T_EXPERT * ROW_CHUNKS, LANES), F32),
                            pltpu.SemaphoreType.DMA, pltpu.SemaphoreType.DMA, pltpu.SemaphoreType.DMA]),
        out_shape=jax.ShapeDtypeStruct((n_slots * ROW_CHUNKS, LANES), F32),
        compiler_params=pltpu.CompilerParams(dimension_semantics=("arbitrary",)),
        name="dispatch_rows",
    )(zero_blk, zero_flag, dest_tiles, h2_rows)


def _expert_kernel(be_ref, nu_ref, xs_ref, wgu_ref, bgu_ref, wd_ref, bd_ref, ys_ref, wgu_s, wd_s):
    j = pl.program_id(0)
    t = T_EXPERT

    @pl.when(j < nu_ref[0])
    def _():
        prev = be_ref[jnp.maximum(j - 1, 0)]

        @pl.when((j == 0) | (be_ref[j] != prev))
        def _():
            wgu_s[...] = wgu_ref[0].astype(BF16)
            wd_s[...] = wd_ref[0].astype(BF16)

        x = jnp.concatenate([xs_ref[pl.ds(c, t, stride=ROW_CHUNKS), :] for c in range(ROW_CHUNKS)], axis=1)
        gu = _dot(x.astype(BF16), wgu_s[...]) + bgu_ref[0]
        g = jnp.minimum(gu[:, :D_MODEL], SWIGLU_LIMIT)
        u = jnp.clip(gu[:, D_MODEL:], -SWIGLU_LIMIT, SWIGLU_LIMIT)
        mid = (u + 1.0) * (g * jax.nn.sigmoid(SWIGLU_ALPHA * g))
        y = _dot(mid.astype(BF16), wd_s[...]) + bd_ref[0]
        for c in range(ROW_CHUNKS):
            ys_ref[pl.ds(c, t, stride=ROW_CHUNKS), :] = y[:, c * LANES:(c + 1) * LANES]


def _expert_ffn(blk_e, n_used, xs_rows, w_gate_up, b_gate_up, w_down, b_down):
    n_blk = blk_e.shape[0]
    rows = T_EXPERT * ROW_CHUNKS
    live = lambda j, be, nu: jnp.minimum(j, nu[0] - 1)
    sink = lambda j, be, nu: jnp.where(j < nu[0], j, n_blk - 1)
    return pl.pallas_call(
        _expert_kernel,
        grid_spec=pltpu.PrefetchScalarGridSpec(
            num_scalar_prefetch=2, grid=(n_blk,),
            in_specs=[pl.BlockSpec((rows, LANES), lambda j, be, nu: (live(j, be, nu), 0)),
                      pl.BlockSpec((1, D_MODEL, 2 * D_MODEL), lambda j, be, nu: (be[live(j, be, nu)], 0, 0)),
                      pl.BlockSpec((1, 1, 2 * D_MODEL), lambda j, be, nu: (be[live(j, be, nu)], 0, 0)),
                      pl.BlockSpec((1, D_MODEL, D_MODEL), lambda j, be, nu: (be[live(j, be, nu)], 0, 0)),
                      pl.BlockSpec((1, 1, D_MODEL), lambda j, be, nu: (be[live(j, be, nu)], 0, 0))],
            out_specs=pl.BlockSpec((rows, LANES), lambda j, be, nu: (sink(j, be, nu), 0)),
            scratch_shapes=[pltpu.VMEM((D_MODEL, 2 * D_MODEL), BF16), pltpu.VMEM((D_MODEL, D_MODEL), BF16)]),
        out_shape=jax.ShapeDtypeStruct(xs_rows.shape, F32),
        compiler_params=pltpu.CompilerParams(dimension_semantics=("arbitrary",), vmem_limit_bytes=VMEM_LIMIT),
        name="expert_ffn",
    )(blk_e, n_used, xs_rows, w_gate_up, b_gate_up.reshape(N_EXPERTS, 1, 2 * D_MODEL),
      w_down, b_down.reshape(N_EXPERTS, 1, D_MODEL))


def _combine_kernel(dest_hbm, ys_hbm, x1_ref, rg_ref, o_ref, idx_s, buf, gate_s, sem_i, sem_r):
    i = pl.program_id(0)
    n_move = TOP_K * TC_MOVE
    idx_copy = pltpu.make_async_copy(dest_hbm.at[i], idx_s, sem_i)
    idx_copy.start()
    idx_copy.wait()

    def issue(jo, carry):
        for u in range(DMA_UNROLL):
            j = jo * DMA_UNROLL + u
            _row_copy(ys_hbm, idx_s[j], buf, j, sem_r).start(priority=u % 2)
        return carry

    lax.fori_loop(0, n_move // DMA_UNROLL, issue, 0)

    def drain(jo, carry):
        for u in range(DMA_UNROLL):
            _row_copy(ys_hbm, 0, buf, 0, sem_r).wait()
        return carry

    lax.fori_loop(0, n_move // DMA_UNROLL, drain, 0)

    for k in range(TOP_K):
        gate_s[k] = jnp.broadcast_to(rg_ref[:, k:k + 1], (TC_MOVE, LANES))
    for c in range(ROW_CHUNKS):
        acc = x1_ref[:, c * LANES:(c + 1) * LANES]
        for k in range(TOP_K):
            rows = buf[pl.ds(k * TC_MOVE * ROW_CHUNKS + c, TC_MOVE, stride=ROW_CHUNKS), :]
            acc = acc + gate_s[k] * rows
        o_ref[:, c * LANES:(c + 1) * LANES] = acc


def _combine(dest_tiles, ys_rows, x1, rg):
    n = x1.shape[0]
    nt = dest_tiles.shape[0]
    return pl.pallas_call(
        _combine_kernel,
        grid=(nt,),
        in_specs=[pl.BlockSpec(memory_space=pl.ANY), pl.BlockSpec(memory_space=pl.ANY),
                  pl.BlockSpec((TC_MOVE, D_MODEL), lambda i: (i, 0)),
                  pl.BlockSpec((TC_MOVE, LANES), lambda i: (i, 0))],
        out_specs=pl.BlockSpec((TC_MOVE, D_MODEL), lambda i: (i, 0)),
        out_shape=jax.ShapeDtypeStruct((n, D_MODEL), F32),
        scratch_shapes=[pltpu.SMEM((TOP_K * TC_MOVE,), jnp.int32),
                        pltpu.VMEM((TOP_K * TC_MOVE * ROW_CHUNKS, LANES), F32),
                        pltpu.VMEM((TOP_K, TC_MOVE, LANES), F32),
                        pltpu.SemaphoreType.DMA, pltpu.SemaphoreType.DMA],
        compiler_params=pltpu.CompilerParams(dimension_semantics=("arbitrary",), vmem_limit_bytes=VMEM_LIMIT),
        name="combine_rows",
    )(dest_tiles, ys_rows, x1, rg)


def _layer(x, ln1_g, w_in, b_forget, b_gate, q_norm_a, k_norm_a, q_norm_b, k_norm_b, rel_bias,
           w_branch_a, w_branch_b, w_out, ln2_g, w_router, b_router, w_gate_up, b_gate_up, w_down, b_down):
    batch, seq, _ = x.shape
    n = batch * seq
    x2 = x.reshape(n, D_MODEL)
    *qkv_groups, qb, kb, vt, lf, gate = _in_projection(x2, batch, seq, ln1_g, w_in, b_forget, b_gate,
                                                       q_norm_a, k_norm_a, q_norm_b, k_norm_b)

    oas, lss = [], []
    for g in range(DSWA_GROUPS):
        o, lse = _dilated_attention(qkv_groups[g], rel_bias, g, batch, seq)
        oas.append(o)
        lss.append(lse)

    obt = _forgetting_attention(qb, kb, lf, vt, batch, seq)

    x1, h2_rows, ri, rg, cnt = _merge_and_route(x2, batch, seq, oas, lss, obt, gate, w_branch_a, w_branch_b,
                                                w_out, ln2_g, w_router, b_router)

    counts = cnt[0, :N_EXPERTS].astype(jnp.int32)
    pcounts = ((counts + T_EXPERT - 1) // T_EXPERT) * T_EXPERT
    pends = jnp.cumsum(pcounts)
    pstarts = pends - pcounts
    idx, rank = ri[:, :TOP_K], ri[:, TOP_K:2 * TOP_K]
    start_of = jnp.sum(jnp.where(idx[..., None] == jnp.arange(N_EXPERTS), pstarts, 0), axis=-1)
    dest = start_of + rank
    nt = n // TC_MOVE
    dest_tiles = jnp.transpose(dest.reshape(nt, TC_MOVE, TOP_K), (0, 2, 1)).reshape(nt, TOP_K * TC_MOVE)
    n_slots = n * TOP_K + N_EXPERTS * T_EXPERT
    n_blk = n_slots // T_EXPERT
    blk_first = jnp.arange(n_blk, dtype=jnp.int32) * T_EXPERT
    blk_e = jnp.minimum(jnp.sum((pends[None, :] <= blk_first[:, None]).astype(jnp.int32), axis=1), N_EXPERTS - 1)
    n_used = (pends[-1:] // T_EXPERT).astype(jnp.int32)
    zero_blk = jnp.maximum(pends // T_EXPERT - 1, 0).astype(jnp.int32)
    zero_flag = (pcounts > 0).astype(jnp.int32)

    xs_rows = _dispatch(dest_tiles, zero_blk, zero_flag, h2_rows, n_slots)
    ys_rows = _expert_ffn(blk_e, n_used, xs_rows, w_gate_up, b_gate_up, w_down, b_down)
    out = _combine(dest_tiles, ys_rows, x1, rg)
    return out.reshape(batch, seq, D_MODEL)


def kernel(x, ln1_g, w_in, b_forget, b_gate, q_norm_a, k_norm_a, q_norm_b, k_norm_b, rel_bias, w_branch_a, w_branch_b, w_out, ln2_g, w_router, b_router, w_gate_up, b_gate_up, w_down, b_down):
    for layer in range(ln1_g.shape[0]):
        x = _layer(x, ln1_g[layer], w_in[layer], b_forget[layer], b_gate[layer], q_norm_a[layer],
                   k_norm_a[layer], q_norm_b[layer], k_norm_b[layer], rel_bias, w_branch_a[layer],
                   w_branch_b[layer], w_out[layer], ln2_g[layer], w_router[layer], b_router[layer],
                   w_gate_up[layer], b_gate_up[layer], w_down[layer], b_down[layer])
    return x
```
